```python
import jax, jax.numpy as jnp
from jax import lax
import numpy as np

D_MODEL = 1024
BATCH = 16
SEQ = 2048
DEPTH = 1
DEC_BATCH = 2
DEC_SEQ = 8192
PAST_LEN = 128

N_MEM = 256
CONV_WIDTH = 768
CONV_K = 3
ATT_HEAD_DIM = 64
DIL_GROUPS = ((128, 1), (512, 4), (2048, 16))
HEADS_PER_GROUP = 4
N_ATT_HEADS = HEADS_PER_GROUP * len(DIL_GROUPS)
ATT_WIDTH = N_ATT_HEADS * ATT_HEAD_DIM
ATT_OUT_WIDTH = HEADS_PER_GROUP * ATT_HEAD_DIM
DIL_BLOCK = 64
XATT_HEADS = 4
XATT_HEAD_DIM = 128
XATT_WIDTH = XATT_HEADS * XATT_HEAD_DIM
N_BRANCH = 3
IN_COLS = 3 * CONV_WIDTH + 3 * ATT_WIDTH + XATT_WIDTH + N_BRANCH * D_MODEL
N_EXPERTS = 16
EC_CAPACITY = 2
D_FF_EXPERT = 1024
ALIBI_MAX_EXP = 8.0
EPS = 1e-6
NEG_INF = -1e30

kernel_name = "hybrid_conv_dilattn_memxattn_ec_encoder"


def rmsnorm(x, g):
    xf = x.astype(jnp.float32)
    y = xf * lax.rsqrt(jnp.mean(xf * xf, axis=-1, keepdims=True) + EPS)
    return (y * g.astype(jnp.float32)).astype(x.dtype)


def alibi_slopes(n):
    return jnp.exp2(-ALIBI_MAX_EXP * jnp.arange(1, n + 1, dtype=jnp.float32) / n)


def short_conv(u, w):
    c = u.shape[-1]
    return lax.conv_general_dilated(
        u, w[:, None, :].astype(u.dtype), window_strides=(1,),
        padding=[(CONV_K // 2, CONV_K // 2)],
        dimension_numbers=("NWC", "WIO", "NWC"), feature_group_count=c)


def dilated_window_attention(q, k, v, dilation, radius, slopes):
    B, S, H, E = q.shape
    L = S // dilation
    nb = -(-L // DIL_BLOCK)
    pad = nb * DIL_BLOCK - L

    def to_sub(t):
        return t.reshape(B, L, dilation, H, E).transpose(0, 2, 1, 3, 4)

    qb = jnp.pad(to_sub(q), ((0, 0), (0, 0), (0, pad), (0, 0), (0, 0)))
    qb = qb.reshape(B, dilation, nb, DIL_BLOCK, H, E)

    def key_windows(t):
        tp = jnp.pad(to_sub(t), ((0, 0), (0, 0), (DIL_BLOCK, pad + DIL_BLOCK), (0, 0), (0, 0)))
        tp = tp.reshape(B, dilation, nb + 2, DIL_BLOCK, H, E)
        return jnp.concatenate([tp[:, :, :-2], tp[:, :, 1:-1], tp[:, :, 2:]], axis=3)

    kw = key_windows(k)
    vw = key_windows(v)

    blk = jnp.arange(nb)[:, None, None]
    a = jnp.arange(DIL_BLOCK)[None, :, None]
    c = jnp.arange(3 * DIL_BLOCK)[None, None, :]
    delta = c - DIL_BLOCK - a
    kpos = (blk - 1) * DIL_BLOCK + c
    valid = (jnp.abs(delta) <= radius) & (kpos >= 0) & (kpos < L)
    dist = (dilation * jnp.abs(delta[0])).astype(jnp.float32)
    bias = -slopes[:, None, None] * dist[None]

    s = jnp.einsum("bdnqhe,bdnkhe->bdnhqk", qb, kw).astype(jnp.float32) * (E ** -0.5)
    s = jnp.where(valid[None, None, :, None], s + bias, NEG_INF)
    lse = jax.nn.logsumexp(s, axis=-1)
    p = jnp.exp(s - lse[..., None]).astype(v.dtype)
    o = jnp.einsum("bdnhqk,bdnkhe->bdnqhe", p, vw)
    o = o.reshape(B, dilation, nb * DIL_BLOCK, H, E)[:, :, :L]
    o = o.transpose(0, 2, 1, 3, 4).reshape(B, S, H, E)
    lse = lse.transpose(0, 1, 2, 4, 3).reshape(B, dilation, nb * DIL_BLOCK, H)[:, :, :L]
    lse = lse.transpose(0, 2, 1, 3).reshape(B, S, H)
    return o, lse


def dilated_mixture_attention(q, k, v, q_g, k_g):
    B, S, _ = q.shape
    q = rmsnorm(q.reshape(B, S, N_ATT_HEADS, ATT_HEAD_DIM), q_g)
    k = rmsnorm(k.reshape(B, S, N_ATT_HEADS, ATT_HEAD_DIM), k_g)
    v = v.reshape(B, S, N_ATT_HEADS, ATT_HEAD_DIM)
    slopes = alibi_slopes(N_ATT_HEADS)
    outs, lses = [], []
    for g, (window, dilation) in enumerate(DIL_GROUPS):
        hs = slice(g * HEADS_PER_GROUP, (g + 1) * HEADS_PER_GROUP)
        radius = window // (2 * dilation)
        o, lse = dilated_window_attention(q[:, :, hs], k[:, :, hs], v[:, :, hs],
                                          dilation, radius, slopes[hs])
        outs.append(o)
        lses.append(lse)
    w = jax.nn.softmax(jnp.stack(lses, axis=0), axis=0)
    o = jnp.sum(w[..., None] * jnp.stack(outs, axis=0).astype(jnp.float32), axis=0)
    return o.astype(q.dtype).reshape(B, S, ATT_OUT_WIDTH)


def memory_cross_attention(xq, mem, mem_g, w_mem_kv, xq_g, xk_g):
    B, S, _ = xq.shape
    M = mem.shape[1]
    q = rmsnorm(xq.reshape(B, S, XATT_HEADS, XATT_HEAD_DIM), xq_g)
    kv = rmsnorm(mem, mem_g) @ w_mem_kv
    k, v = jnp.split(kv, 2, axis=-1)
    k = rmsnorm(k.reshape(B, M, XATT_HEADS, XATT_HEAD_DIM), xk_g)
    v = v.reshape(B, M, XATT_HEADS, XATT_HEAD_DIM)
    s = jnp.einsum("bshe,bmhe->bhsm", q, k).astype(jnp.float32) * (XATT_HEAD_DIM ** -0.5)
    p = jax.nn.softmax(s, axis=-1).astype(v.dtype)
    return jnp.einsum("bhsm,bmhe->bshe", p, v).reshape(B, S, XATT_WIDTH)


def expert_choice_ffn(h, w_router, w_gate, w_up, w_down):
    B, S, D = h.shape
    n_tok = B * S
    t = h.reshape(n_tok, D)
    aff = jax.nn.softmax((t @ w_router).astype(jnp.float32), axis=-1)
    cap = max(1, EC_CAPACITY * n_tok // N_EXPERTS)
    gate, idx = lax.top_k(aff.T, cap)
    xe = jnp.take(t, idx, axis=0)
    hid = jax.nn.silu(jnp.einsum("ecd,edf->ecf", xe, w_gate)) * jnp.einsum("ecd,edf->ecf", xe, w_up)
    ye = jnp.einsum("ecf,efd->ecd", hid, w_down) * gate[..., None].astype(h.dtype)
    out = jnp.zeros_like(t).at[idx.reshape(-1)].add(ye.reshape(-1, D))
    return out.reshape(B, S, D)


def encoder_layer(x, mem, norm1_g, w_in, conv_w, q_norm_g, k_norm_g, mem_norm_g, w_mem_kv,
                  xq_norm_g, xk_norm_g, w_br_conv, w_br_attn, w_br_xattn, w_o, norm2_g,
                  w_router, w_exp_gate, w_exp_up, w_exp_down):
    B, S, D = x.shape
    h = rmsnorm(x, norm1_g)
    proj = h @ w_in
    sizes = [CONV_WIDTH] * 3 + [ATT_WIDTH] * 3 + [XATT_WIDTH, N_BRANCH * D_MODEL]
    cuts = [int(c) for c in np.cumsum(sizes)[:-1]]
    cb, cc, cx, aq, ak, av, xq, gates = jnp.split(proj, cuts, axis=-1)

    y_conv = (cb * short_conv(cc * cx, conv_w)) @ w_br_conv
    y_attn = dilated_mixture_attention(aq, ak, av, q_norm_g, k_norm_g) @ w_br_attn
    y_x = memory_cross_attention(xq, mem, mem_norm_g, w_mem_kv, xq_norm_g, xk_norm_g) @ w_br_xattn

    g = jax.nn.sigmoid(gates.astype(jnp.float32)).astype(x.dtype).reshape(B, S, N_BRANCH, D)
    merged = g[:, :, 0] * y_conv + g[:, :, 1] * y_attn + g[:, :, 2] * y_x
    x = x + merged @ w_o

    x = x + expert_choice_ffn(rmsnorm(x, norm2_g), w_router, w_exp_gate, w_exp_up, w_exp_down)
    return x


def setup_inputs(seed: int = 0) -> dict:
    key = jax.random.key(seed)
    ks = jax.random.split(key, 24)
    f32 = jnp.float32

    def nrm(k, shape, fan_in):
        return jax.random.normal(k, shape, f32) * (fan_in ** -0.5)

    def gain(k, shape):
        return 1.0 + 0.02 * jax.random.normal(k, shape, f32)

    L = DEPTH
    return {
        "x_prompt": jax.random.normal(ks[0], (BATCH, SEQ, D_MODEL), f32),
        "x_sample": jax.random.normal(ks[1], (DEC_BATCH, DEC_SEQ, D_MODEL), f32),
        "mem_prompt": jax.random.normal(ks[2], (BATCH, N_MEM, D_MODEL), f32),
        "mem_sample": jax.random.normal(ks[3], (DEC_BATCH, N_MEM, D_MODEL), f32),
        "norm1_g": gain(ks[4], (L, D_MODEL)),
        "w_in": nrm(ks[5], (L, D_MODEL, IN_COLS), D_MODEL),
        "conv_w": nrm(ks[6], (L, CONV_K, CONV_WIDTH), CONV_K),
        "q_norm_g": gain(ks[7], (L, ATT_HEAD_DIM)),
        "k_norm_g": gain(ks[8], (L, ATT_HEAD_DIM)),
        "mem_norm_g": gain(ks[9], (L, D_MODEL)),
        "w_mem_kv": nrm(ks[10], (L, D_MODEL, 2 * XATT_WIDTH), D_MODEL),
        "xq_norm_g": gain(ks[11], (L, XATT_HEAD_DIM)),
        "xk_norm_g": gain(ks[12], (L, XATT_HEAD_DIM)),
        "w_br_conv": nrm(ks[13], (L, CONV_WIDTH, D_MODEL), CONV_WIDTH),
        "w_br_attn": nrm(ks[14], (L, ATT_OUT_WIDTH, D_MODEL), ATT_OUT_WIDTH),
        "w_br_xattn": nrm(ks[15], (L, XATT_WIDTH, D_MODEL), XATT_WIDTH),
        "w_o": nrm(ks[16], (L, D_MODEL, D_MODEL), D_MODEL),
        "norm2_g": gain(ks[17], (L, D_MODEL)),
        "w_router": nrm(ks[18], (L, D_MODEL, N_EXPERTS), D_MODEL),
        "w_exp_gate": nrm(ks[19], (L, N_EXPERTS, D_MODEL, D_FF_EXPERT), D_MODEL),
        "w_exp_up": nrm(ks[20], (L, N_EXPERTS, D_MODEL, D_FF_EXPERT), D_MODEL),
        "w_exp_down": nrm(ks[21], (L, N_EXPERTS, D_FF_EXPERT, D_MODEL), D_FF_EXPERT),
    }


def reference(x_prompt, x_sample, mem_prompt, mem_sample, norm1_g, w_in, conv_w, q_norm_g,
              k_norm_g, mem_norm_g, w_mem_kv, xq_norm_g, xk_norm_g, w_br_conv, w_br_attn,
              w_br_xattn, w_o, norm2_g, w_router, w_exp_gate, w_exp_up, w_exp_down):
    y_prompt = x_prompt
    y_sample = x_sample
    for l in range(DEPTH):
        p = (norm1_g[l], w_in[l], conv_w[l], q_norm_g[l], k_norm_g[l], mem_norm_g[l], w_mem_kv[l],
             xq_norm_g[l], xk_norm_g[l], w_br_conv[l], w_br_attn[l], w_br_xattn[l], w_o[l],
             norm2_g[l], w_router[l], w_exp_gate[l], w_exp_up[l], w_exp_down[l])
        y_prompt = encoder_layer(y_prompt, mem_prompt, *p)
        y_sample = encoder_layer(y_sample, mem_sample, *p)
    return (y_prompt, y_sample)
```

```python
import functools

import jax
import jax.numpy as jnp
from jax import lax
from jax.experimental import pallas as pl
from jax.experimental.pallas import tpu as pltpu

F32 = jnp.float32
BF16 = jnp.bfloat16
I32 = jnp.int32

D_MODEL = 1024
N_MEM = 256
CONV_WIDTH = 768
ATT_HEAD_DIM = 64
DIL_GROUPS = ((128, 1), (512, 4), (2048, 16))
HEADS_PER_GROUP = 4
N_ATT_HEADS = HEADS_PER_GROUP * len(DIL_GROUPS)
ATT_WIDTH = N_ATT_HEADS * ATT_HEAD_DIM
GROUP_WIDTH = HEADS_PER_GROUP * ATT_HEAD_DIM
ATT_RADIUS = 64
XATT_HEADS = 4
XATT_HEAD_DIM = 128
XATT_WIDTH = XATT_HEADS * XATT_HEAD_DIM
N_EXPERTS = 16
EC_CAPACITY = 2
ALIBI_MAX_EXP = 8.0
EPS = 1e-6
NEG_INF = -1e30

C_CB, C_CC, C_AQ, C_XQ, C_GATE, C_END = 0, 768, 2304, 4608, 5120, 8192

V7X_VMEM_LIMIT_BYTES = 56 * 1024 * 1024
LANES = 128
HALO = 16

PROJ_TILE = 256
ATT_QB = 128
ROUTE_TILE = 256
ROUTE_CHUNK = 64
SEG_ALIGN = 16
FFN_TILE = 512
XE_W = D_MODEL + LANES


def _dot(a, b):
    return jnp.dot(a, b, preferred_element_type=F32)


def _dot_nt(a, b):
    return lax.dot_general(a, b, (((1,), (1,)), ((), ())), preferred_element_type=F32)


def _params(sem):
    return pltpu.CompilerParams(dimension_semantics=sem, vmem_limit_bytes=V7X_VMEM_LIMIT_BYTES)


def _full(shape):
    return pl.BlockSpec(shape, lambda *_: (0,) * len(shape))


def _resident(shape):
    return pl.BlockSpec(shape, lambda *_: (0,) * len(shape), pipeline_mode=pl.Buffered(1))


def _memkv_kernel(mem_ref, g_ref, w_ref, kg_ref, k_ref, v_ref):
    m = mem_ref[...]
    hn = (m * lax.rsqrt(jnp.mean(m * m, axis=-1, keepdims=True) + EPS) * g_ref[...]).astype(BF16)
    kv = _dot(hn, w_ref[...])
    ks = []
    for h in range(XATT_HEADS):
        kh = kv[:, h * XATT_HEAD_DIM:(h + 1) * XATT_HEAD_DIM]
        ks.append(kh * lax.rsqrt(jnp.mean(kh * kh, axis=-1, keepdims=True) + EPS) * kg_ref[...])
    k_ref[...] = jnp.concatenate(ks, axis=1).astype(BF16)
    v_ref[...] = kv[:, XATT_WIDTH:].astype(BF16)


def _memkv(mem, mem_g, w_mem_kv, xk_g):
    b, m, d = mem.shape
    out = jax.ShapeDtypeStruct((b, m, XATT_WIDTH), BF16)
    return pl.pallas_call(
        _memkv_kernel,
        grid=(b,),
        in_specs=[pl.BlockSpec((None, m, d), lambda i: (i, 0, 0)), _full((1, d)),
                  _full((d, 2 * XATT_WIDTH)), _full((1, XATT_HEAD_DIM))],
        out_specs=[pl.BlockSpec((None, m, XATT_WIDTH), lambda i: (i, 0, 0))] * 2,
        out_shape=[out, out],
        compiler_params=_params(("arbitrary",)),
        name="memkv",
    )(mem, mem_g, w_mem_kv, xk_g)


def _proj_kernel(xp_ref, x_ref, xn_ref, n1g_ref, win_ref, cw_ref, qg_ref, kg_ref, xqg_ref, kx_ref, vx_ref,
                 wbc_ref, wbx_ref, bd_ref,
                 q0_ref, q1_ref, q2_ref, k0_ref, k1_ref, k2_ref, v0_ref, v1_ref, v2_ref, p_ref, g1_ref,
                 hb_s, u_s, *, tile, tiles_per_seq):
    tin = pl.program_id(0) % tiles_per_seq
    gain = n1g_ref[...]

    def nrm(x):
        return (x * lax.rsqrt(jnp.mean(x * x, axis=-1, keepdims=True) + EPS) * gain).astype(BF16)

    hb_s[0:HALO, :] = nrm(xp_ref[...])
    hb_s[HALO:HALO + tile, :] = nrm(x_ref[...])
    hb_s[HALO + tile:, :] = nrm(xn_ref[...])
    hc = hb_s[HALO:HALO + tile, :]

    ccx = _dot(hb_s[...], win_ref[:, C_CC:C_AQ])
    u = ccx[:, :CONV_WIDTH] * ccx[:, CONV_WIDTH:]
    u_s[0:HALO, :] = u[0:HALO] * jnp.where(tin == 0, 0.0, 1.0)
    u_s[HALO:HALO + tile, :] = u[HALO:HALO + tile]
    u_s[HALO + tile:, :] = u[HALO + tile:] * jnp.where(tin == tiles_per_seq - 1, 0.0, 1.0)
    cw = cw_ref[...]
    conv = (cw[0:1] * u_s[HALO - 1:HALO - 1 + tile, :] + cw[1:2] * u_s[HALO:HALO + tile, :]
            + cw[2:3] * u_s[HALO + 1:HALO + 1 + tile, :])
    cb = _dot(hc, win_ref[:, C_CB:C_CC])
    y_conv = _dot((cb * conv).astype(BF16), wbc_ref[...])

    qkv = _dot(hc, win_ref[:, C_AQ:C_XQ])
    bd = bd_ref[...]

    def head_norm(z, g_ref):
        outs = []
        for c in range(len(DIL_GROUPS)):
            zc = z[:, c * GROUP_WIDTH:(c + 1) * GROUP_WIDTH]
            ms = _dot((zc * zc).astype(BF16), bd)
            outs.append((zc * lax.rsqrt(ms + EPS) * g_ref[...]).astype(BF16))
        return outs

    for ref, val in zip((q0_ref, q1_ref, q2_ref), head_norm(qkv[:, :ATT_WIDTH], qg_ref)):
        ref[...] = val
    for ref, val in zip((k0_ref, k1_ref, k2_ref), head_norm(qkv[:, ATT_WIDTH:2 * ATT_WIDTH], kg_ref)):
        ref[...] = val
    for c, ref in enumerate((v0_ref, v1_ref, v2_ref)):
        ref[...] = qkv[:, 2 * ATT_WIDTH + c * GROUP_WIDTH:2 * ATT_WIDTH + (c + 1) * GROUP_WIDTH].astype(BF16)

    xq = _dot(hc, win_ref[:, C_XQ:C_GATE])
    kx = kx_ref[...]
    vx = vx_ref[...]
    outs = []
    for h in range(XATT_HEADS):
        hs = slice(h * XATT_HEAD_DIM, (h + 1) * XATT_HEAD_DIM)
        qh = xq[:, hs]
        qh = qh * lax.rsqrt(jnp.mean(qh * qh, axis=-1, keepdims=True) + EPS) * xqg_ref[...]
        s = _dot_nt(qh.astype(BF16), kx[:, hs]) * (XATT_HEAD_DIM ** -0.5)
        p = jnp.exp(s - jnp.max(s, axis=-1, keepdims=True))
        den = jnp.sum(p, axis=-1, keepdims=True)
        outs.append(_dot(p.astype(BF16), vx[:, hs]) / den)
    y_x = _dot(jnp.concatenate(outs, axis=1).astype(BF16), wbx_ref[...])

    gs = jax.nn.sigmoid(_dot(hc, win_ref[:, C_GATE:C_END]))
    p_ref[...] = (gs[:, :D_MODEL] * y_conv + gs[:, 2 * D_MODEL:] * y_x).astype(BF16)
    g1_ref[...] = gs[:, D_MODEL:2 * D_MODEL].astype(BF16)


def _proj(x2, seq_len, kx, vx, w):
    n, d = x2.shape
    tile = PROJ_TILE
    tps = seq_len // tile
    hb = tile // HALO
    n_hblk = n // HALO
    grp = jax.ShapeDtypeStruct((n, GROUP_WIDTH), BF16)
    wide = jax.ShapeDtypeStruct((n, d), BF16)
    row = lambda width: pl.BlockSpec((tile, width), lambda i: (i, 0))
    return pl.pallas_call(
        functools.partial(_proj_kernel, tile=tile, tiles_per_seq=tps),
        grid=(n // tile,),
        in_specs=[
            pl.BlockSpec((HALO, d), lambda i: (jnp.maximum(i * hb - 1, 0), 0)),
            row(d),
            pl.BlockSpec((HALO, d), lambda i: (jnp.minimum((i + 1) * hb, n_hblk - 1), 0)),
            _full((1, d)), _resident((d, C_END)), _full((3, CONV_WIDTH)),
            _full((1, GROUP_WIDTH)), _full((1, GROUP_WIDTH)), _full((1, XATT_HEAD_DIM)),
            pl.BlockSpec((None, N_MEM, XATT_WIDTH), lambda i: (i // tps, 0, 0)),
            pl.BlockSpec((None, N_MEM, XATT_WIDTH), lambda i: (i // tps, 0, 0)),
            _resident((CONV_WIDTH, d)), _resident((XATT_WIDTH, d)), _full((GROUP_WIDTH, GROUP_WIDTH)),
        ],
        out_specs=[row(GROUP_WIDTH)] * 9 + [row(d), row(d)],
        out_shape=[grp] * 9 + [wide, wide],
        scratch_shapes=[pltpu.VMEM((tile + 2 * HALO, d), BF16), pltpu.VMEM((tile + 2 * HALO, CONV_WIDTH), F32)],
        compiler_params=_params(("arbitrary",)),
        name="proj",
    )(x2, x2, x2, w["norm1_g"], w["w_in"], w["conv_w"], w["q_gain"], w["k_gain"], w["xq_gain"], kx, vx,
      w["w_br_conv"], w["w_br_xattn"], w["head_avg"])


def _attn_kernel(q_ref, k_ref, v_ref, bias_ref, o_ref, lse_ref, *, seq_sub, lq, qb, kw, n_res):
    i = pl.program_id(2)
    lane_head = lax.broadcasted_iota(I32, (qb, GROUP_WIDTH), 1) // ATT_HEAD_DIM
    head_mask = [lane_head == h for h in range(HEADS_PER_GROUP)]
    head_mask_bf = [jnp.where(m, 1.0, 0.0).astype(BF16) for m in head_mask]

    def block(sb, carry):
        row0 = pl.multiple_of(sb * qb, qb)
        qs = i * lq + row0
        ks = pl.multiple_of(jnp.clip(qs - ATT_RADIUS, 0, seq_sub - kw), ATT_RADIUS)
        case = jnp.where(qs == 0, 0, jnp.where(qs == seq_sub - qb, 2, 1))
        for r in range(n_res):
            cs = slice(r * GROUP_WIDTH, (r + 1) * GROUP_WIDTH)
            q = q_ref[pl.ds(row0, qb), cs]
            kk = k_ref[pl.ds(ks, kw), cs]
            vv = v_ref[pl.ds(ks, kw), cs]
            s = _dot_nt(jnp.concatenate([q * m for m in head_mask_bf], axis=0), kk)
            ps, inv_den, lse = [], [], []
            for h in range(HEADS_PER_GROUP):
                sh = s[h * qb:(h + 1) * qb] + bias_ref[case * HEADS_PER_GROUP + h]
                m = jnp.max(sh, axis=-1, keepdims=True)
                p = jnp.exp(sh - m)
                den = jnp.sum(p, axis=-1, keepdims=True)
                ps.append(p.astype(BF16))
                inv_den.append(1.0 / den)
                lse.append(m + jnp.log(den))
            of = _dot(jnp.concatenate(ps, axis=0), vv)
            o = jnp.zeros((qb, GROUP_WIDTH), F32)
            lb = jnp.zeros((qb, GROUP_WIDTH), F32)
            for h in range(HEADS_PER_GROUP):
                o = jnp.where(head_mask[h], of[h * qb:(h + 1) * qb] * inv_den[h], o)
                lb = jnp.where(head_mask[h], lse[h], lb)
            o_ref[pl.ds(row0, qb), cs] = o.astype(BF16)
            lse_ref[pl.ds(row0, qb), cs] = lb
        return carry

    lax.fori_loop(0, lq // qb, block, 0)


def _attn_bias(dilation, qb, kw, group):
    slopes = jnp.exp2(-ALIBI_MAX_EXP * jnp.arange(1, N_ATT_HEADS + 1, dtype=F32) / N_ATT_HEADS)
    slopes = slopes[group * HEADS_PER_GROUP:(group + 1) * HEADS_PER_GROUP]
    a = jnp.arange(qb)[:, None]
    c = jnp.arange(kw)[None, :]
    tabs = []
    for off in (0, -ATT_RADIUS, qb - kw):
        delta = off + c - a
        dist = (dilation * jnp.abs(delta)).astype(F32)
        bias = -slopes[:, None, None] * dist[None]
        tabs.append(jnp.where((jnp.abs(delta) <= ATT_RADIUS)[None], bias, NEG_INF))
    return jnp.concatenate(tabs, axis=0)


def _attn(q, k, v, batch, seq_len, group):
    _, dilation = DIL_GROUPS[group]
    seq_sub = seq_len // dilation
    qb = min(ATT_QB, seq_sub)
    kw = min(qb + 2 * ATT_RADIUS, seq_sub)
    lq = min(4 * qb, seq_sub)
    n_res = min(dilation, 4)
    width = n_res * GROUP_WIDTH
    view = lambda t: t.reshape(batch, seq_sub, dilation * GROUP_WIDTH)
    qspec = pl.BlockSpec((None, lq, width), lambda b, r, i: (b, i, r))
    kspec = pl.BlockSpec((None, seq_sub, width), lambda b, r, i: (b, 0, r))
    bias = _attn_bias(dilation, qb, kw, group)
    o, lse = pl.pallas_call(
        functools.partial(_attn_kernel, seq_sub=seq_sub, lq=lq, qb=qb, kw=kw, n_res=n_res),
        grid=(batch, dilation // n_res, seq_sub // lq),
        in_specs=[qspec, kspec, kspec, _full(bias.shape)],
        out_specs=[qspec, qspec],
        out_shape=[jax.ShapeDtypeStruct((batch, seq_sub, dilation * GROUP_WIDTH), BF16),
                   jax.ShapeDtypeStruct((batch, seq_sub, dilation * GROUP_WIDTH), F32)],
        compiler_params=_params(("arbitrary", "arbitrary", "arbitrary")),
        name=f"attn_d{dilation}",
    )(view(q), view(k), view(v), bias)
    n = batch * seq_len
    return o.reshape(n, GROUP_WIDTH), lse.reshape(n, GROUP_WIDTH)


def _merge_kernel(x_ref, p_ref, g1_ref, o0_ref, o1_ref, o2_ref, l0_ref, l1_ref, l2_ref, wba_ref, wo_ref,
                  n2g_ref, wr_ref, x1_ref, h2e_ref, aff_ref):
    l0, l1, l2 = l0_ref[...], l1_ref[...], l2_ref[...]
    m = jnp.maximum(jnp.maximum(l0, l1), l2)
    e0, e1, e2 = jnp.exp(l0 - m), jnp.exp(l1 - m), jnp.exp(l2 - m)
    o = (e0 * o0_ref[...].astype(F32) + e1 * o1_ref[...].astype(F32) + e2 * o2_ref[...].astype(F32)) / (e0 + e1 + e2)
    y_attn = _dot(o.astype(BF16), wba_ref[...])
    merged = p_ref[...].astype(F32) + g1_ref[...].astype(F32) * y_attn
    x1 = x_ref[...] + _dot(merged.astype(BF16), wo_ref[...])
    x1_ref[...] = x1
    h2 = (x1 * lax.rsqrt(jnp.mean(x1 * x1, axis=-1, keepdims=True) + EPS) * n2g_ref[...]).astype(BF16)
    h2e_ref[:, :D_MODEL] = h2
    lg = _dot(h2, wr_ref[...])
    lane = lax.broadcasted_iota(I32, lg.shape, 1)
    first = lane < N_EXPERTS
    mx = jnp.max(jnp.where(first, lg, -jnp.inf), axis=-1, keepdims=True)
    ex = jnp.exp(lg - mx)
    a = ex / jnp.sum(jnp.where(first, ex, 0.0), axis=-1, keepdims=True)
    aff_ref[...] = a
    hi = a.astype(BF16).astype(F32)
    mid = (a - hi).astype(BF16).astype(F32)
    lo = (a - hi) - mid
    ext = jnp.where(first, hi, jnp.where(lane < 2 * N_EXPERTS, mid, jnp.where(lane < 3 * N_EXPERTS, lo, 0.0)))
    h2e_ref[:, D_MODEL:] = ext.astype(BF16)


def _merge(x2, p, g1, os_, ls_, w):
    n, d = x2.shape
    tile = PROJ_TILE
    row = lambda width: pl.BlockSpec((tile, width), lambda i: (i, 0))
    return pl.pallas_call(
        _merge_kernel,
        grid=(n // tile,),
        in_specs=[row(d), row(d), row(d)] + [row(GROUP_WIDTH)] * 6
                 + [_full((GROUP_WIDTH, d)), _full((d, d)), _full((1, d)), _full((d, LANES))],
        out_specs=[row(d), row(d + LANES), row(LANES)],
        out_shape=[jax.ShapeDtypeStruct((n, d), F32), jax.ShapeDtypeStruct((n, d + LANES), BF16),
                   jax.ShapeDtypeStruct((n, LANES), F32)],
        compiler_params=_params(("arbitrary",)),
        name="merge",
    )(x2, p, g1, *os_, *ls_, w["w_br_attn"], w["w_o"], w["norm2_g"], w["w_router3"])


def _select_kernel(aff_ref, slot_ref, starts_ref, nchunk_ref, tot_ref, *, n_tok, cap, tile):
    n_tiles = n_tok // tile

    def bisect(b, thr_bits):
        cand = thr_bits | jnp.left_shift(jnp.int32(1), 30 - b)
        cnt = jnp.sum(jnp.where(aff_ref[...] >= lax.bitcast_convert_type(cand, F32), 1.0, 0.0), axis=1, keepdims=True)
        return jnp.where(cnt >= cap, cand, thr_bits)

    thr = lax.bitcast_convert_type(lax.fori_loop(0, 31, bisect, jnp.zeros((N_EXPERTS, 1), I32)), F32)
    need = cap - jnp.sum(jnp.where(aff_ref[...] > thr, 1.0, 0.0), axis=1, keepdims=True)

    r = lax.broadcasted_iota(I32, (tile, tile), 0)
    c = lax.broadcasted_iota(I32, (tile, tile), 1)
    before = jnp.where(r < c, 1.0, 0.0).astype(BF16)

    def tile_body(i, carry):
        start, eq_seen = carry
        off = pl.multiple_of(i * tile, tile)
        a = aff_ref[:, pl.ds(off, tile)]
        eq = jnp.where(a == thr, 1.0, 0.0)
        eq_rank = eq_seen + _dot(eq.astype(BF16), before)
        sel = jnp.where(a > thr, 1.0, jnp.where(eq_rank < need, eq, 0.0))
        rank = _dot(sel.astype(BF16), before)
        slot_ref[:, pl.ds(off, tile)] = jnp.where(sel > 0.0, rank, -1.0).astype(I32)
        cnt = jnp.sum(sel, axis=1, keepdims=True)
        starts_ref[i] = jnp.broadcast_to(start, (N_EXPERTS, LANES)).astype(I32)
        nch = jnp.max(jnp.ceil(cnt * (1.0 / ROUTE_CHUNK)), axis=0, keepdims=True)
        nchunk_ref[i] = jnp.broadcast_to(nch, (8, LANES)).astype(I32)
        start = start + jnp.ceil(cnt * (1.0 / SEG_ALIGN)) * SEG_ALIGN
        return start, eq_seen + jnp.sum(eq, axis=1, keepdims=True)

    zero = jnp.zeros((N_EXPERTS, 1), F32)
    total, _ = lax.fori_loop(0, n_tiles, tile_body, (zero, zero))
    tot_ref[...] = jnp.broadcast_to(total, (N_EXPERTS, LANES)).astype(I32)


def _select(aff_t, cap):
    n_exp, n = aff_t.shape
    tile = ROUTE_TILE
    n_tiles = n // tile
    slot, starts, nchunk, tot = pl.pallas_call(
        functools.partial(_select_kernel, n_tok=n, cap=cap, tile=tile),
        grid=(1,),
        in_specs=[_full((n_exp, n))],
        out_specs=[_full((n_exp, n)), _full((n_tiles, n_exp, LANES)), _full((n_tiles, 8, LANES)), _full((n_exp, LANES))],
        out_shape=[jax.ShapeDtypeStruct((n_exp, n), I32), jax.ShapeDtypeStruct((n_tiles, n_exp, LANES), I32),
                   jax.ShapeDtypeStruct((n_tiles, 8, LANES), I32), jax.ShapeDtypeStruct((n_exp, LANES), I32)],
        compiler_params=_params(("arbitrary",)),
        name="select",
    )(aff_t)
    return slot, starts[:, :, 0].reshape(-1), nchunk[:, 0, 0], tot[:, 0]


def _dispatch_kernel(starts_ref, nchunk_ref, slot_ref, h_ref, xe_hbm, stage, sem, *, tile, chunk):
    i = pl.program_id(0)
    sl = i % 2

    def copies(step, c, buf):
        return [pltpu.make_async_copy(
            stage.at[buf, pl.ds(e * chunk, chunk), :],
            xe_hbm.at[e, pl.ds(pl.multiple_of(starts_ref[step * N_EXPERTS + e] + c * chunk, SEG_ALIGN), chunk), :],
            sem.at[buf]) for e in range(N_EXPERTS)]

    def build(c, buf):
        slot = slot_ref[...]
        want = lax.broadcasted_iota(I32, (chunk, tile), 0) + c * chunk
        onehot = jnp.concatenate(
            [jnp.where(slot[e:e + 1, :] == want, 1.0, 0.0).astype(BF16) for e in range(N_EXPERTS)], axis=0)
        stage[buf] = _dot(onehot, h_ref[...]).astype(BF16)

    build(0, sl)

    @pl.when(i > 0)
    def _():
        for cp in copies(i - 1, 0, 1 - sl):
            cp.wait()

    for cp in copies(i, 0, sl):
        cp.start()

    def overflow(c, carry):
        for cp in copies(i, c - 1, sl):
            cp.wait()
        build(c, sl)
        for cp in copies(i, c, sl):
            cp.start()
        return carry

    lax.fori_loop(1, nchunk_ref[i], overflow, 0)

    @pl.when(i == pl.num_programs(0) - 1)
    def _():
        for cp in copies(i, 0, sl):
            cp.wait()
        rows = xe_hbm.shape[1]
        zbuf = 1 - sl
        stage[zbuf] = jnp.zeros(stage.shape[1:], BF16)
        end = [starts_ref[i * N_EXPERTS + e] + jnp.maximum(nchunk_ref[i], 1) * chunk for e in range(N_EXPERTS)]
        n_full = [(rows - end[e]) // chunk for e in range(N_EXPERTS)]

        def zero_copy(e, pos):
            return pltpu.make_async_copy(stage.at[zbuf, pl.ds(e * chunk, chunk), :],
                                         xe_hbm.at[e, pl.ds(pl.multiple_of(pos, SEG_ALIGN), chunk), :], sem.at[zbuf])

        def fill(k, carry):
            for e in range(N_EXPERTS):
                @pl.when(k < n_full[e])
                def _():
                    zero_copy(e, end[e] + k * chunk).start()
            for e in range(N_EXPERTS):
                @pl.when(k < n_full[e])
                def _():
                    zero_copy(e, end[e] + k * chunk).wait()
            return carry

        lax.fori_loop(0, functools.reduce(jnp.maximum, n_full), fill, 0)
        for e in range(N_EXPERTS):
            zero_copy(e, rows - chunk).start()
        for e in range(N_EXPERTS):
            zero_copy(e, rows - chunk).wait()


def _dispatch(slot, h2e, starts, nchunk, rows):
    n = h2e.shape[0]
    tile, chunk = ROUTE_TILE, ROUTE_CHUNK
    return pl.pallas_call(
        functools.partial(_dispatch_kernel, tile=tile, chunk=chunk),
        grid_spec=pltpu.PrefetchScalarGridSpec(
            num_scalar_prefetch=2, grid=(n // tile,),
            in_specs=[pl.BlockSpec((N_EXPERTS, tile), lambda i, *_: (0, i)),
                      pl.BlockSpec((tile, D_MODEL + LANES), lambda i, *_: (i, 0))],
            out_specs=pl.BlockSpec(memory_space=pl.ANY),
            scratch_shapes=[pltpu.VMEM((2, N_EXPERTS * chunk, XE_W), BF16), pltpu.SemaphoreType.DMA((2,))]),
        out_shape=jax.ShapeDtypeStruct((N_EXPERTS, rows, XE_W), BF16),
        compiler_params=_params(("arbitrary",)),
        name="dispatch",
    )(starts, nchunk, slot, h2e)


def _ffn_kernel(tot_ref, xe_ref, wg_ref, wu_ref, wd_ref, ye_ref, wg_s, wu_s, wd_s, *, tile):
    e = pl.program_id(0)
    j = pl.program_id(1)

    @pl.when(j == 0)
    def _():
        wg_s[...] = wg_ref[...].astype(BF16)
        wu_s[...] = wu_ref[...].astype(BF16)
        wd_s[...] = wd_ref[...].astype(BF16)

    live = tot_ref[e] - j * tile

    @pl.when(live > 0)
    def _():
        x = xe_ref[:, :D_MODEL]
        pieces = xe_ref[:, D_MODEL:].astype(F32)
        lane = lax.broadcasted_iota(I32, pieces.shape, 1)
        mine = ((lane & (N_EXPERTS - 1)) == e) & (lane < 3 * N_EXPERTS)
        gate = jnp.sum(jnp.where(mine, pieces, 0.0), axis=-1, keepdims=True)
        g = _dot(x, wg_s[...])
        u = _dot(x, wu_s[...])
        hid = (g * jax.nn.sigmoid(g) * u).astype(BF16)
        y = _dot(hid, wd_s[...]) * gate
        rows = lax.broadcasted_iota(I32, y.shape, 0)
        y = jnp.where(rows < live, y, 0.0)
        ye_ref[...] = y.astype(BF16)

    @pl.when(live <= 0)
    def _():
        ye_ref[...] = jnp.zeros(ye_ref.shape, BF16)


def _ffn(xe, tot, w_gate, w_up, w_down):
    n_exp, rows, _ = xe.shape
    tile = FFN_TILE
    d, f = w_gate.shape[1:]
    last = lambda e, j, tot: jnp.minimum(j, jnp.maximum(tot[e] - 1, 0) // tile)
    wspec = lambda a, b: pl.BlockSpec((None, a, b), lambda e, j, tot: (e, 0, 0))
    return pl.pallas_call(
        functools.partial(_ffn_kernel, tile=tile),
        grid_spec=pltpu.PrefetchScalarGridSpec(
            num_scalar_prefetch=1, grid=(n_exp, rows // tile),
            in_specs=[pl.BlockSpec((None, tile, XE_W), lambda e, j, tot: (e, last(e, j, tot), 0)),
                      wspec(d, f), wspec(d, f), wspec(f, d)],
            out_specs=pl.BlockSpec((None, tile, d), lambda e, j, tot: (e, j, 0)),
            scratch_shapes=[pltpu.VMEM((d, f), BF16), pltpu.VMEM((d, f), BF16), pltpu.VMEM((f, d), BF16)]),
        out_shape=jax.ShapeDtypeStruct((n_exp, rows, d), BF16),
        compiler_params=_params(("arbitrary", "arbitrary")),
        name="ffn",
    )(tot, xe, w_gate, w_up, w_down)


def _combine_kernel(starts_ref, nchunk_ref, x1_ref, slot_ref, ye_hbm, out_ref, stage, sem, *, tile, chunk):
    i = pl.program_id(0)
    sl = i % 2
    over = 2

    def copies(step, c, buf):
        return [pltpu.make_async_copy(
            ye_hbm.at[e, pl.ds(pl.multiple_of(starts_ref[step * N_EXPERTS + e] + c * chunk, SEG_ALIGN), chunk), :],
            stage.at[buf, pl.ds(e * chunk, chunk), :],
            sem.at[buf]) for e in range(N_EXPERTS)]

    @pl.when(i == 0)
    def _():
        for cp in copies(0, 0, 0):
            cp.start()

    @pl.when(i + 1 < pl.num_programs(0))
    def _():
        for cp in copies(i + 1, 0, 1 - sl):
            cp.start()

    col = lax.broadcasted_iota(I32, (N_EXPERTS, N_EXPERTS * chunk), 1)
    exp_row = lax.broadcasted_iota(I32, (N_EXPERTS, N_EXPERTS * chunk), 0)
    spread = jnp.where(col // chunk == exp_row, 1.0, 0.0).astype(BF16)
    slots = _dot(slot_ref[...].astype(F32).astype(BF16), spread)
    lane_slot = (lax.broadcasted_iota(I32, slots.shape, 1) & (chunk - 1)).astype(F32)

    def gathered(c, buf):
        onehot = jnp.where(slots == lane_slot + c * chunk, 1.0, 0.0).astype(BF16)
        return _dot(onehot, stage[buf])

    for cp in copies(i, 0, sl):
        cp.wait()
    out_ref[...] = x1_ref[...] + gathered(0, sl)

    def overflow(c, carry):
        for cp in copies(i, c, over):
            cp.start()
        for cp in copies(i, c, over):
            cp.wait()
        out_ref[...] += gathered(c, over)
        return carry

    lax.fori_loop(1, nchunk_ref[i], overflow, 0)


def _combine(x1, slot_t, ye, starts, nchunk):
    n, d = x1.shape
    tile, chunk = ROUTE_TILE, ROUTE_CHUNK
    return pl.pallas_call(
        functools.partial(_combine_kernel, tile=tile, chunk=chunk),
        grid_spec=pltpu.PrefetchScalarGridSpec(
            num_scalar_prefetch=2, grid=(n // tile,),
            in_specs=[pl.BlockSpec((tile, d), lambda i, *_: (i, 0)),
                      pl.BlockSpec((tile, N_EXPERTS), lambda i, *_: (i, 0)),
                      pl.BlockSpec(memory_space=pl.ANY)],
            out_specs=pl.BlockSpec((tile, d), lambda i, *_: (i, 0)),
            scratch_shapes=[pltpu.VMEM((3, N_EXPERTS * chunk, d), BF16), pltpu.SemaphoreType.DMA((3,))]),
        out_shape=jax.ShapeDtypeStruct((n, d), F32),
        compiler_params=_params(("arbitrary",)),
        name="combine",
    )(starts, nchunk, x1, slot_t, ye)


def _expert_rows(n_tok, cap):
    n_tiles = n_tok // ROUTE_TILE
    need = cap + SEG_ALIGN * n_tiles + ROUTE_TILE
    return -(-need // FFN_TILE) * FFN_TILE


def _routed_ffn(x1, h2e, aff, w):
    n = x1.shape[0]
    cap = max(1, EC_CAPACITY * n // N_EXPERTS)
    slot, starts, nchunk, tot = _select(aff[:, :N_EXPERTS].T, cap)
    xe = _dispatch(slot, h2e, starts, nchunk, _expert_rows(n, cap))
    ye = _ffn(xe, tot, w["w_exp_gate"], w["w_exp_up"], w["w_exp_down"])
    return _combine(x1, slot.T, ye, starts, nchunk)


def _layer(x, mem, w):
    b, s, d = x.shape
    x2 = x.reshape(b * s, d)
    kx, vx = _memkv(mem, w["mem_norm_g"], w["w_mem_kv"], w["xk_gain"])
    *qkv, p, g1 = _proj(x2, s, kx, vx, w)
    os_, ls_ = [], []
    for g in range(len(DIL_GROUPS)):
        o, lse = _attn(qkv[g], qkv[3 + g], qkv[6 + g], b, s, g)
        os_.append(o)
        ls_.append(lse)
    x1, h2e, aff = _merge(x2, p, g1, os_, ls_, w)
    return _routed_ffn(x1, h2e, aff, w).reshape(b, s, d)


def _prepare(norm1_g, w_in, conv_w, q_norm_g, k_norm_g, mem_norm_g, w_mem_kv, xq_norm_g, xk_norm_g,
             w_br_conv, w_br_attn, w_br_xattn, w_o, norm2_g, w_router, w_exp_gate, w_exp_up, w_exp_down):
    row = lambda v: v.reshape(1, -1).astype(F32)
    head = jnp.arange(GROUP_WIDTH) // ATT_HEAD_DIM
    return {
        "norm1_g": row(norm1_g), "norm2_g": row(norm2_g), "mem_norm_g": row(mem_norm_g),
        "w_in": w_in.astype(BF16), "conv_w": conv_w.astype(F32),
        "q_gain": row(jnp.tile(q_norm_g, HEADS_PER_GROUP) * (ATT_HEAD_DIM ** -0.5)),
        "k_gain": row(jnp.tile(k_norm_g, HEADS_PER_GROUP)),
        "xq_gain": row(xq_norm_g), "xk_gain": row(xk_norm_g),
        "w_mem_kv": w_mem_kv.astype(BF16), "w_br_conv": w_br_conv.astype(BF16),
        "w_br_attn": w_br_attn.astype(BF16), "w_br_xattn": w_br_xattn.astype(BF16), "w_o": w_o.astype(BF16),
        "head_avg": jnp.where(head[:, None] == head[None, :], 1.0 / ATT_HEAD_DIM, 0.0).astype(BF16),
        "w_router3": jnp.concatenate([w_router] * 3 + [jnp.zeros((D_MODEL, LANES - 3 * N_EXPERTS), F32)],
                                     axis=1).astype(BF16),
        "w_exp_gate": w_exp_gate, "w_exp_up": w_exp_up, "w_exp_down": w_exp_down,
    }


def kernel(x_prompt, x_sample, mem_prompt, mem_sample, norm1_g, w_in, conv_w, q_norm_g, k_norm_g, mem_norm_g, w_mem_kv, xq_norm_g, xk_norm_g, w_br_conv, w_br_attn, w_br_xattn, w_o, norm2_g, w_router, w_exp_gate, w_exp_up, w_exp_down):
    per_layer = (norm1_g, w_in, conv_w, q_norm_g, k_norm_g, mem_norm_g, w_mem_kv, xq_norm_g, xk_norm_g,
                 w_br_conv, w_br_attn, w_br_xattn, w_o, norm2_g, w_router, w_exp_gate, w_exp_up, w_exp_down)
    y_prompt, y_sample = x_prompt, x_sample
    for layer in range(norm1_g.shape[0]):
        w = _prepare(*(t[layer] for t in per_layer))
        y_prompt = _layer(y_prompt, mem_prompt, w)
        y_sample = _layer(y_sample, mem_sample, w)
    return (y_prompt, y_sample)
```

```python
import functools

import jax
import jax.numpy as jnp
from jax import lax
from jax.experimental import pallas as pl
from jax.experimental.pallas import tpu as pltpu

F32 = jnp.float32
BF16 = jnp.bfloat16
I32 = jnp.int32

D_MODEL = 1024
N_MEM = 256
CONV_WIDTH = 768
ATT_HEAD_DIM = 64
DIL_GROUPS = ((128, 1), (512, 4), (2048, 16))
HEADS_PER_GROUP = 4
N_ATT_HEADS = HEADS_PER_GROUP * len(DIL_GROUPS)
ATT_WIDTH = N_ATT_HEADS * ATT_HEAD_DIM
GROUP_WIDTH = HEADS_PER_GROUP * ATT_HEAD_DIM
ATT_RADIUS = 64
XATT_HEADS = 4
XATT_HEAD_DIM = 128
XATT_WIDTH = XATT_HEADS * XATT_HEAD_DIM
N_EXPERTS = 16
EC_CAPACITY = 2
ALIBI_MAX_EXP = 8.0
EPS = 1e-6
NEG_INF = -1e30

C_CB, C_CC, C_AQ, C_XQ, C_GATE, C_END = 0, 768, 2304, 4608, 5120, 8192

V7X_VMEM_LIMIT_BYTES = 56 * 1024 * 1024
LANES = 128
HALO = 16

PROJ_TILE = 512
ATT_QB = 128
ROUTE_TILE = 256
ROUTE_CHUNK = 64
SEG_ALIGN = 16
FFN_TILE = 512
XE_W = D_MODEL + LANES


def _dot(a, b):
    return jnp.dot(a, b, preferred_element_type=F32)


def _dot_nt(a, b):
    return lax.dot_general(a, b, (((1,), (1,)), ((), ())), preferred_element_type=F32)


def _params(sem):
    return pltpu.CompilerParams(dimension_semantics=sem, vmem_limit_bytes=V7X_VMEM_LIMIT_BYTES)


def _full(shape):
    return pl.BlockSpec(shape, lambda *_: (0,) * len(shape))


def _resident(shape):
    return pl.BlockSpec(shape, lambda *_: (0,) * len(shape), pipeline_mode=pl.Buffered(1))


def _memkv_kernel(mem_ref, g_ref, w_ref, kg_ref, k_ref, v_ref):
    m = mem_ref[...]
    hn = (m * lax.rsqrt(jnp.mean(m * m, axis=-1, keepdims=True) + EPS) * g_ref[...]).astype(BF16)
    kv = _dot(hn, w_ref[...])
    ks = []
    for h in range(XATT_HEADS):
        kh = kv[:, h * XATT_HEAD_DIM:(h + 1) * XATT_HEAD_DIM]
        ks.append(kh * lax.rsqrt(jnp.mean(kh * kh, axis=-1, keepdims=True) + EPS) * kg_ref[...])
    k_ref[...] = jnp.concatenate(ks, axis=1).astype(BF16)
    v_ref[...] = kv[:, XATT_WIDTH:].astype(BF16)


def _memkv(mem, mem_g, w_mem_kv, xk_g):
    b, m, d = mem.shape
    out = jax.ShapeDtypeStruct((b, m, XATT_WIDTH), BF16)
    return pl.pallas_call(
        _memkv_kernel,
        grid=(b,),
        in_specs=[pl.BlockSpec((None, m, d), lambda i: (i, 0, 0)), _full((1, d)),
                  _full((d, 2 * XATT_WIDTH)), _full((1, XATT_HEAD_DIM))],
        out_specs=[pl.BlockSpec((None, m, XATT_WIDTH), lambda i: (i, 0, 0))] * 2,
        out_shape=[out, out],
        compiler_params=_params(("arbitrary",)),
        name="memkv",
    )(mem, mem_g, w_mem_kv, xk_g)


def _proj_kernel(xp_ref, x_ref, xn_ref, n1g_ref, win_ref, cw_ref, qg_ref, kg_ref, xqg_ref, kx_ref, vx_ref,
                 wbc_ref, wbx_ref, bd_ref,
                 q0_ref, q1_ref, q2_ref, k0_ref, k1_ref, k2_ref, v0_ref, v1_ref, v2_ref, p_ref, g1_ref,
                 hb_s, u_s, il_s, *, tile, tiles_per_seq):
    tin = pl.program_id(0) % tiles_per_seq
    gain = n1g_ref[...]

    def nrm(x):
        return (x * lax.rsqrt(jnp.mean(x * x, axis=-1, keepdims=True) + EPS) * gain).astype(BF16)

    hb_s[0:HALO, :] = nrm(xp_ref[...])
    hb_s[HALO:HALO + tile, :] = nrm(x_ref[...])
    hb_s[HALO + tile:, :] = nrm(xn_ref[...])
    hc = hb_s[HALO:HALO + tile, :]

    ccx = _dot(hb_s[...], win_ref[:, C_CC:C_AQ])
    u = ccx[:, :CONV_WIDTH] * ccx[:, CONV_WIDTH:]
    u_s[0:HALO, :] = u[0:HALO] * jnp.where(tin == 0, 0.0, 1.0)
    u_s[HALO:HALO + tile, :] = u[HALO:HALO + tile]
    u_s[HALO + tile:, :] = u[HALO + tile:] * jnp.where(tin == tiles_per_seq - 1, 0.0, 1.0)
    cw = cw_ref[...]
    conv = (cw[0:1] * u_s[HALO - 1:HALO - 1 + tile, :] + cw[1:2] * u_s[HALO:HALO + tile, :]
            + cw[2:3] * u_s[HALO + 1:HALO + 1 + tile, :])
    cb = _dot(hc, win_ref[:, C_CB:C_CC])
    y_conv = _dot((cb * conv).astype(BF16), wbc_ref[...])

    qkv = _dot(hc, win_ref[:, C_AQ:C_XQ])
    bd = bd_ref[...]

    def head_norm(z, g_ref):
        outs = []
        for c in range(len(DIL_GROUPS)):
            zc = z[:, c * GROUP_WIDTH:(c + 1) * GROUP_WIDTH]
            ms = _dot((zc * zc).astype(BF16), bd)
            outs.append(zc * lax.rsqrt(ms + EPS) * g_ref[...])
        return outs

    def emit(ref, val, dilation):
        if dilation == 1:
            ref[...] = val.astype(BF16)
            return
        for half in range(GROUP_WIDTH // LANES):
            il_s[half] = val[:, half * LANES:(half + 1) * LANES]
        for r in range(dilation):
            for half in range(GROUP_WIDTH // LANES):
                col = r * GROUP_WIDTH + half * LANES
                ref[:, col:col + LANES] = il_s[half, pl.ds(r, tile // dilation, stride=dilation), :].astype(BF16)

    vals = (head_norm(qkv[:, :ATT_WIDTH], qg_ref) + head_norm(qkv[:, ATT_WIDTH:2 * ATT_WIDTH], kg_ref)
            + [qkv[:, 2 * ATT_WIDTH + c * GROUP_WIDTH:2 * ATT_WIDTH + (c + 1) * GROUP_WIDTH] for c in range(len(DIL_GROUPS))])
    refs = (q0_ref, q1_ref, q2_ref, k0_ref, k1_ref, k2_ref, v0_ref, v1_ref, v2_ref)
    for n, (ref, val) in enumerate(zip(refs, vals)):
        emit(ref, val, DIL_GROUPS[n % len(DIL_GROUPS)][1])

    xq = _dot(hc, win_ref[:, C_XQ:C_GATE])
    kx = kx_ref[...]
    vx = vx_ref[...]
    outs = []
    for h in range(XATT_HEADS):
        hs = slice(h * XATT_HEAD_DIM, (h + 1) * XATT_HEAD_DIM)
        qh = xq[:, hs]
        qh = qh * lax.rsqrt(jnp.mean(qh * qh, axis=-1, keepdims=True) + EPS) * xqg_ref[...]
        s = _dot_nt(qh.astype(BF16), kx[:, hs]) * (XATT_HEAD_DIM ** -0.5)
        p = jnp.exp(s - jnp.max(s, axis=-1, keepdims=True))
        den = jnp.sum(p, axis=-1, keepdims=True)
        outs.append(_dot(p.astype(BF16), vx[:, hs]) / den)
    y_x = _dot(jnp.concatenate(outs, axis=1).astype(BF16), wbx_ref[...])

    gs = jax.nn.sigmoid(_dot(hc, win_ref[:, C_GATE:C_END]))
    p_ref[...] = (gs[:, :D_MODEL] * y_conv + gs[:, 2 * D_MODEL:] * y_x).astype(BF16)
    g1_ref[...] = gs[:, D_MODEL:2 * D_MODEL].astype(BF16)


def _proj(x2, seq_len, kx, vx, w):
    n, d = x2.shape
    tile = PROJ_TILE
    tps = seq_len // tile
    hb = tile // HALO
    n_hblk = n // HALO
    wide = jax.ShapeDtypeStruct((n, d), BF16)
    row = lambda width: pl.BlockSpec((tile, width), lambda i: (i, 0))
    dils = [dil for _, dil in DIL_GROUPS] * 3
    grp_shapes = [jax.ShapeDtypeStruct((n // dil, dil * GROUP_WIDTH), BF16) for dil in dils]
    grp_specs = [pl.BlockSpec((tile // dil, dil * GROUP_WIDTH), lambda i: (i, 0)) for dil in dils]
    return pl.pallas_call(
        functools.partial(_proj_kernel, tile=tile, tiles_per_seq=tps),
        grid=(n // tile,),
        in_specs=[
            pl.BlockSpec((HALO, d), lambda i: (jnp.maximum(i * hb - 1, 0), 0)),
            row(d),
            pl.BlockSpec((HALO, d), lambda i: (jnp.minimum((i + 1) * hb, n_hblk - 1), 0)),
            _full((1, d)), _resident((d, C_END)), _full((3, CONV_WIDTH)),
            _full((1, GROUP_WIDTH)), _full((1, GROUP_WIDTH)), _full((1, XATT_HEAD_DIM)),
            pl.BlockSpec((None, N_MEM, XATT_WIDTH), lambda i: (i // tps, 0, 0)),
            pl.BlockSpec((None, N_MEM, XATT_WIDTH), lambda i: (i // tps, 0, 0)),
            _resident((CONV_WIDTH, d)), _resident((XATT_WIDTH, d)), _full((GROUP_WIDTH, GROUP_WIDTH)),
        ],
        out_specs=grp_specs + [row(d), row(d)],
        out_shape=grp_shapes + [wide, wide],
        scratch_shapes=[pltpu.VMEM((tile + 2 * HALO, d), BF16), pltpu.VMEM((tile + 2 * HALO, CONV_WIDTH), F32),
                        pltpu.VMEM((GROUP_WIDTH // LANES, tile, LANES), F32)],
        compiler_params=_params(("arbitrary",)),
        name="proj",
    )(x2, x2, x2, w["norm1_g"], w["w_in"], w["conv_w"], w["q_gain"], w["k_gain"], w["xq_gain"], kx, vx,
      w["w_br_conv"], w["w_br_xattn"], w["head_avg"])


def _attn_kernel(q_ref, k_ref, v_ref, bias_ref, o_ref, lse_ref, *, seq_sub, lq, qb, kw, n_res):
    i = pl.program_id(2)
    lane_head = lax.broadcasted_iota(I32, (qb, GROUP_WIDTH), 1) // ATT_HEAD_DIM
    head_mask = [lane_head == h for h in range(HEADS_PER_GROUP)]
    head_mask_bf = [jnp.where(m, 1.0, 0.0).astype(BF16) for m in head_mask]

    def block(sb, carry):
        row0 = pl.multiple_of(sb * qb, qb)
        qs = i * lq + row0
        ks = pl.multiple_of(jnp.clip(qs - ATT_RADIUS, 0, seq_sub - kw), ATT_RADIUS)
        case = jnp.where(qs == 0, 0, jnp.where(qs == seq_sub - qb, 2, 1))
        for r in range(n_res):
            cs = slice(r * GROUP_WIDTH, (r + 1) * GROUP_WIDTH)
            q = q_ref[pl.ds(row0, qb), cs]
            kk = k_ref[pl.ds(ks, kw), cs]
            vv = v_ref[pl.ds(ks, kw), cs]
            s = _dot_nt(jnp.concatenate([q * m for m in head_mask_bf], axis=0), kk)
            ps, inv_den, lse = [], [], []
            for h in range(HEADS_PER_GROUP):
                sh = s[h * qb:(h + 1) * qb] + bias_ref[case * HEADS_PER_GROUP + h]
                m = jnp.max(sh, axis=-1, keepdims=True)
                p = jnp.exp(sh - m)
                den = jnp.sum(p, axis=-1, keepdims=True)
                ps.append(p.astype(BF16))
                inv_den.append(1.0 / den)
                lse.append(m + jnp.log(den))
            of = _dot(jnp.concatenate(ps, axis=0), vv)
            o = jnp.zeros((qb, GROUP_WIDTH), F32)
            lb = jnp.zeros((qb, GROUP_WIDTH), F32)
            for h in range(HEADS_PER_GROUP):
                o = jnp.where(head_mask[h], of[h * qb:(h + 1) * qb] * inv_den[h], o)
                lb = jnp.where(head_mask[h], lse[h], lb)
            o_ref[pl.ds(row0, qb), cs] = o.astype(BF16)
            lse_ref[pl.ds(row0, qb), cs] = lb
        return carry

    lax.fori_loop(0, lq // qb, block, 0)


def _attn_bias(dilation, qb, kw, group):
    slopes = jnp.exp2(-ALIBI_MAX_EXP * jnp.arange(1, N_ATT_HEADS + 1, dtype=F32) / N_ATT_HEADS)
    slopes = slopes[group * HEADS_PER_GROUP:(group + 1) * HEADS_PER_GROUP]
    a = jnp.arange(qb)[:, None]
    c = jnp.arange(kw)[None, :]
    tabs = []
    for off in (0, -ATT_RADIUS, qb - kw):
        delta = off + c - a
        dist = (dilation * jnp.abs(delta)).astype(F32)
        bias = -slopes[:, None, None] * dist[None]
        tabs.append(jnp.where((jnp.abs(delta) <= ATT_RADIUS)[None], bias, NEG_INF))
    return jnp.concatenate(tabs, axis=0)


def _attn(q, k, v, batch, seq_len, group):
    _, dilation = DIL_GROUPS[group]
    seq_sub = seq_len // dilation
    qb = min(ATT_QB, seq_sub)
    kw = min(qb + 2 * ATT_RADIUS, seq_sub)
    lq = min(4 * qb, seq_sub)
    n_res = min(dilation, 4)
    width = n_res * GROUP_WIDTH
    view = lambda t: t.reshape(batch, seq_sub, dilation * GROUP_WIDTH)
    qspec = pl.BlockSpec((None, lq, width), lambda b, r, i: (b, i, r))
    kspec = pl.BlockSpec((None, seq_sub, width), lambda b, r, i: (b, 0, r))
    bias = _attn_bias(dilation, qb, kw, group)
    o, lse = pl.pallas_call(
        functools.partial(_attn_kernel, seq_sub=seq_sub, lq=lq, qb=qb, kw=kw, n_res=n_res),
        grid=(batch, dilation // n_res, seq_sub // lq),
        in_specs=[qspec, kspec, kspec, _full(bias.shape)],
        out_specs=[qspec, qspec],
        out_shape=[jax.ShapeDtypeStruct((batch, seq_sub, dilation * GROUP_WIDTH), BF16),
                   jax.ShapeDtypeStruct((batch, seq_sub, dilation * GROUP_WIDTH), F32)],
        compiler_params=_params(("arbitrary", "arbitrary", "arbitrary")),
        name=f"attn_d{dilation}",
    )(view(q), view(k), view(v), bias)
    rows = batch * seq_sub
    return o.reshape(rows, dilation * GROUP_WIDTH), lse.reshape(rows, dilation * GROUP_WIDTH)


def _merge_kernel(x_ref, p_ref, g1_ref, o0_ref, o1_ref, o2_ref, l0_ref, l1_ref, l2_ref, wba_ref, wo_ref,
                  n2g_ref, wr_ref, x1_ref, h2e_ref, aff_ref, *il_s, tile):
    def token_order(ref, dilation, scratch):
        if dilation == 1:
            return ref[...].astype(F32)
        halves = GROUP_WIDTH // LANES
        for r in range(dilation):
            for half in range(halves):
                col = r * GROUP_WIDTH + half * LANES
                scratch[half, pl.ds(r, tile // dilation, stride=dilation), :] = ref[:, col:col + LANES].astype(F32)
        return jnp.concatenate([scratch[half] for half in range(halves)], axis=1)

    dils = [dil for _, dil in DIL_GROUPS]
    o0, o1, o2 = (token_order(r, dil, s) for r, dil, s in zip((o0_ref, o1_ref, o2_ref), dils, il_s[:3]))
    l0, l1, l2 = (token_order(r, dil, s) for r, dil, s in zip((l0_ref, l1_ref, l2_ref), dils, il_s[3:]))
    m = jnp.maximum(jnp.maximum(l0, l1), l2)
    e0, e1, e2 = jnp.exp(l0 - m), jnp.exp(l1 - m), jnp.exp(l2 - m)
    o = (e0 * o0 + e1 * o1 + e2 * o2) / (e0 + e1 + e2)
    y_attn = _dot(o.astype(BF16), wba_ref[...])
    merged = p_ref[...].astype(F32) + g1_ref[...].astype(F32) * y_attn
    x1 = x_ref[...] + _dot(merged.astype(BF16), wo_ref[...])
    x1_ref[...] = x1
    h2 = (x1 * lax.rsqrt(jnp.mean(x1 * x1, axis=-1, keepdims=True) + EPS) * n2g_ref[...]).astype(BF16)
    h2e_ref[:, :D_MODEL] = h2
    lg = _dot(h2, wr_ref[...])
    lane = lax.broadcasted_iota(I32, lg.shape, 1)
    first = lane < N_EXPERTS
    mx = jnp.max(jnp.where(first, lg, -jnp.inf), axis=-1, keepdims=True)
    ex = jnp.exp(lg - mx)
    a = ex / jnp.sum(jnp.where(first, ex, 0.0), axis=-1, keepdims=True)
    aff_ref[...] = a
    hi = a.astype(BF16).astype(F32)
    mid = (a - hi).astype(BF16).astype(F32)
    lo = (a - hi) - mid
    ext = jnp.where(first, hi, jnp.where(lane < 2 * N_EXPERTS, mid, jnp.where(lane < 3 * N_EXPERTS, lo, 0.0)))
    h2e_ref[:, D_MODEL:] = ext.astype(BF16)


def _merge(x2, p, g1, os_, ls_, w):
    n, d = x2.shape
    tile = PROJ_TILE
    row = lambda width: pl.BlockSpec((tile, width), lambda i: (i, 0))
    grp = [pl.BlockSpec((tile // dil, dil * GROUP_WIDTH), lambda i: (i, 0)) for _, dil in DIL_GROUPS]
    return pl.pallas_call(
        functools.partial(_merge_kernel, tile=tile),
        grid=(n // tile,),
        in_specs=[row(d), row(d), row(d)] + grp + grp
                 + [_full((GROUP_WIDTH, d)), _full((d, d)), _full((1, d)), _full((d, LANES))],
        out_specs=[row(d), row(d + LANES), row(LANES)],
        out_shape=[jax.ShapeDtypeStruct((n, d), F32), jax.ShapeDtypeStruct((n, d + LANES), BF16),
                   jax.ShapeDtypeStruct((n, LANES), F32)],
        scratch_shapes=[pltpu.VMEM((GROUP_WIDTH // LANES, tile, LANES), F32)] * 6,
        compiler_params=_params(("arbitrary",)),
        name="merge",
    )(x2, p, g1, *os_, *ls_, w["w_br_attn"], w["w_o"], w["norm2_g"], w["w_router3"])


def _select_kernel(aff_ref, slot_ref, starts_ref, nchunk_ref, tot_ref, *, n_tok, cap, tile):
    n_tiles = n_tok // tile

    def bisect(b, thr_bits):
        cand = thr_bits | jnp.left_shift(jnp.int32(1), 30 - b)
        cnt = jnp.sum(jnp.where(aff_ref[...] >= lax.bitcast_convert_type(cand, F32), 1.0, 0.0), axis=1, keepdims=True)
        return jnp.where(cnt >= cap, cand, thr_bits)

    thr = lax.bitcast_convert_type(lax.fori_loop(0, 31, bisect, jnp.zeros((N_EXPERTS, 1), I32)), F32)
    need = cap - jnp.sum(jnp.where(aff_ref[...] > thr, 1.0, 0.0), axis=1, keepdims=True)

    r = lax.broadcasted_iota(I32, (tile, tile), 0)
    c = lax.broadcasted_iota(I32, (tile, tile), 1)
    before = jnp.where(r < c, 1.0, 0.0).astype(BF16)

    def tile_body(i, carry):
        start, eq_seen = carry
        off = pl.multiple_of(i * tile, tile)
        a = aff_ref[:, pl.ds(off, tile)]
        eq = jnp.where(a == thr, 1.0, 0.0)
        eq_rank = eq_seen + _dot(eq.astype(BF16), before)
        sel = jnp.where(a > thr, 1.0, jnp.where(eq_rank < need, eq, 0.0))
        rank = _dot(sel.astype(BF16), before)
        slot_ref[:, pl.ds(off, tile)] = jnp.where(sel > 0.0, rank, -1.0).astype(I32)
        cnt = jnp.sum(sel, axis=1, keepdims=True)
        starts_ref[i] = jnp.broadcast_to(start, (N_EXPERTS, LANES)).astype(I32)
        head = start - jnp.floor(start * (1.0 / SEG_ALIGN)) * SEG_ALIGN
        nch = jnp.max(jnp.floor((head + cnt) * (1.0 / ROUTE_CHUNK)) + 1.0, axis=0, keepdims=True)
        nchunk_ref[i] = jnp.broadcast_to(nch, (8, LANES)).astype(I32)
        return start + cnt, eq_seen + jnp.sum(eq, axis=1, keepdims=True)

    zero = jnp.zeros((N_EXPERTS, 1), F32)
    total, _ = lax.fori_loop(0, n_tiles, tile_body, (zero, zero))
    tot_ref[...] = jnp.broadcast_to(total, (N_EXPERTS, LANES)).astype(I32)


def _select(aff_t, cap):
    n_exp, n = aff_t.shape
    tile = ROUTE_TILE
    n_tiles = n // tile
    slot, starts, nchunk, tot = pl.pallas_call(
        functools.partial(_select_kernel, n_tok=n, cap=cap, tile=tile),
        grid=(1,),
        in_specs=[_full((n_exp, n))],
        out_specs=[_full((n_exp, n)), _full((n_tiles, n_exp, LANES)), _full((n_tiles, 8, LANES)), _full((n_exp, LANES))],
        out_shape=[jax.ShapeDtypeStruct((n_exp, n), I32), jax.ShapeDtypeStruct((n_tiles, n_exp, LANES), I32),
                   jax.ShapeDtypeStruct((n_tiles, 8, LANES), I32), jax.ShapeDtypeStruct((n_exp, LANES), I32)],
        compiler_params=_params(("arbitrary",)),
        name="select",
    )(aff_t)
    return slot, starts[:, :, 0], nchunk[:, 0, 0], tot[:, 0]


def _seg_base(starts_ref, step, e):
    return pl.multiple_of((starts_ref[step * N_EXPERTS + e] // SEG_ALIGN) * SEG_ALIGN, SEG_ALIGN)


def _dispatch_kernel(starts_ref, nchunk_ref, slot_ref, h_ref, scol_ref, xe_hbm, stage, head, sem, *, tile, chunk):
    i = pl.program_id(0)
    sl = i % 2

    def copies(step, c, buf):
        return [pltpu.make_async_copy(
            stage.at[buf, pl.ds(e * chunk, chunk), :],
            xe_hbm.at[e, pl.ds(_seg_base(starts_ref, step, e) + c * chunk, chunk), :],
            sem.at[buf]) for e in range(N_EXPERTS)]

    @pl.when(i == 0)
    def _():
        head[...] = jnp.zeros(head.shape, BF16)

    slot = slot_ref[...]
    pos = jnp.where(slot >= 0, slot + (scol_ref[...] & (SEG_ALIGN - 1)), -1)

    def build(c, buf):
        want = lax.broadcasted_iota(I32, (chunk, tile), 0) + c * chunk
        onehot = jnp.concatenate(
            [jnp.where(pos[e:e + 1, :] == want, 1.0, 0.0).astype(BF16) for e in range(N_EXPERTS)], axis=0)
        stage[buf] = _dot(onehot, h_ref[...]).astype(BF16)

    def keep_head(c, buf):
        @pl.when(i + 1 < pl.num_programs(0))
        def _():
            for e in range(N_EXPERTS):
                rel = starts_ref[(i + 1) * N_EXPERTS + e] - _seg_base(starts_ref, i, e)

                @pl.when(rel // chunk == c)
                def _():
                    row = pl.multiple_of(e * chunk + ((rel % chunk) // SEG_ALIGN) * SEG_ALIGN, SEG_ALIGN)
                    head[e] = stage[buf, pl.ds(row, SEG_ALIGN), :]

    build(0, sl)
    for e in range(N_EXPERTS):
        stage[sl, e * chunk:e * chunk + SEG_ALIGN, :] = stage[sl, e * chunk:e * chunk + SEG_ALIGN, :] + head[e]
    keep_head(0, sl)

    @pl.when(i > 0)
    def _():
        for cp in copies(i - 1, 0, 1 - sl):
            cp.wait()

    for cp in copies(i, 0, sl):
        cp.start()

    def overflow(c, carry):
        for cp in copies(i, c - 1, sl):
            cp.wait()
        build(c, sl)
        keep_head(c, sl)
        for cp in copies(i, c, sl):
            cp.start()
        return carry

    lax.fori_loop(1, nchunk_ref[i], overflow, 0)

    @pl.when(i == pl.num_programs(0) - 1)
    def _():
        for cp in copies(i, 0, sl):
            cp.wait()
        rows = xe_hbm.shape[1]
        zbuf = 1 - sl
        stage[zbuf] = jnp.zeros(stage.shape[1:], BF16)
        end = [_seg_base(starts_ref, i, e) + jnp.maximum(nchunk_ref[i], 1) * chunk for e in range(N_EXPERTS)]
        n_full = [(rows - end[e]) // chunk for e in range(N_EXPERTS)]

        def zero_copy(e, pos):
            return pltpu.make_async_copy(stage.at[zbuf, pl.ds(e * chunk, chunk), :],
                                         xe_hbm.at[e, pl.ds(pl.multiple_of(pos, SEG_ALIGN), chunk), :], sem.at[zbuf])

        def fill(k, carry):
            for e in range(N_EXPERTS):
                @pl.when(k < n_full[e])
                def _():
                    zero_copy(e, end[e] + k * chunk).start()
            for e in range(N_EXPERTS):
                @pl.when(k < n_full[e])
                def _():
                    zero_copy(e, end[e] + k * chunk).wait()
            return carry

        lax.fori_loop(0, functools.reduce(jnp.maximum, n_full), fill, 0)
        for e in range(N_EXPERTS):
            zero_copy(e, rows - chunk).start()
        for e in range(N_EXPERTS):
            zero_copy(e, rows - chunk).wait()


def _dispatch(slot, h2e, starts, nchunk, rows):
    n = h2e.shape[0]
    tile, chunk = ROUTE_TILE, ROUTE_CHUNK
    n_tiles = n // tile
    return pl.pallas_call(
        functools.partial(_dispatch_kernel, tile=tile, chunk=chunk),
        grid_spec=pltpu.PrefetchScalarGridSpec(
            num_scalar_prefetch=2, grid=(n_tiles,),
            in_specs=[pl.BlockSpec((N_EXPERTS, tile), lambda i, *_: (0, i)),
                      pl.BlockSpec((tile, D_MODEL + LANES), lambda i, *_: (i, 0)),
                      pl.BlockSpec((None, N_EXPERTS, 1), lambda i, *_: (i, 0, 0))],
            out_specs=pl.BlockSpec(memory_space=pl.ANY),
            scratch_shapes=[pltpu.VMEM((2, N_EXPERTS * chunk, XE_W), BF16),
                            pltpu.VMEM((N_EXPERTS, SEG_ALIGN, XE_W), BF16), pltpu.SemaphoreType.DMA((2,))]),
        out_shape=jax.ShapeDtypeStruct((N_EXPERTS, rows, XE_W), BF16),
        compiler_params=_params(("arbitrary",)),
        name="dispatch",
    )(starts.reshape(-1), nchunk, slot, h2e, starts.reshape(n_tiles, N_EXPERTS, 1))


def _ffn_kernel(tot_ref, xe_ref, wg_ref, wu_ref, wd_ref, ye_ref, *, tile):
    e = pl.program_id(0)
    j = pl.program_id(1)
    live = tot_ref[e] - j * tile

    @pl.when(live > 0)
    def _():
        x = xe_ref[:, :D_MODEL]
        pieces = xe_ref[:, D_MODEL:].astype(F32)
        lane = lax.broadcasted_iota(I32, pieces.shape, 1)
        mine = ((lane & (N_EXPERTS - 1)) == e) & (lane < 3 * N_EXPERTS)
        gate = jnp.sum(jnp.where(mine, pieces, 0.0), axis=-1, keepdims=True)
        g = _dot(x, wg_ref[...])
        u = _dot(x, wu_ref[...])
        hid = (g * jax.nn.sigmoid(g) * u).astype(BF16)
        y = _dot(hid, wd_ref[...]) * gate
        rows = lax.broadcasted_iota(I32, y.shape, 0)
        y = jnp.where(rows < live, y, 0.0)
        ye_ref[...] = y.astype(BF16)

    @pl.when(live <= 0)
    def _():
        ye_ref[...] = jnp.zeros(ye_ref.shape, BF16)


def _ffn(xe, tot, w_gate, w_up, w_down):
    n_exp, rows, _ = xe.shape
    tile = FFN_TILE
    d, f = w_gate.shape[1:]
    last = lambda e, j, tot: jnp.minimum(j, jnp.maximum(tot[e] - 1, 0) // tile)
    wspec = lambda a, b: pl.BlockSpec((None, a, b), lambda e, j, tot: (e, 0, 0))
    return pl.pallas_call(
        functools.partial(_ffn_kernel, tile=tile),
        grid_spec=pltpu.PrefetchScalarGridSpec(
            num_scalar_prefetch=1, grid=(n_exp, rows // tile),
            in_specs=[pl.BlockSpec((None, tile, XE_W), lambda e, j, tot: (e, last(e, j, tot), 0)),
                      wspec(d, f), wspec(d, f), wspec(f, d)],
            out_specs=pl.BlockSpec((None, tile, d), lambda e, j, tot: (e, j, 0))),
        out_shape=jax.ShapeDtypeStruct((n_exp, rows, d), BF16),
        compiler_params=_params(("arbitrary", "arbitrary")),
        name="ffn",
    )(tot, xe, w_gate, w_up, w_down)


def _combine_kernel(starts_ref, nchunk_ref, x1_ref, slot_ref, srow_ref, ye_hbm, out_ref, stage, sem, *, tile, chunk):
    i = pl.program_id(0)
    sl = i % 2
    over = 2

    def copies(step, c, buf):
        return [pltpu.make_async_copy(
            ye_hbm.at[e, pl.ds(_seg_base(starts_ref, step, e) + c * chunk, chunk), :],
            stage.at[buf, pl.ds(e * chunk, chunk), :],
            sem.at[buf]) for e in range(N_EXPERTS)]

    @pl.when(i == 0)
    def _():
        for cp in copies(0, 0, 0):
            cp.start()

    @pl.when(i + 1 < pl.num_programs(0))
    def _():
        for cp in copies(i + 1, 0, 1 - sl):
            cp.start()

    col = lax.broadcasted_iota(I32, (N_EXPERTS, N_EXPERTS * chunk), 1)
    exp_row = lax.broadcasted_iota(I32, (N_EXPERTS, N_EXPERTS * chunk), 0)
    spread = jnp.where(col // chunk == exp_row, 1.0, 0.0).astype(BF16)
    slot = slot_ref[...]
    rank = jnp.where(slot < 0, -float(tile), slot.astype(F32)).astype(BF16)
    head = jnp.broadcast_to((srow_ref[...] & (SEG_ALIGN - 1)).astype(F32).astype(BF16), (8, N_EXPERTS))
    slots = _dot(rank, spread) + _dot(head, spread)[0:1]
    lane_slot = (lax.broadcasted_iota(I32, slots.shape, 1) & (chunk - 1)).astype(F32)

    def gathered(c, buf):
        onehot = jnp.where(slots == lane_slot + c * chunk, 1.0, 0.0).astype(BF16)
        return _dot(onehot, stage[buf])

    for cp in copies(i, 0, sl):
        cp.wait()
    out_ref[...] = x1_ref[...] + gathered(0, sl)

    def overflow(c, carry):
        for cp in copies(i, c, over):
            cp.start()
        for cp in copies(i, c, over):
            cp.wait()
        out_ref[...] += gathered(c, over)
        return carry

    lax.fori_loop(1, nchunk_ref[i], overflow, 0)


def _combine(x1, slot_t, ye, starts, nchunk):
    n, d = x1.shape
    tile, chunk = ROUTE_TILE, ROUTE_CHUNK
    n_tiles = n // tile
    return pl.pallas_call(
        functools.partial(_combine_kernel, tile=tile, chunk=chunk),
        grid_spec=pltpu.PrefetchScalarGridSpec(
            num_scalar_prefetch=2, grid=(n_tiles,),
            in_specs=[pl.BlockSpec((tile, d), lambda i, *_: (i, 0)),
                      pl.BlockSpec((tile, N_EXPERTS), lambda i, *_: (i, 0)),
                      pl.BlockSpec((None, 1, N_EXPERTS), lambda i, *_: (i, 0, 0)),
                      pl.BlockSpec(memory_space=pl.ANY)],
            out_specs=pl.BlockSpec((tile, d), lambda i, *_: (i, 0)),
            scratch_shapes=[pltpu.VMEM((3, N_EXPERTS * chunk, d), BF16), pltpu.SemaphoreType.DMA((3,))]),
        out_shape=jax.ShapeDtypeStruct((n, d), F32),
        compiler_params=_params(("arbitrary",)),
        name="combine",
    )(starts.reshape(-1), nchunk, x1, slot_t, starts.reshape(n_tiles, 1, N_EXPERTS), ye)


def _expert_rows(cap):
    need = cap + ROUTE_TILE + 2 * ROUTE_CHUNK
    return -(-need // FFN_TILE) * FFN_TILE


def _routed_ffn(x1, h2e, aff, w):
    n = x1.shape[0]
    cap = max(1, EC_CAPACITY * n // N_EXPERTS)
    slot, starts, nchunk, tot = _select(aff[:, :N_EXPERTS].T, cap)
    xe = _dispatch(slot, h2e, starts, nchunk, _expert_rows(cap))
    ye = _ffn(xe, tot, w["w_exp_gate"], w["w_exp_up"], w["w_exp_down"])
    return _combine(x1, slot.T, ye, starts, nchunk)


def _layer(x, mem, w):
    b, s, d = x.shape
    x2 = x.reshape(b * s, d)
    kx, vx = _memkv(mem, w["mem_norm_g"], w["w_mem_kv"], w["xk_gain"])
    *qkv, p, g1 = _proj(x2, s, kx, vx, w)
    os_, ls_ = [], []
    for g in range(len(DIL_GROUPS)):
        o, lse = _attn(qkv[g], qkv[3 + g], qkv[6 + g], b, s, g)
        os_.append(o)
        ls_.append(lse)
    x1, h2e, aff = _merge(x2, p, g1, os_, ls_, w)
    return _routed_ffn(x1, h2e, aff, w).reshape(b, s, d)


def _prepare(norm1_g, w_in, conv_w, q_norm_g, k_norm_g, mem_norm_g, w_mem_kv, xq_norm_g, xk_norm_g,
             w_br_conv, w_br_attn, w_br_xattn, w_o, norm2_g, w_router, w_exp_gate, w_exp_up, w_exp_down):
    row = lambda v: v.reshape(1, -1).astype(F32)
    head = jnp.arange(GROUP_WIDTH) // ATT_HEAD_DIM
    return {
        "norm1_g": row(norm1_g), "norm2_g": row(norm2_g), "mem_norm_g": row(mem_norm_g),
        "w_in": w_in.astype(BF16), "conv_w": conv_w.astype(F32),
        "q_gain": row(jnp.tile(q_norm_g, HEADS_PER_GROUP) * (ATT_HEAD_DIM ** -0.5)),
        "k_gain": row(jnp.tile(k_norm_g, HEADS_PER_GROUP)),
        "xq_gain": row(xq_norm_g), "xk_gain": row(xk_norm_g),
        "w_mem_kv": w_mem_kv.astype(BF16), "w_br_conv": w_br_conv.astype(BF16),
        "w_br_attn": w_br_attn.astype(BF16), "w_br_xattn": w_br_xattn.astype(BF16), "w_o": w_o.astype(BF16),
        "head_avg": jnp.where(head[:, None] == head[None, :], 1.0 / ATT_HEAD_DIM, 0.0).astype(BF16),
        "w_router3": jnp.concatenate([w_router] * 3 + [jnp.zeros((D_MODEL, LANES - 3 * N_EXPERTS), F32)],
                                     axis=1).astype(BF16),
        "w_exp_gate": w_exp_gate.astype(BF16), "w_exp_up": w_exp_up.astype(BF16), "w_exp_down": w_exp_down.astype(BF16),
    }


def kernel(x_prompt, x_sample, mem_prompt, mem_sample, norm1_g, w_in, conv_w, q_norm_g, k_norm_g, mem_norm_g, w_mem_kv, xq_norm_g, xk_norm_g, w_br_conv, w_br_attn, w_br_xattn, w_o, norm2_g, w_router, w_exp_gate, w_exp_up, w_exp_down):
    per_layer = (norm1_g, w_in, conv_w, q_norm_g, k_norm_g, mem_norm_g, w_mem_kv, xq_norm_g, xk_norm_g,
                 w_br_conv, w_br_attn, w_br_xattn, w_o, norm2_g, w_router, w_exp_gate, w_exp_up, w_exp_down)
    y_prompt, y_sample = x_prompt, x_sample
    for layer in range(norm1_g.shape[0]):
        w = _prepare(*(t[layer] for t in per_layer))
        y_prompt = _layer(y_prompt, mem_prompt, w)
        y_sample = _layer(y_sample, mem_sample, w)
    return (y_prompt, y_sample)
```

```python
import functools

import jax
import jax.numpy as jnp
from jax import lax
from jax.experimental import pallas as pl
from jax.experimental.pallas import tpu as pltpu

F32 = jnp.float32
BF16 = jnp.bfloat16
I32 = jnp.int32

D_MODEL = 1024
N_MEM = 256
CONV_WIDTH = 768
ATT_HEAD_DIM = 64
DIL_GROUPS = ((128, 1), (512, 4), (2048, 16))
HEADS_PER_GROUP = 4
N_ATT_HEADS = HEADS_PER_GROUP * len(DIL_GROUPS)
ATT_WIDTH = N_ATT_HEADS * ATT_HEAD_DIM
GROUP_WIDTH = HEADS_PER_GROUP * ATT_HEAD_DIM
ATT_RADIUS = 64
XATT_HEADS = 4
XATT_HEAD_DIM = 128
XATT_WIDTH = XATT_HEADS * XATT_HEAD_DIM
N_EXPERTS = 16
EC_CAPACITY = 2
ALIBI_MAX_EXP = 8.0
EPS = 1e-6
NEG_INF = -1e30

C_CB, C_CC, C_AQ, C_XQ, C_GATE, C_END = 0, 768, 2304, 4608, 5120, 8192

V7X_VMEM_LIMIT_BYTES = 56 * 1024 * 1024
LANES = 128
HALO = 16

PROJ_TILE = 512
ATT_QB = 128
ROUTE_TILE = 256
ROUTE_CHUNK = 64
SEG_ALIGN = 16
FFN_TILE = 512
FFN_COLS = 256
XE_W = D_MODEL + LANES


def _dot(a, b):
    return jnp.dot(a, b, preferred_element_type=F32)


def _dot_nt(a, b):
    return lax.dot_general(a, b, (((1,), (1,)), ((), ())), preferred_element_type=F32)


def _params(sem):
    return pltpu.CompilerParams(dimension_semantics=sem, vmem_limit_bytes=V7X_VMEM_LIMIT_BYTES)


def _full(shape):
    return pl.BlockSpec(shape, lambda *_: (0,) * len(shape))


def _resident(shape):
    return pl.BlockSpec(shape, lambda *_: (0,) * len(shape), pipeline_mode=pl.Buffered(1))


def _memkv_kernel(mem_ref, g_ref, w_ref, kg_ref, k_ref, v_ref):
    m = mem_ref[...]
    hn = (m * lax.rsqrt(jnp.mean(m * m, axis=-1, keepdims=True) + EPS) * g_ref[...]).astype(BF16)
    kv = _dot(hn, w_ref[...])
    ks = []
    for h in range(XATT_HEADS):
        kh = kv[:, h * XATT_HEAD_DIM:(h + 1) * XATT_HEAD_DIM]
        ks.append(kh * lax.rsqrt(jnp.mean(kh * kh, axis=-1, keepdims=True) + EPS) * kg_ref[...])
    k_ref[...] = jnp.concatenate(ks, axis=1).astype(BF16)
    v_ref[...] = kv[:, XATT_WIDTH:].astype(BF16)


def _memkv(mem, mem_g, w_mem_kv, xk_g):
    b, m, d = mem.shape
    out = jax.ShapeDtypeStruct((b, m, XATT_WIDTH), BF16)
    return pl.pallas_call(
        _memkv_kernel,
        grid=(b,),
        in_specs=[pl.BlockSpec((None, m, d), lambda i: (i, 0, 0)), _full((1, d)),
                  _full((d, 2 * XATT_WIDTH)), _full((1, XATT_HEAD_DIM))],
        out_specs=[pl.BlockSpec((None, m, XATT_WIDTH), lambda i: (i, 0, 0))] * 2,
        out_shape=[out, out],
        compiler_params=_params(("arbitrary",)),
        name="memkv",
    )(mem, mem_g, w_mem_kv, xk_g)


def _proj_kernel(xp_ref, x_ref, xn_ref, n1g_ref, win_ref, cw_ref, qg_ref, kg_ref, xqg_ref, kx_ref, vx_ref,
                 wbc_ref, wbx_ref, bd_ref,
                 q0_ref, q1_ref, q2_ref, k0_ref, k1_ref, k2_ref, v0_ref, v1_ref, v2_ref, p_ref, g1_ref,
                 hb_s, u_s, il_s, *, tile, tiles_per_seq):
    tin = pl.program_id(0) % tiles_per_seq
    gain = n1g_ref[...]

    def nrm(x):
        return (x * lax.rsqrt(jnp.mean(x * x, axis=-1, keepdims=True) + EPS) * gain).astype(BF16)

    hb_s[0:HALO, :] = nrm(xp_ref[...])
    hb_s[HALO:HALO + tile, :] = nrm(x_ref[...])
    hb_s[HALO + tile:, :] = nrm(xn_ref[...])
    hc = hb_s[HALO:HALO + tile, :]

    ccx = _dot(hb_s[...], win_ref[:, C_CC:C_AQ])
    u = ccx[:, :CONV_WIDTH] * ccx[:, CONV_WIDTH:]
    u_s[0:HALO, :] = u[0:HALO] * jnp.where(tin == 0, 0.0, 1.0)
    u_s[HALO:HALO + tile, :] = u[HALO:HALO + tile]
    u_s[HALO + tile:, :] = u[HALO + tile:] * jnp.where(tin == tiles_per_seq - 1, 0.0, 1.0)
    cw = cw_ref[...]
    conv = (cw[0:1] * u_s[HALO - 1:HALO - 1 + tile, :] + cw[1:2] * u_s[HALO:HALO + tile, :]
            + cw[2:3] * u_s[HALO + 1:HALO + 1 + tile, :])
    cb = _dot(hc, win_ref[:, C_CB:C_CC])
    y_conv = _dot((cb * conv).astype(BF16), wbc_ref[...])

    qkv = _dot(hc, win_ref[:, C_AQ:C_XQ])
    bd = bd_ref[...]

    def head_norm(z, g_ref):
        outs = []
        for c in range(len(DIL_GROUPS)):
            zc = z[:, c * GROUP_WIDTH:(c + 1) * GROUP_WIDTH]
            ms = _dot((zc * zc).astype(BF16), bd)
            outs.append(zc * lax.rsqrt(ms + EPS) * g_ref[...])
        return outs

    def emit(ref, val, dilation):
        if dilation == 1:
            ref[...] = val.astype(BF16)
            return
        for half in range(GROUP_WIDTH // LANES):
            il_s[half] = val[:, half * LANES:(half + 1) * LANES]
        for r in range(dilation):
            for half in range(GROUP_WIDTH // LANES):
                col = r * GROUP_WIDTH + half * LANES
                ref[:, col:col + LANES] = il_s[half, pl.ds(r, tile // dilation, stride=dilation), :].astype(BF16)

    vals = (head_norm(qkv[:, :ATT_WIDTH], qg_ref) + head_norm(qkv[:, ATT_WIDTH:2 * ATT_WIDTH], kg_ref)
            + [qkv[:, 2 * ATT_WIDTH + c * GROUP_WIDTH:2 * ATT_WIDTH + (c + 1) * GROUP_WIDTH] for c in range(len(DIL_GROUPS))])
    refs = (q0_ref, q1_ref, q2_ref, k0_ref, k1_ref, k2_ref, v0_ref, v1_ref, v2_ref)
    for n, (ref, val) in enumerate(zip(refs, vals)):
        emit(ref, val, DIL_GROUPS[n % len(DIL_GROUPS)][1])

    xq = _dot(hc, win_ref[:, C_XQ:C_GATE])
    kx = kx_ref[...]
    vx = vx_ref[...]
    outs = []
    for h in range(XATT_HEADS):
        hs = slice(h * XATT_HEAD_DIM, (h + 1) * XATT_HEAD_DIM)
        qh = xq[:, hs]
        qh = qh * lax.rsqrt(jnp.mean(qh * qh, axis=-1, keepdims=True) + EPS) * xqg_ref[...]
        s = _dot_nt(qh.astype(BF16), kx[:, hs]) * (XATT_HEAD_DIM ** -0.5)
        p = jnp.exp(s - jnp.max(s, axis=-1, keepdims=True))
        den = jnp.sum(p, axis=-1, keepdims=True)
        outs.append(_dot(p.astype(BF16), vx[:, hs]) / den)
    y_x = _dot(jnp.concatenate(outs, axis=1).astype(BF16), wbx_ref[...])

    gs = jax.nn.sigmoid(_dot(hc, win_ref[:, C_GATE:C_END]))
    p_ref[...] = (gs[:, :D_MODEL] * y_conv + gs[:, 2 * D_MODEL:] * y_x).astype(BF16)
    g1_ref[...] = gs[:, D_MODEL:2 * D_MODEL].astype(BF16)


def _proj(x2, seq_len, kx, vx, w):
    n, d = x2.shape
    tile = PROJ_TILE
    tps = seq_len // tile
    hb = tile // HALO
    n_hblk = n // HALO
    wide = jax.ShapeDtypeStruct((n, d), BF16)
    row = lambda width: pl.BlockSpec((tile, width), lambda i: (i, 0))
    dils = [dil for _, dil in DIL_GROUPS] * 3
    grp_shapes = [jax.ShapeDtypeStruct((n // dil, dil * GROUP_WIDTH), BF16) for dil in dils]
    grp_specs = [pl.BlockSpec((tile // dil, dil * GROUP_WIDTH), lambda i: (i, 0)) for dil in dils]
    return pl.pallas_call(
        functools.partial(_proj_kernel, tile=tile, tiles_per_seq=tps),
        grid=(n // tile,),
        in_specs=[
            pl.BlockSpec((HALO, d), lambda i: (jnp.maximum(i * hb - 1, 0), 0)),
            row(d),
            pl.BlockSpec((HALO, d), lambda i: (jnp.minimum((i + 1) * hb, n_hblk - 1), 0)),
            _full((1, d)), _resident((d, C_END)), _full((3, CONV_WIDTH)),
            _full((1, GROUP_WIDTH)), _full((1, GROUP_WIDTH)), _full((1, XATT_HEAD_DIM)),
            pl.BlockSpec((None, N_MEM, XATT_WIDTH), lambda i: (i // tps, 0, 0)),
            pl.BlockSpec((None, N_MEM, XATT_WIDTH), lambda i: (i // tps, 0, 0)),
            _resident((CONV_WIDTH, d)), _resident((XATT_WIDTH, d)), _full((GROUP_WIDTH, GROUP_WIDTH)),
        ],
        out_specs=grp_specs + [row(d), row(d)],
        out_shape=grp_shapes + [wide, wide],
        scratch_shapes=[pltpu.VMEM((tile + 2 * HALO, d), BF16), pltpu.VMEM((tile + 2 * HALO, CONV_WIDTH), F32),
                        pltpu.VMEM((GROUP_WIDTH // LANES, tile, LANES), F32)],
        compiler_params=_params(("arbitrary",)),
        name="proj",
    )(x2, x2, x2, w["norm1_g"], w["w_in"], w["conv_w"], w["q_gain"], w["k_gain"], w["xq_gain"], kx, vx,
      w["w_br_conv"], w["w_br_xattn"], w["head_avg"])


def _attn_kernel(q_ref, k_ref, v_ref, bias_ref, o_ref, lse_ref, *, seq_sub, lq, qb, kw, n_res):
    i = pl.program_id(2)
    lane_head = lax.broadcasted_iota(I32, (qb, GROUP_WIDTH), 1) // ATT_HEAD_DIM
    head_mask = [lane_head == h for h in range(HEADS_PER_GROUP)]
    head_mask_bf = [jnp.where(m, 1.0, 0.0).astype(BF16) for m in head_mask]

    def block(sb, carry):
        row0 = pl.multiple_of(sb * qb, qb)
        qs = i * lq + row0
        ks = pl.multiple_of(jnp.clip(qs - ATT_RADIUS, 0, seq_sub - kw), ATT_RADIUS)
        case = jnp.where(qs == 0, 0, jnp.where(qs == seq_sub - qb, 2, 1))
        for r in range(n_res):
            cs = slice(r * GROUP_WIDTH, (r + 1) * GROUP_WIDTH)
            q = q_ref[pl.ds(row0, qb), cs]
            kk = k_ref[pl.ds(ks, kw), cs]
            vv = v_ref[pl.ds(ks, kw), cs]
            s = _dot_nt(jnp.concatenate([q * m for m in head_mask_bf], axis=0), kk)
            ps, inv_den, lse = [], [], []
            for h in range(HEADS_PER_GROUP):
                sh = s[h * qb:(h + 1) * qb] + bias_ref[case * HEADS_PER_GROUP + h]
                m = jnp.max(sh, axis=-1, keepdims=True)
                p = jnp.exp(sh - m)
                den = jnp.sum(p, axis=-1, keepdims=True)
                ps.append(p.astype(BF16))
                inv_den.append(1.0 / den)
                lse.append(m + jnp.log(den))
            of = _dot(jnp.concatenate(ps, axis=0), vv)
            o = jnp.zeros((qb, GROUP_WIDTH), F32)
            lb = jnp.zeros((qb, GROUP_WIDTH), F32)
            for h in range(HEADS_PER_GROUP):
                o = jnp.where(head_mask[h], of[h * qb:(h + 1) * qb] * inv_den[h], o)
                lb = jnp.where(head_mask[h], lse[h], lb)
            o_ref[pl.ds(row0, qb), cs] = o.astype(BF16)
            lse_ref[pl.ds(row0, qb), cs] = lb
        return carry

    lax.fori_loop(0, lq // qb, block, 0, unroll=max(1, min(lq // qb, 4 // n_res)))


def _attn_bias(dilation, qb, kw, group):
    slopes = jnp.exp2(-ALIBI_MAX_EXP * jnp.arange(1, N_ATT_HEADS + 1, dtype=F32) / N_ATT_HEADS)
    slopes = slopes[group * HEADS_PER_GROUP:(group + 1) * HEADS_PER_GROUP]
    a = jnp.arange(qb)[:, None]
    c = jnp.arange(kw)[None, :]
    tabs = []
    for off in (0, -ATT_RADIUS, qb - kw):
        delta = off + c - a
        dist = (dilation * jnp.abs(delta)).astype(F32)
        bias = -slopes[:, None, None] * dist[None]
        tabs.append(jnp.where((jnp.abs(delta) <= ATT_RADIUS)[None], bias, NEG_INF))
    return jnp.concatenate(tabs, axis=0)


def _attn(q, k, v, batch, seq_len, group):
    _, dilation = DIL_GROUPS[group]
    seq_sub = seq_len // dilation
    qb = min(ATT_QB, seq_sub)
    kw = min(qb + 2 * ATT_RADIUS, seq_sub)
    lq = min(4 * qb, seq_sub)
    n_res = min(dilation, 4)
    width = n_res * GROUP_WIDTH
    view = lambda t: t.reshape(batch, seq_sub, dilation * GROUP_WIDTH)
    qspec = pl.BlockSpec((None, lq, width), lambda b, r, i: (b, i, r))
    kspec = pl.BlockSpec((None, seq_sub, width), lambda b, r, i: (b, 0, r))
    bias = _attn_bias(dilation, qb, kw, group)
    o, lse = pl.pallas_call(
        functools.partial(_attn_kernel, seq_sub=seq_sub, lq=lq, qb=qb, kw=kw, n_res=n_res),
        grid=(batch, dilation // n_res, seq_sub // lq),
        in_specs=[qspec, kspec, kspec, _full(bias.shape)],
        out_specs=[qspec, qspec],
        out_shape=[jax.ShapeDtypeStruct((batch, seq_sub, dilation * GROUP_WIDTH), BF16),
                   jax.ShapeDtypeStruct((batch, seq_sub, dilation * GROUP_WIDTH), F32)],
        compiler_params=_params(("arbitrary", "arbitrary", "arbitrary")),
        name=f"attn_d{dilation}",
    )(view(q), view(k), view(v), bias)
    rows = batch * seq_sub
    return o.reshape(rows, dilation * GROUP_WIDTH), lse.reshape(rows, dilation * GROUP_WIDTH)


def _merge_kernel(x_ref, p_ref, g1_ref, o0_ref, o1_ref, o2_ref, l0_ref, l1_ref, l2_ref, wba_ref, wo_ref,
                  n2g_ref, wr_ref, x1_ref, h2e_ref, aff_ref, *il_s, tile):
    def token_order(ref, dilation, scratch):
        if dilation == 1:
            return ref[...].astype(F32)
        halves = GROUP_WIDTH // LANES
        for r in range(dilation):
            for half in range(halves):
                col = r * GROUP_WIDTH + half * LANES
                scratch[half, pl.ds(r, tile // dilation, stride=dilation), :] = ref[:, col:col + LANES].astype(F32)
        return jnp.concatenate([scratch[half] for half in range(halves)], axis=1)

    dils = [dil for _, dil in DIL_GROUPS]
    o0, o1, o2 = (token_order(r, dil, s) for r, dil, s in zip((o0_ref, o1_ref, o2_ref), dils, il_s[:3]))
    l0, l1, l2 = (token_order(r, dil, s) for r, dil, s in zip((l0_ref, l1_ref, l2_ref), dils, il_s[3:]))
    m = jnp.maximum(jnp.maximum(l0, l1), l2)
    e0, e1, e2 = jnp.exp(l0 - m), jnp.exp(l1 - m), jnp.exp(l2 - m)
    o = (e0 * o0 + e1 * o1 + e2 * o2) / (e0 + e1 + e2)
    y_attn = _dot(o.astype(BF16), wba_ref[...])
    merged = p_ref[...].astype(F32) + g1_ref[...].astype(F32) * y_attn
    x1 = x_ref[...] + _dot(merged.astype(BF16), wo_ref[...])
    x1_ref[...] = x1
    h2 = (x1 * lax.rsqrt(jnp.mean(x1 * x1, axis=-1, keepdims=True) + EPS) * n2g_ref[...]).astype(BF16)
    h2e_ref[:, :D_MODEL] = h2
    lg = _dot(h2, wr_ref[...])
    lane = lax.broadcasted_iota(I32, lg.shape, 1)
    first = lane < N_EXPERTS
    mx = jnp.max(jnp.where(first, lg, -jnp.inf), axis=-1, keepdims=True)
    ex = jnp.exp(lg - mx)
    a = ex / jnp.sum(jnp.where(first, ex, 0.0), axis=-1, keepdims=True)
    aff_ref[...] = a
    hi = a.astype(BF16).astype(F32)
    mid = (a - hi).astype(BF16).astype(F32)
    lo = (a - hi) - mid
    ext = jnp.where(first, hi, jnp.where(lane < 2 * N_EXPERTS, mid, jnp.where(lane < 3 * N_EXPERTS, lo, 0.0)))
    h2e_ref[:, D_MODEL:] = ext.astype(BF16)


def _merge(x2, p, g1, os_, ls_, w):
    n, d = x2.shape
    tile = PROJ_TILE
    row = lambda width: pl.BlockSpec((tile, width), lambda i: (i, 0))
    grp = [pl.BlockSpec((tile // dil, dil * GROUP_WIDTH), lambda i: (i, 0)) for _, dil in DIL_GROUPS]
    return pl.pallas_call(
        functools.partial(_merge_kernel, tile=tile),
        grid=(n // tile,),
        in_specs=[row(d), row(d), row(d)] + grp + grp
                 + [_full((GROUP_WIDTH, d)), _full((d, d)), _full((1, d)), _full((d, LANES))],
        out_specs=[row(d), row(d + LANES), row(LANES)],
        out_shape=[jax.ShapeDtypeStruct((n, d), F32), jax.ShapeDtypeStruct((n, d + LANES), BF16),
                   jax.ShapeDtypeStruct((n, LANES), F32)],
        scratch_shapes=[pltpu.VMEM((GROUP_WIDTH // LANES, tile, LANES), F32)] * 6,
        compiler_params=_params(("arbitrary",)),
        name="merge",
    )(x2, p, g1, *os_, *ls_, w["w_br_attn"], w["w_o"], w["norm2_g"], w["w_router3"])


def _select_kernel(aff_ref, slot_ref, starts_ref, nchunk_ref, tot_ref, *, n_tok, cap, tile):
    n_tiles = n_tok // tile

    def bisect(b, thr_bits):
        cand = thr_bits | jnp.left_shift(jnp.int32(1), 30 - b)
        cnt = jnp.sum(jnp.where(aff_ref[...] >= lax.bitcast_convert_type(cand, F32), 1.0, 0.0), axis=1, keepdims=True)
        return jnp.where(cnt >= cap, cand, thr_bits)

    thr = lax.bitcast_convert_type(lax.fori_loop(0, 31, bisect, jnp.zeros((N_EXPERTS, 1), I32)), F32)
    need = cap - jnp.sum(jnp.where(aff_ref[...] > thr, 1.0, 0.0), axis=1, keepdims=True)

    r = lax.broadcasted_iota(I32, (tile, tile), 0)
    c = lax.broadcasted_iota(I32, (tile, tile), 1)
    before = jnp.where(r < c, 1.0, 0.0).astype(BF16)

    def tile_body(i, carry):
        start, eq_seen = carry
        off = pl.multiple_of(i * tile, tile)
        a = aff_ref[:, pl.ds(off, tile)]
        eq = jnp.where(a == thr, 1.0, 0.0)
        eq_rank = eq_seen + _dot(eq.astype(BF16), before)
        sel = jnp.where(a > thr, 1.0, jnp.where(eq_rank < need, eq, 0.0))
        rank = _dot(sel.astype(BF16), before)
        slot_ref[:, pl.ds(off, tile)] = jnp.where(sel > 0.0, rank, -1.0).astype(I32)
        cnt = jnp.sum(sel, axis=1, keepdims=True)
        starts_ref[i] = jnp.broadcast_to(start, (N_EXPERTS, LANES)).astype(I32)
        head = start - jnp.floor(start * (1.0 / SEG_ALIGN)) * SEG_ALIGN
        nch = jnp.max(jnp.floor((head + cnt) * (1.0 / ROUTE_CHUNK)) + 1.0, axis=0, keepdims=True)
        nchunk_ref[i] = jnp.broadcast_to(nch, (8, LANES)).astype(I32)
        return start + cnt, eq_seen + jnp.sum(eq, axis=1, keepdims=True)

    zero = jnp.zeros((N_EXPERTS, 1), F32)
    total, _ = lax.fori_loop(0, n_tiles, tile_body, (zero, zero), unroll=4)
    tot_ref[...] = jnp.broadcast_to(total, (N_EXPERTS, LANES)).astype(I32)


def _select(aff_t, cap):
    n_exp, n = aff_t.shape
    tile = ROUTE_TILE
    n_tiles = n // tile
    slot, starts, nchunk, tot = pl.pallas_call(
        functools.partial(_select_kernel, n_tok=n, cap=cap, tile=tile),
        grid=(1,),
        in_specs=[_full((n_exp, n))],
        out_specs=[_full((n_exp, n)), _full((n_tiles, n_exp, LANES)), _full((n_tiles, 8, LANES)), _full((n_exp, LANES))],
        out_shape=[jax.ShapeDtypeStruct((n_exp, n), I32), jax.ShapeDtypeStruct((n_tiles, n_exp, LANES), I32),
                   jax.ShapeDtypeStruct((n_tiles, 8, LANES), I32), jax.ShapeDtypeStruct((n_exp, LANES), I32)],
        compiler_params=_params(("arbitrary",)),
        name="select",
    )(aff_t)
    return slot, starts[:, :, 0], nchunk[:, 0, 0], tot[:, 0]


def _seg_base(starts_ref, step, e):
    return pl.multiple_of((starts_ref[step * N_EXPERTS + e] // SEG_ALIGN) * SEG_ALIGN, SEG_ALIGN)


def _dispatch_kernel(starts_ref, nchunk_ref, slot_ref, h_ref, scol_ref, xe_hbm, stage, head, sem, *, tile, chunk):
    i = pl.program_id(0)
    sl = i % 2

    def copies(step, c, buf):
        return [pltpu.make_async_copy(
            stage.at[buf, pl.ds(e * chunk, chunk), :],
            xe_hbm.at[e, pl.ds(_seg_base(starts_ref, step, e) + c * chunk, chunk), :],
            sem.at[buf]) for e in range(N_EXPERTS)]

    @pl.when(i == 0)
    def _():
        head[...] = jnp.zeros(head.shape, BF16)

    slot = slot_ref[...]
    pos = jnp.where(slot >= 0, slot + (scol_ref[...] & (SEG_ALIGN - 1)), -1)

    def build(c, buf):
        want = lax.broadcasted_iota(I32, (chunk, tile), 0) + c * chunk
        onehot = jnp.concatenate(
            [jnp.where(pos[e:e + 1, :] == want, 1.0, 0.0).astype(BF16) for e in range(N_EXPERTS)], axis=0)
        stage[buf] = _dot(onehot, h_ref[...]).astype(BF16)

    nxt = jnp.minimum(i + 1, pl.num_programs(0) - 1)
    rel = [starts_ref[nxt * N_EXPERTS + e] - _seg_base(starts_ref, i, e) for e in range(N_EXPERTS)]

    def keep_head(e, buf):
        row = pl.multiple_of(e * chunk + ((rel[e] % chunk) // SEG_ALIGN) * SEG_ALIGN, SEG_ALIGN)
        head[e] = stage[buf, pl.ds(row, SEG_ALIGN), :]

    build(0, sl)
    for e in range(N_EXPERTS):
        stage[sl, e * chunk:e * chunk + SEG_ALIGN, :] = stage[sl, e * chunk:e * chunk + SEG_ALIGN, :] + head[e]
    for e in range(N_EXPERTS):
        keep_head(e, sl)

    @pl.when(i > 0)
    def _():
        for cp in copies(i - 1, 0, 1 - sl):
            cp.wait()

    for cp in copies(i, 0, sl):
        cp.start()

    def overflow(c, carry):
        for cp in copies(i, c - 1, sl):
            cp.wait()
        build(c, sl)
        for e in range(N_EXPERTS):
            @pl.when(rel[e] // chunk == c)
            def _():
                keep_head(e, sl)
        for cp in copies(i, c, sl):
            cp.start()
        return carry

    lax.fori_loop(1, nchunk_ref[i], overflow, 0)

    @pl.when(i == pl.num_programs(0) - 1)
    def _():
        for cp in copies(i, 0, sl):
            cp.wait()
        rows = xe_hbm.shape[1]
        zbuf = 1 - sl
        stage[zbuf] = jnp.zeros(stage.shape[1:], BF16)
        end = [_seg_base(starts_ref, i, e) + jnp.maximum(nchunk_ref[i], 1) * chunk for e in range(N_EXPERTS)]
        n_full = [(rows - end[e]) // chunk for e in range(N_EXPERTS)]

        def zero_copy(e, pos):
            return pltpu.make_async_copy(stage.at[zbuf, pl.ds(e * chunk, chunk), :],
                                         xe_hbm.at[e, pl.ds(pl.multiple_of(pos, SEG_ALIGN), chunk), :], sem.at[zbuf])

        def fill(k, carry):
            for e in range(N_EXPERTS):
                @pl.when(k < n_full[e])
                def _():
                    zero_copy(e, end[e] + k * chunk).start()
            for e in range(N_EXPERTS):
                @pl.when(k < n_full[e])
                def _():
                    zero_copy(e, end[e] + k * chunk).wait()
            return carry

        lax.fori_loop(0, functools.reduce(jnp.maximum, n_full), fill, 0)
        for e in range(N_EXPERTS):
            zero_copy(e, rows - chunk).start()
        for e in range(N_EXPERTS):
            zero_copy(e, rows - chunk).wait()


def _dispatch(slot, h2e, starts, nchunk, rows):
    n = h2e.shape[0]
    tile, chunk = ROUTE_TILE, ROUTE_CHUNK
    n_tiles = n // tile
    return pl.pallas_call(
        functools.partial(_dispatch_kernel, tile=tile, chunk=chunk),
        grid_spec=pltpu.PrefetchScalarGridSpec(
            num_scalar_prefetch=2, grid=(n_tiles,),
            in_specs=[pl.BlockSpec((N_EXPERTS, tile), lambda i, *_: (0, i)),
                      pl.BlockSpec((tile, D_MODEL + LANES), lambda i, *_: (i, 0)),
                      pl.BlockSpec((None, N_EXPERTS, 1), lambda i, *_: (i, 0, 0))],
            out_specs=pl.BlockSpec(memory_space=pl.ANY),
            scratch_shapes=[pltpu.VMEM((2, N_EXPERTS * chunk, XE_W), BF16),
                            pltpu.VMEM((N_EXPERTS, SEG_ALIGN, XE_W), BF16), pltpu.SemaphoreType.DMA((2,))]),
        out_shape=jax.ShapeDtypeStruct((N_EXPERTS, rows, XE_W), BF16),
        compiler_params=_params(("arbitrary",)),
        name="dispatch",
    )(starts.reshape(-1), nchunk, slot, h2e, starts.reshape(n_tiles, N_EXPERTS, 1))


def _ffn_kernel(tot_ref, xe_ref, wg_ref, wu_ref, wd_ref, ye_ref, *, tile):
    e = pl.program_id(0)
    j = pl.program_id(1)
    live = tot_ref[e] - j * tile

    @pl.when(live > 0)
    def _():
        x = xe_ref[:, :D_MODEL]
        pieces = xe_ref[:, D_MODEL:].astype(F32)
        lane = lax.broadcasted_iota(I32, pieces.shape, 1)
        mine = ((lane & (N_EXPERTS - 1)) == e) & (lane < 3 * N_EXPERTS)
        gate = jnp.sum(jnp.where(mine, pieces, 0.0), axis=-1, keepdims=True)
        hid = []
        for fc in range(wg_ref.shape[1] // FFN_COLS):
            cs = slice(fc * FFN_COLS, (fc + 1) * FFN_COLS)
            g = _dot(x, wg_ref[:, cs])
            u = _dot(x, wu_ref[:, cs])
            hid.append((g * jax.nn.sigmoid(g) * u).astype(BF16))
        y = _dot(jnp.concatenate(hid, axis=1), wd_ref[...]) * gate
        rows = lax.broadcasted_iota(I32, y.shape, 0)
        y = jnp.where(rows < live, y, 0.0)
        ye_ref[...] = y.astype(BF16)

    @pl.when(live <= 0)
    def _():
        ye_ref[...] = jnp.zeros(ye_ref.shape, BF16)


def _ffn(xe, tot, w_gate, w_up, w_down):
    n_exp, rows, _ = xe.shape
    tile = FFN_TILE
    d, f = w_gate.shape[1:]
    last = lambda e, j, tot: jnp.minimum(j, jnp.maximum(tot[e] - 1, 0) // tile)
    wspec = lambda a, b: pl.BlockSpec((None, a, b), lambda e, j, tot: (e, 0, 0))
    return pl.pallas_call(
        functools.partial(_ffn_kernel, tile=tile),
        grid_spec=pltpu.PrefetchScalarGridSpec(
            num_scalar_prefetch=1, grid=(n_exp, rows // tile),
            in_specs=[pl.BlockSpec((None, tile, XE_W), lambda e, j, tot: (e, last(e, j, tot), 0)),
                      wspec(d, f), wspec(d, f), wspec(f, d)],
            out_specs=pl.BlockSpec((None, tile, d), lambda e, j, tot: (e, j, 0))),
        out_shape=jax.ShapeDtypeStruct((n_exp, rows, d), BF16),
        compiler_params=_params(("arbitrary", "arbitrary")),
        name="ffn",
    )(tot, xe, w_gate, w_up, w_down)


def _combine_kernel(starts_ref, nchunk_ref, x1_ref, slot_ref, srow_ref, ye_hbm, out_ref, stage, sem, *, tile, chunk):
    i = pl.program_id(0)
    sl = i % 2
    over = 2

    def copies(step, c, buf):
        return [pltpu.make_async_copy(
            ye_hbm.at[e, pl.ds(_seg_base(starts_ref, step, e) + c * chunk, chunk), :],
            stage.at[buf, pl.ds(e * chunk, chunk), :],
            sem.at[buf]) for e in range(N_EXPERTS)]

    @pl.when(i == 0)
    def _():
        for cp in copies(0, 0, 0):
            cp.start()

    @pl.when(i + 1 < pl.num_programs(0))
    def _():
        for cp in copies(i + 1, 0, 1 - sl):
            cp.start()

    col = lax.broadcasted_iota(I32, (N_EXPERTS, N_EXPERTS * chunk), 1)
    exp_row = lax.broadcasted_iota(I32, (N_EXPERTS, N_EXPERTS * chunk), 0)
    spread = jnp.where(col // chunk == exp_row, 1.0, 0.0).astype(BF16)
    slot = slot_ref[...]
    rank = jnp.where(slot < 0, -float(tile), slot.astype(F32)).astype(BF16)
    head = jnp.broadcast_to((srow_ref[...] & (SEG_ALIGN - 1)).astype(F32).astype(BF16), (8, N_EXPERTS))
    slots = _dot(rank, spread) + _dot(head, spread)[0:1]
    lane_slot = (lax.broadcasted_iota(I32, slots.shape, 1) & (chunk - 1)).astype(F32)

    def gathered(c, buf):
        onehot = jnp.where(slots == lane_slot + c * chunk, 1.0, 0.0).astype(BF16)
        return _dot(onehot, stage[buf])

    for cp in copies(i, 0, sl):
        cp.wait()
    out_ref[...] = x1_ref[...] + gathered(0, sl)

    def overflow(c, carry):
        for cp in copies(i, c, over):
            cp.start()
        for cp in copies(i, c, over):
            cp.wait()
        out_ref[...] += gathered(c, over)
        return carry

    lax.fori_loop(1, nchunk_ref[i], overflow, 0)


def _combine(x1, slot_t, ye, starts, nchunk):
    n, d = x1.shape
    tile, chunk = ROUTE_TILE, ROUTE_CHUNK
    n_tiles = n // tile
    return pl.pallas_call(
        functools.partial(_combine_kernel, tile=tile, chunk=chunk),
        grid_spec=pltpu.PrefetchScalarGridSpec(
            num_scalar_prefetch=2, grid=(n_tiles,),
            in_specs=[pl.BlockSpec((tile, d), lambda i, *_: (i, 0)),
                      pl.BlockSpec((tile, N_EXPERTS), lambda i, *_: (i, 0)),
                      pl.BlockSpec((None, 1, N_EXPERTS), lambda i, *_: (i, 0, 0)),
                      pl.BlockSpec(memory_space=pl.ANY)],
            out_specs=pl.BlockSpec((tile, d), lambda i, *_: (i, 0)),
            scratch_shapes=[pltpu.VMEM((3, N_EXPERTS * chunk, d), BF16), pltpu.SemaphoreType.DMA((3,))]),
        out_shape=jax.ShapeDtypeStruct((n, d), F32),
        compiler_params=_params(("arbitrary",)),
        name="combine",
    )(starts.reshape(-1), nchunk, x1, slot_t, starts.reshape(n_tiles, 1, N_EXPERTS), ye)


def _expert_rows(cap):
    need = cap + ROUTE_TILE + 2 * ROUTE_CHUNK
    return -(-need // FFN_TILE) * FFN_TILE


def _routed_ffn(x1, h2e, aff, w):
    n = x1.shape[0]
    cap = max(1, EC_CAPACITY * n // N_EXPERTS)
    slot, starts, nchunk, tot = _select(aff[:, :N_EXPERTS].T, cap)
    xe = _dispatch(slot, h2e, starts, nchunk, _expert_rows(cap))
    ye = _ffn(xe, tot, w["w_exp_gate"], w["w_exp_up"], w["w_exp_down"])
    return _combine(x1, slot.T, ye, starts, nchunk)


def _layer(x, mem, w):
    b, s, d = x.shape
    x2 = x.reshape(b * s, d)
    kx, vx = _memkv(mem, w["mem_norm_g"], w["w_mem_kv"], w["xk_gain"])
    *qkv, p, g1 = _proj(x2, s, kx, vx, w)
    os_, ls_ = [], []
    for g in range(len(DIL_GROUPS)):
        o, lse = _attn(qkv[g], qkv[3 + g], qkv[6 + g], b, s, g)
        os_.append(o)
        ls_.append(lse)
    x1, h2e, aff = _merge(x2, p, g1, os_, ls_, w)
    return _routed_ffn(x1, h2e, aff, w).reshape(b, s, d)


def _prepare(norm1_g, w_in, conv_w, q_norm_g, k_norm_g, mem_norm_g, w_mem_kv, xq_norm_g, xk_norm_g,
             w_br_conv, w_br_attn, w_br_xattn, w_o, norm2_g, w_router, w_exp_gate, w_exp_up, w_exp_down):
    row = lambda v: v.reshape(1, -1).astype(F32)
    head = jnp.arange(GROUP_WIDTH) // ATT_HEAD_DIM
    return {
        "norm1_g": row(norm1_g), "norm2_g": row(norm2_g), "mem_norm_g": row(mem_norm_g),
        "w_in": w_in.astype(BF16), "conv_w": conv_w.astype(F32),
        "q_gain": row(jnp.tile(q_norm_g, HEADS_PER_GROUP) * (ATT_HEAD_DIM ** -0.5)),
        "k_gain": row(jnp.tile(k_norm_g, HEADS_PER_GROUP)),
        "xq_gain": row(xq_norm_g), "xk_gain": row(xk_norm_g),
        "w_mem_kv": w_mem_kv.astype(BF16), "w_br_conv": w_br_conv.astype(BF16),
        "w_br_attn": w_br_attn.astype(BF16), "w_br_xattn": w_br_xattn.astype(BF16), "w_o": w_o.astype(BF16),
        "head_avg": jnp.where(head[:, None] == head[None, :], 1.0 / ATT_HEAD_DIM, 0.0).astype(BF16),
        "w_router3": jnp.concatenate([w_router] * 3 + [jnp.zeros((D_MODEL, LANES - 3 * N_EXPERTS), F32)],
                                     axis=1).astype(BF16),
        "w_exp_gate": w_exp_gate.astype(BF16), "w_exp_up": w_exp_up.astype(BF16), "w_exp_down": w_exp_down.astype(BF16),
    }


def kernel(x_prompt, x_sample, mem_prompt, mem_sample, norm1_g, w_in, conv_w, q_norm_g, k_norm_g, mem_norm_g, w_mem_kv, xq_norm_g, xk_norm_g, w_br_conv, w_br_attn, w_br_xattn, w_o, norm2_g, w_router, w_exp_gate, w_exp_up, w_exp_down):
    per_layer = (norm1_g, w_in, conv_w, q_norm_g, k_norm_g, mem_norm_g, w_mem_kv, xq_norm_g, xk_norm_g,
                 w_br_conv, w_br_attn, w_br_xattn, w_o, norm2_g, w_router, w_exp_gate, w_exp_up, w_exp_down)
    y_prompt, y_sample = x_prompt, x_sample
    for layer in range(norm1_g.shape[0]):
        w = _prepare(*(t[layer] for t in per_layer))
        y_prompt = _layer(y_prompt, mem_prompt, w)
        y_sample = _layer(y_sample, mem_sample, w)
    return (y_prompt, y_sample)
```

```python
import functools

import jax
import jax.numpy as jnp
from jax import lax
from jax.experimental import pallas as pl
from jax.experimental.pallas import tpu as pltpu

F32 = jnp.float32
BF16 = jnp.bfloat16
I32 = jnp.int32

D_MODEL = 1024
N_MEM = 256
CONV_WIDTH = 768
ATT_HEAD_DIM = 64
DIL_GROUPS = ((128, 1), (512, 4), (2048, 16))
HEADS_PER_GROUP = 4
N_ATT_HEADS = HEADS_PER_GROUP * len(DIL_GROUPS)
ATT_WIDTH = N_ATT_HEADS * ATT_HEAD_DIM
GROUP_WIDTH = HEADS_PER_GROUP * ATT_HEAD_DIM
ATT_RADIUS = 64
XATT_HEADS = 4
XATT_HEAD_DIM = 128
XATT_WIDTH = XATT_HEADS * XATT_HEAD_DIM
N_EXPERTS = 16
EC_CAPACITY = 2
ALIBI_MAX_EXP = 8.0
EPS = 1e-6
NEG_INF = -1e30

C_CB, C_CC, C_AQ, C_XQ, C_GATE, C_END = 0, 768, 2304, 4608, 5120, 8192

V7X_VMEM_LIMIT_BYTES = 56 * 1024 * 1024
LANES = 128
HALO = 16

PROJ_TILE = 512
MERGE_SPLIT = 2
ATT_QB = 128
ROUTE_TILE = 256
ROUTE_CHUNK = 64
SEG_ALIGN = 16
FFN_TILE = 512
FFN_COLS = 256
XE_W = D_MODEL + LANES


def _dot(a, b):
    return jnp.dot(a, b, preferred_element_type=F32)


def _dot_nt(a, b):
    return lax.dot_general(a, b, (((1,), (1,)), ((), ())), preferred_element_type=F32)


def _params(sem):
    return pltpu.CompilerParams(dimension_semantics=sem, vmem_limit_bytes=V7X_VMEM_LIMIT_BYTES)


def _full(shape):
    return pl.BlockSpec(shape, lambda *_: (0,) * len(shape))


def _resident(shape):
    return pl.BlockSpec(shape, lambda *_: (0,) * len(shape), pipeline_mode=pl.Buffered(1))


def _memkv_kernel(mem_ref, g_ref, w_ref, kg_ref, k_ref, v_ref):
    m = mem_ref[...]
    hn = (m * lax.rsqrt(jnp.mean(m * m, axis=-1, keepdims=True) + EPS) * g_ref[...]).astype(BF16)
    kv = _dot(hn, w_ref[...])
    ks = []
    for h in range(XATT_HEADS):
        kh = kv[:, h * XATT_HEAD_DIM:(h + 1) * XATT_HEAD_DIM]
        ks.append(kh * lax.rsqrt(jnp.mean(kh * kh, axis=-1, keepdims=True) + EPS) * kg_ref[...])
    k_ref[...] = jnp.concatenate(ks, axis=1).astype(BF16)
    v_ref[...] = kv[:, XATT_WIDTH:].astype(BF16)


def _memkv(mem, mem_g, w_mem_kv, xk_g):
    b, m, d = mem.shape
    out = jax.ShapeDtypeStruct((b, m, XATT_WIDTH), BF16)
    return pl.pallas_call(
        _memkv_kernel,
        grid=(b,),
        in_specs=[pl.BlockSpec((None, m, d), lambda i: (i, 0, 0)), _full((1, d)),
                  _full((d, 2 * XATT_WIDTH)), _full((1, XATT_HEAD_DIM))],
        out_specs=[pl.BlockSpec((None, m, XATT_WIDTH), lambda i: (i, 0, 0))] * 2,
        out_shape=[out, out],
        compiler_params=_params(("arbitrary",)),
        name="memkv",
    )(mem, mem_g, w_mem_kv, xk_g)


def _proj_kernel(xp_ref, x_ref, xn_ref, n1g_ref, win_ref, cw_ref, qg_ref, kg_ref, xqg_ref, kx_ref, vx_ref,
                 wbc_ref, wbx_ref, bd_ref,
                 q0_ref, q1_ref, q2_ref, k0_ref, k1_ref, k2_ref, v0_ref, v1_ref, v2_ref, p_ref, g1_ref,
                 hb_s, u_s, il_s, *, tile, tiles_per_seq):
    tin = pl.program_id(0) % tiles_per_seq
    gain = n1g_ref[...]

    def nrm(x):
        return (x * lax.rsqrt(jnp.mean(x * x, axis=-1, keepdims=True) + EPS) * gain).astype(BF16)

    hb_s[0:HALO, :] = nrm(xp_ref[...])
    hb_s[HALO:HALO + tile, :] = nrm(x_ref[...])
    hb_s[HALO + tile:, :] = nrm(xn_ref[...])
    hc = hb_s[HALO:HALO + tile, :]

    ccx = _dot(hb_s[...], win_ref[:, C_CC:C_AQ])
    u = ccx[:, :CONV_WIDTH] * ccx[:, CONV_WIDTH:]
    u_s[0:HALO, :] = u[0:HALO] * jnp.where(tin == 0, 0.0, 1.0)
    u_s[HALO:HALO + tile, :] = u[HALO:HALO + tile]
    u_s[HALO + tile:, :] = u[HALO + tile:] * jnp.where(tin == tiles_per_seq - 1, 0.0, 1.0)
    cw = cw_ref[...]
    conv = (cw[0:1] * u_s[HALO - 1:HALO - 1 + tile, :] + cw[1:2] * u_s[HALO:HALO + tile, :]
            + cw[2:3] * u_s[HALO + 1:HALO + 1 + tile, :])
    cb = _dot(hc, win_ref[:, C_CB:C_CC])
    y_conv = _dot((cb * conv).astype(BF16), wbc_ref[...])

    qkv = _dot(hc, win_ref[:, C_AQ:C_XQ])
    bd = bd_ref[...]

    def head_norm(z, g_ref):
        outs = []
        for c in range(len(DIL_GROUPS)):
            zc = z[:, c * GROUP_WIDTH:(c + 1) * GROUP_WIDTH]
            ms = _dot((zc * zc).astype(BF16), bd)
            outs.append(zc * lax.rsqrt(ms + EPS) * g_ref[...])
        return outs

    def emit(ref, val, dilation):
        if dilation == 1:
            ref[...] = val.astype(BF16)
            return
        for half in range(GROUP_WIDTH // LANES):
            il_s[half] = val[:, half * LANES:(half + 1) * LANES]
        for r in range(dilation):
            for half in range(GROUP_WIDTH // LANES):
                col = r * GROUP_WIDTH + half * LANES
                ref[:, col:col + LANES] = il_s[half, pl.ds(r, tile // dilation, stride=dilation), :].astype(BF16)

    vals = (head_norm(qkv[:, :ATT_WIDTH], qg_ref) + head_norm(qkv[:, ATT_WIDTH:2 * ATT_WIDTH], kg_ref)
            + [qkv[:, 2 * ATT_WIDTH + c * GROUP_WIDTH:2 * ATT_WIDTH + (c + 1) * GROUP_WIDTH] for c in range(len(DIL_GROUPS))])
    refs = (q0_ref, q1_ref, q2_ref, k0_ref, k1_ref, k2_ref, v0_ref, v1_ref, v2_ref)
    for n, (ref, val) in enumerate(zip(refs, vals)):
        emit(ref, val, DIL_GROUPS[n % len(DIL_GROUPS)][1])

    xq = _dot(hc, win_ref[:, C_XQ:C_GATE])
    kx = kx_ref[...]
    vx = vx_ref[...]
    outs = []
    for h in range(XATT_HEADS):
        hs = slice(h * XATT_HEAD_DIM, (h + 1) * XATT_HEAD_DIM)
        qh = xq[:, hs]
        qh = qh * lax.rsqrt(jnp.mean(qh * qh, axis=-1, keepdims=True) + EPS) * xqg_ref[...]
        s = _dot_nt(qh.astype(BF16), kx[:, hs]) * (XATT_HEAD_DIM ** -0.5)
        p = jnp.exp(s - jnp.max(s, axis=-1, keepdims=True))
        den = jnp.sum(p, axis=-1, keepdims=True)
        outs.append(_dot(p.astype(BF16), vx[:, hs]) / den)
    y_x = _dot(jnp.concatenate(outs, axis=1).astype(BF16), wbx_ref[...])

    gs = jax.nn.sigmoid(_dot(hc, win_ref[:, C_GATE:C_END]))
    p_ref[...] = (gs[:, :D_MODEL] * y_conv + gs[:, 2 * D_MODEL:] * y_x).astype(BF16)
    g1_ref[...] = gs[:, D_MODEL:2 * D_MODEL].astype(BF16)


def _proj(x2, seq_len, kx, vx, w):
    n, d = x2.shape
    tile = PROJ_TILE
    tps = seq_len // tile
    hb = tile // HALO
    n_hblk = n // HALO
    wide = jax.ShapeDtypeStruct((n, d), BF16)
    row = lambda width: pl.BlockSpec((tile, width), lambda i: (i, 0))
    dils = [dil for _, dil in DIL_GROUPS] * 3
    grp_shapes = [jax.ShapeDtypeStruct((n // dil, dil * GROUP_WIDTH), BF16) for dil in dils]
    grp_specs = [pl.BlockSpec((tile // dil, dil * GROUP_WIDTH), lambda i: (i, 0)) for dil in dils]
    return pl.pallas_call(
        functools.partial(_proj_kernel, tile=tile, tiles_per_seq=tps),
        grid=(n // tile,),
        in_specs=[
            pl.BlockSpec((HALO, d), lambda i: (jnp.maximum(i * hb - 1, 0), 0)),
            row(d),
            pl.BlockSpec((HALO, d), lambda i: (jnp.minimum((i + 1) * hb, n_hblk - 1), 0)),
            _full((1, d)), _resident((d, C_END)), _full((3, CONV_WIDTH)),
            _full((1, GROUP_WIDTH)), _full((1, GROUP_WIDTH)), _full((1, XATT_HEAD_DIM)),
            pl.BlockSpec((None, N_MEM, XATT_WIDTH), lambda i: (i // tps, 0, 0)),
            pl.BlockSpec((None, N_MEM, XATT_WIDTH), lambda i: (i // tps, 0, 0)),
            _resident((CONV_WIDTH, d)), _resident((XATT_WIDTH, d)), _full((GROUP_WIDTH, GROUP_WIDTH)),
        ],
        out_specs=grp_specs + [row(d), row(d)],
        out_shape=grp_shapes + [wide, wide],
        scratch_shapes=[pltpu.VMEM((tile + 2 * HALO, d), BF16), pltpu.VMEM((tile + 2 * HALO, CONV_WIDTH), F32),
                        pltpu.VMEM((GROUP_WIDTH // LANES, tile, LANES), F32)],
        compiler_params=_params(("arbitrary",)),
        name="proj",
    )(x2, x2, x2, w["norm1_g"], w["w_in"], w["conv_w"], w["q_gain"], w["k_gain"], w["xq_gain"], kx, vx,
      w["w_br_conv"], w["w_br_xattn"], w["head_avg"])


def _attn_kernel(q_ref, k_ref, v_ref, bias_ref, o_ref, lse_ref, *, seq_sub, lq, qb, kw, n_res):
    i = pl.program_id(2)
    lane_head = lax.broadcasted_iota(I32, (qb, GROUP_WIDTH), 1) // ATT_HEAD_DIM
    head_mask = [lane_head == h for h in range(HEADS_PER_GROUP)]
    head_mask_bf = [jnp.where(m, 1.0, 0.0).astype(BF16) for m in head_mask]

    def block(sb, carry):
        row0 = pl.multiple_of(sb * qb, qb)
        qs = i * lq + row0
        ks = pl.multiple_of(jnp.clip(qs - ATT_RADIUS, 0, seq_sub - kw), ATT_RADIUS)
        case = jnp.where(qs == 0, 0, jnp.where(qs == seq_sub - qb, 2, 1))
        for r in range(n_res):
            cs = slice(r * GROUP_WIDTH, (r + 1) * GROUP_WIDTH)
            q = q_ref[pl.ds(row0, qb), cs]
            kk = k_ref[pl.ds(ks, kw), cs]
            vv = v_ref[pl.ds(ks, kw), cs]
            s = _dot_nt(jnp.concatenate([q * m for m in head_mask_bf], axis=0), kk)
            ps, inv_den, lse = [], [], []
            for h in range(HEADS_PER_GROUP):
                sh = s[h * qb:(h + 1) * qb] + bias_ref[case * HEADS_PER_GROUP + h]
                m = jnp.max(sh, axis=-1, keepdims=True)
                p = jnp.exp(sh - m)
                den = jnp.sum(p, axis=-1, keepdims=True)
                ps.append(p.astype(BF16))
                inv_den.append(1.0 / den)
                lse.append(m + jnp.log(den))
            of = _dot(jnp.concatenate(ps, axis=0), vv)
            o = jnp.zeros((qb, GROUP_WIDTH), F32)
            lb = jnp.zeros((qb, GROUP_WIDTH), F32)
            for h in range(HEADS_PER_GROUP):
                o = jnp.where(head_mask[h], of[h * qb:(h + 1) * qb] * inv_den[h], o)
                lb = jnp.where(head_mask[h], lse[h], lb)
            o_ref[pl.ds(row0, qb), cs] = o.astype(BF16)
            lse_ref[pl.ds(row0, qb), cs] = lb
        return carry

    lax.fori_loop(0, lq // qb, block, 0, unroll=max(1, min(lq // qb, 4 // n_res)))


def _attn_bias(dilation, qb, kw, group):
    slopes = jnp.exp2(-ALIBI_MAX_EXP * jnp.arange(1, N_ATT_HEADS + 1, dtype=F32) / N_ATT_HEADS)
    slopes = slopes[group * HEADS_PER_GROUP:(group + 1) * HEADS_PER_GROUP]
    a = jnp.arange(qb)[:, None]
    c = jnp.arange(kw)[None, :]
    tabs = []
    for off in (0, -ATT_RADIUS, qb - kw):
        delta = off + c - a
        dist = (dilation * jnp.abs(delta)).astype(F32)
        bias = -slopes[:, None, None] * dist[None]
        tabs.append(jnp.where((jnp.abs(delta) <= ATT_RADIUS)[None], bias, NEG_INF))
    return jnp.concatenate(tabs, axis=0)


def _attn(q, k, v, batch, seq_len, group):
    _, dilation = DIL_GROUPS[group]
    seq_sub = seq_len // dilation
    qb = min(ATT_QB, seq_sub)
    kw = min(qb + 2 * ATT_RADIUS, seq_sub)
    lq = min(4 * qb, seq_sub)
    n_res = min(dilation, 4)
    width = n_res * GROUP_WIDTH
    view = lambda t: t.reshape(batch, seq_sub, dilation * GROUP_WIDTH)
    qspec = pl.BlockSpec((None, lq, width), lambda b, r, i: (b, i, r))
    kspec = pl.BlockSpec((None, seq_sub, width), lambda b, r, i: (b, 0, r))
    bias = _attn_bias(dilation, qb, kw, group)
    o, lse = pl.pallas_call(
        functools.partial(_attn_kernel, seq_sub=seq_sub, lq=lq, qb=qb, kw=kw, n_res=n_res),
        grid=(batch, dilation // n_res, seq_sub // lq),
        in_specs=[qspec, kspec, kspec, _full(bias.shape)],
        out_specs=[qspec, qspec],
        out_shape=[jax.ShapeDtypeStruct((batch, seq_sub, dilation * GROUP_WIDTH), BF16),
                   jax.ShapeDtypeStruct((batch, seq_sub, dilation * GROUP_WIDTH), F32)],
        compiler_params=_params(("arbitrary", "arbitrary", "arbitrary")),
        name=f"attn_d{dilation}",
    )(view(q), view(k), view(v), bias)
    rows = batch * seq_sub
    return o.reshape(rows, dilation * GROUP_WIDTH), lse.reshape(rows, dilation * GROUP_WIDTH)


def _merge_kernel(x_ref, p_ref, g1_ref, o0_ref, o1_ref, o2_ref, l0_ref, l1_ref, l2_ref, wba_ref, wo_ref,
                  n2g_ref, wr_ref, x1_ref, h2e_ref, afft_ref, *il_s, tile):
    halves = GROUP_WIDTH // LANES
    dils = [dil for _, dil in DIL_GROUPS] * 2
    grp_refs = (o0_ref, o1_ref, o2_ref, l0_ref, l1_ref, l2_ref)

    for ref, dilation, scratch in zip(grp_refs, dils, il_s):
        for r in range(dilation if dilation > 1 else 0):
            for half in range(halves):
                col = r * GROUP_WIDTH + half * LANES
                scratch[half, pl.ds(r, tile // dilation, stride=dilation), :] = ref[:, col:col + LANES].astype(F32)

    def token_order(n, rs):
        if dils[n] == 1:
            return grp_refs[n][rs, :].astype(F32)
        return jnp.concatenate([il_s[n][half, rs, :] for half in range(halves)], axis=1)

    rows = tile // MERGE_SPLIT
    for blk in range(MERGE_SPLIT):
        rs = slice(blk * rows, (blk + 1) * rows)
        o0, o1, o2, l0, l1, l2 = (token_order(n, rs) for n in range(6))
        m = jnp.maximum(jnp.maximum(l0, l1), l2)
        e0, e1, e2 = jnp.exp(l0 - m), jnp.exp(l1 - m), jnp.exp(l2 - m)
        o = (e0 * o0 + e1 * o1 + e2 * o2) / (e0 + e1 + e2)
        y_attn = _dot(o.astype(BF16), wba_ref[...])
        merged = p_ref[rs, :].astype(F32) + g1_ref[rs, :].astype(F32) * y_attn
        x1 = x_ref[rs, :] + _dot(merged.astype(BF16), wo_ref[...])
        x1_ref[rs, :] = x1
        h2 = (x1 * lax.rsqrt(jnp.mean(x1 * x1, axis=-1, keepdims=True) + EPS) * n2g_ref[...]).astype(BF16)
        h2e_ref[rs, :D_MODEL] = h2
        lg = _dot(h2, wr_ref[...])
        lane = lax.broadcasted_iota(I32, lg.shape, 1)
        first = lane < N_EXPERTS
        mx = jnp.max(jnp.where(first, lg, -jnp.inf), axis=-1, keepdims=True)
        ex = jnp.exp(lg - mx)
        a = ex / jnp.sum(jnp.where(first, ex, 0.0), axis=-1, keepdims=True)
        afft_ref[:, rs] = a.T[:N_EXPERTS, :]
        hi = a.astype(BF16).astype(F32)
        mid = (a - hi).astype(BF16).astype(F32)
        lo = (a - hi) - mid
        ext = jnp.where(first, hi, jnp.where(lane < 2 * N_EXPERTS, mid, jnp.where(lane < 3 * N_EXPERTS, lo, 0.0)))
        h2e_ref[rs, D_MODEL:] = ext.astype(BF16)


def _merge(x2, p, g1, os_, ls_, w):
    n, d = x2.shape
    tile = PROJ_TILE
    row = lambda width: pl.BlockSpec((tile, width), lambda i: (i, 0))
    grp = [pl.BlockSpec((tile // dil, dil * GROUP_WIDTH), lambda i: (i, 0)) for _, dil in DIL_GROUPS]
    return pl.pallas_call(
        functools.partial(_merge_kernel, tile=tile),
        grid=(n // tile,),
        in_specs=[row(d), row(d), row(d)] + grp + grp
                 + [_full((GROUP_WIDTH, d)), _full((d, d)), _full((1, d)), _full((d, LANES))],
        out_specs=[row(d), row(d + LANES), pl.BlockSpec((N_EXPERTS, tile), lambda i: (0, i))],
        out_shape=[jax.ShapeDtypeStruct((n, d), F32), jax.ShapeDtypeStruct((n, d + LANES), BF16),
                   jax.ShapeDtypeStruct((N_EXPERTS, n), F32)],
        scratch_shapes=[pltpu.VMEM((GROUP_WIDTH // LANES, tile, LANES), F32)] * 6,
        compiler_params=_params(("arbitrary",)),
        name="merge",
    )(x2, p, g1, *os_, *ls_, w["w_br_attn"], w["w_o"], w["norm2_g"], w["w_router3"])


def _select_kernel(aff_ref, slot_ref, starts_ref, nchunk_ref, tot_ref, *, n_tok, cap, tile):
    n_tiles = n_tok // tile

    def bisect(b, thr_bits):
        cand = thr_bits | jnp.left_shift(jnp.int32(1), 30 - b)
        cnt = jnp.sum(jnp.where(aff_ref[...] >= lax.bitcast_convert_type(cand, F32), 1.0, 0.0), axis=1, keepdims=True)
        return jnp.where(cnt >= cap, cand, thr_bits)

    thr = lax.bitcast_convert_type(lax.fori_loop(0, 31, bisect, jnp.zeros((N_EXPERTS, 1), I32)), F32)
    need = cap - jnp.sum(jnp.where(aff_ref[...] > thr, 1.0, 0.0), axis=1, keepdims=True)

    r = lax.broadcasted_iota(I32, (tile, tile), 0)
    c = lax.broadcasted_iota(I32, (tile, tile), 1)
    before = jnp.where(r < c, 1.0, 0.0).astype(BF16)

    def tile_body(i, carry):
        start, eq_seen = carry
        off = pl.multiple_of(i * tile, tile)
        a = aff_ref[:, pl.ds(off, tile)]
        eq = jnp.where(a == thr, 1.0, 0.0)
        eq_rank = eq_seen + _dot(eq.astype(BF16), before)
        sel = jnp.where(a > thr, 1.0, jnp.where(eq_rank < need, eq, 0.0))
        rank = _dot(sel.astype(BF16), before)
        slot_ref[:, pl.ds(off, tile)] = jnp.where(sel > 0.0, rank, -1.0).astype(I32)
        cnt = jnp.sum(sel, axis=1, keepdims=True)
        starts_ref[i] = jnp.broadcast_to(start, (N_EXPERTS, LANES)).astype(I32)
        head = start - jnp.floor(start * (1.0 / SEG_ALIGN)) * SEG_ALIGN
        nch = jnp.max(jnp.floor((head + cnt) * (1.0 / ROUTE_CHUNK)) + 1.0, axis=0, keepdims=True)
        nchunk_ref[i] = jnp.broadcast_to(nch, (8, LANES)).astype(I32)
        return start + cnt, eq_seen + jnp.sum(eq, axis=1, keepdims=True)

    zero = jnp.zeros((N_EXPERTS, 1), F32)
    total, _ = lax.fori_loop(0, n_tiles, tile_body, (zero, zero), unroll=4)
    tot_ref[...] = jnp.broadcast_to(total, (N_EXPERTS, LANES)).astype(I32)


def _select(aff_t, cap):
    n_exp, n = aff_t.shape
    tile = ROUTE_TILE
    n_tiles = n // tile
    slot, starts, nchunk, tot = pl.pallas_call(
        functools.partial(_select_kernel, n_tok=n, cap=cap, tile=tile),
        grid=(1,),
        in_specs=[_full((n_exp, n))],
        out_specs=[_full((n_exp, n)), _full((n_tiles, n_exp, LANES)), _full((n_tiles, 8, LANES)), _full((n_exp, LANES))],
        out_shape=[jax.ShapeDtypeStruct((n_exp, n), I32), jax.ShapeDtypeStruct((n_tiles, n_exp, LANES), I32),
                   jax.ShapeDtypeStruct((n_tiles, 8, LANES), I32), jax.ShapeDtypeStruct((n_exp, LANES), I32)],
        compiler_params=_params(("arbitrary",)),
        name="select",
    )(aff_t)
    return slot, starts[:, :, 0], nchunk[:, 0, 0], tot[:, 0]


def _seg_base(starts_ref, step, e):
    return pl.multiple_of((starts_ref[step * N_EXPERTS + e] // SEG_ALIGN) * SEG_ALIGN, SEG_ALIGN)


def _dispatch_kernel(starts_ref, nchunk_ref, slot_ref, h_ref, scol_ref, xe_hbm, stage, head, sem, *, tile, chunk):
    i = pl.program_id(0)
    sl = i % 2

    def copies(step, c, buf):
        return [pltpu.make_async_copy(
            stage.at[buf, pl.ds(e * chunk, chunk), :],
            xe_hbm.at[e, pl.ds(_seg_base(starts_ref, step, e) + c * chunk, chunk), :],
            sem.at[buf]) for e in range(N_EXPERTS)]

    @pl.when(i == 0)
    def _():
        head[...] = jnp.zeros(head.shape, BF16)

    slot = slot_ref[...]
    pos = jnp.where(slot >= 0, slot + (scol_ref[...] & (SEG_ALIGN - 1)), -1)

    def build(c, buf):
        want = lax.broadcasted_iota(I32, (chunk, tile), 0) + c * chunk
        onehot = jnp.concatenate(
            [jnp.where(pos[e:e + 1, :] == want, 1.0, 0.0).astype(BF16) for e in range(N_EXPERTS)], axis=0)
        stage[buf] = _dot(onehot, h_ref[...]).astype(BF16)

    nxt = jnp.minimum(i + 1, pl.num_programs(0) - 1)
    rel = [starts_ref[nxt * N_EXPERTS + e] - _seg_base(starts_ref, i, e) for e in range(N_EXPERTS)]

    def keep_head(e, buf):
        row = pl.multiple_of(e * chunk + ((rel[e] % chunk) // SEG_ALIGN) * SEG_ALIGN, SEG_ALIGN)
        head[e] = stage[buf, pl.ds(row, SEG_ALIGN), :]

    build(0, sl)
    for e in range(N_EXPERTS):
        stage[sl, e * chunk:e * chunk + SEG_ALIGN, :] = stage[sl, e * chunk:e * chunk + SEG_ALIGN, :] + head[e]
    for e in range(N_EXPERTS):
        keep_head(e, sl)

    @pl.when(i > 0)
    def _():
        for cp in copies(i - 1, 0, 1 - sl):
            cp.wait()

    for cp in copies(i, 0, sl):
        cp.start()

    def overflow(c, carry):
        for cp in copies(i, c - 1, sl):
            cp.wait()
        build(c, sl)
        for e in range(N_EXPERTS):
            @pl.when(rel[e] // chunk == c)
            def _():
                keep_head(e, sl)
        for cp in copies(i, c, sl):
            cp.start()
        return carry

    lax.fori_loop(1, nchunk_ref[i], overflow, 0)

    @pl.when(i == pl.num_programs(0) - 1)
    def _():
        for cp in copies(i, 0, sl):
            cp.wait()
        rows = xe_hbm.shape[1]
        zbuf = 1 - sl
        stage[zbuf] = jnp.zeros(stage.shape[1:], BF16)
        end = [_seg_base(starts_ref, i, e) + jnp.maximum(nchunk_ref[i], 1) * chunk for e in range(N_EXPERTS)]
        n_full = [(rows - end[e]) // chunk for e in range(N_EXPERTS)]

        def zero_copy(e, pos):
            return pltpu.make_async_copy(stage.at[zbuf, pl.ds(e * chunk, chunk), :],
                                         xe_hbm.at[e, pl.ds(pl.multiple_of(pos, SEG_ALIGN), chunk), :], sem.at[zbuf])

        def fill(k, carry):
            for e in range(N_EXPERTS):
                @pl.when(k < n_full[e])
                def _():
                    zero_copy(e, end[e] + k * chunk).start()
            for e in range(N_EXPERTS):
                @pl.when(k < n_full[e])
                def _():
                    zero_copy(e, end[e] + k * chunk).wait()
            return carry

        lax.fori_loop(0, functools.reduce(jnp.maximum, n_full), fill, 0)
        for e in range(N_EXPERTS):
            zero_copy(e, rows - chunk).start()
        for e in range(N_EXPERTS):
            zero_copy(e, rows - chunk).wait()


def _dispatch(slot, h2e, starts, nchunk, rows):
    n = h2e.shape[0]
    tile, chunk = ROUTE_TILE, ROUTE_CHUNK
    n_tiles = n // tile
    return pl.pallas_call(
        functools.partial(_dispatch_kernel, tile=tile, chunk=chunk),
        grid_spec=pltpu.PrefetchScalarGridSpec(
            num_scalar_prefetch=2, grid=(n_tiles,),
            in_specs=[pl.BlockSpec((N_EXPERTS, tile), lambda i, *_: (0, i)),
                      pl.BlockSpec((tile, D_MODEL + LANES), lambda i, *_: (i, 0)),
                      pl.BlockSpec((None, N_EXPERTS, 1), lambda i, *_: (i, 0, 0))],
            out_specs=pl.BlockSpec(memory_space=pl.ANY),
            scratch_shapes=[pltpu.VMEM((2, N_EXPERTS * chunk, XE_W), BF16),
                            pltpu.VMEM((N_EXPERTS, SEG_ALIGN, XE_W), BF16), pltpu.SemaphoreType.DMA((2,))]),
        out_shape=jax.ShapeDtypeStruct((N_EXPERTS, rows, XE_W), BF16),
        compiler_params=_params(("arbitrary",)),
        name="dispatch",
    )(starts.reshape(-1), nchunk, slot, h2e, starts.reshape(n_tiles, N_EXPERTS, 1))


def _ffn_kernel(tot_ref, *refs, tile, first_tile):
    n_grp = len(first_tile) - 1
    xe_refs, (wg_ref, wu_ref, wd_ref) = refs[:n_grp], refs[n_grp:n_grp + 3]
    ye_refs, (wg_s, wu_s, wd_s) = refs[n_grp + 3:2 * n_grp + 3], refs[2 * n_grp + 3:]
    e = pl.program_id(0)
    j = pl.program_id(1)

    @pl.when(j == 0)
    def _():
        wg_s[...] = wg_ref[...].astype(BF16)
        wu_s[...] = wu_ref[...].astype(BF16)
        wd_s[...] = wd_ref[...].astype(BF16)

    def run(xe_ref, ye_ref, live):
        @pl.when(live > 0)
        def _():
            x = xe_ref[:, :D_MODEL]
            pieces = xe_ref[:, D_MODEL:].astype(F32)
            lane = lax.broadcasted_iota(I32, pieces.shape, 1)
            mine = ((lane & (N_EXPERTS - 1)) == e) & (lane < 3 * N_EXPERTS)
            gate = jnp.sum(jnp.where(mine, pieces, 0.0), axis=-1, keepdims=True)
            hid = []
            for fc in range(wg_s.shape[1] // FFN_COLS):
                cs = slice(fc * FFN_COLS, (fc + 1) * FFN_COLS)
                g = _dot(x, wg_s[:, cs])
                u = _dot(x, wu_s[:, cs])
                hid.append((g * jax.nn.sigmoid(g) * u).astype(BF16))
            y = _dot(jnp.concatenate(hid, axis=1), wd_s[...]) * gate
            rows = lax.broadcasted_iota(I32, y.shape, 0)
            ye_ref[...] = jnp.where(rows < live, y, 0.0).astype(BF16)

        @pl.when(live <= 0)
        def _():
            ye_ref[...] = jnp.zeros(ye_ref.shape, BF16)

    for g in range(n_grp):
        @pl.when((j >= first_tile[g]) & (j < first_tile[g + 1]))
        def _():
            run(xe_refs[g], ye_refs[g], tot_ref[g * N_EXPERTS + e] - (j - first_tile[g]) * tile)


def _ffn(xes, tots, w_gate, w_up, w_down):
    tile = FFN_TILE
    n_exp, d, f = w_gate.shape
    n_tiles = [xe.shape[1] // tile for xe in xes]
    first_tile = [sum(n_tiles[:g]) for g in range(len(xes) + 1)]

    def xe_spec(g):
        def index(e, j, tot):
            last = jnp.maximum(tot[g * N_EXPERTS + e] - 1, 0) // tile
            return (e, jnp.clip(j - first_tile[g], 0, last), 0)
        return pl.BlockSpec((None, tile, XE_W), index)

    def ye_spec(g):
        return pl.BlockSpec((None, tile, d), lambda e, j, tot: (e, jnp.clip(j - first_tile[g], 0, n_tiles[g] - 1), 0))

    wspec = lambda a, b: pl.BlockSpec((None, a, b), lambda e, j, tot: (e, 0, 0))
    return pl.pallas_call(
        functools.partial(_ffn_kernel, tile=tile, first_tile=tuple(first_tile)),
        grid_spec=pltpu.PrefetchScalarGridSpec(
            num_scalar_prefetch=1, grid=(n_exp, first_tile[-1]),
            in_specs=[xe_spec(g) for g in range(len(xes))] + [wspec(d, f), wspec(d, f), wspec(f, d)],
            out_specs=[ye_spec(g) for g in range(len(xes))],
            scratch_shapes=[pltpu.VMEM((d, f), BF16), pltpu.VMEM((d, f), BF16), pltpu.VMEM((f, d), BF16)]),
        out_shape=[jax.ShapeDtypeStruct((n_exp, xe.shape[1], d), BF16) for xe in xes],
        compiler_params=_params(("arbitrary", "arbitrary")),
        name="ffn",
    )(jnp.concatenate(tots), *xes, w_gate, w_up, w_down)


def _combine_kernel(starts_ref, nchunk_ref, x1_ref, slot_ref, srow_ref, ye_hbm, out_ref, stage, sem, *, tile, chunk):
    i = pl.program_id(0)
    sl = i % 2
    over = 2

    def copies(step, c, buf):
        return [pltpu.make_async_copy(
            ye_hbm.at[e, pl.ds(_seg_base(starts_ref, step, e) + c * chunk, chunk), :],
            stage.at[buf, pl.ds(e * chunk, chunk), :],
            sem.at[buf]) for e in range(N_EXPERTS)]

    @pl.when(i == 0)
    def _():
        for cp in copies(0, 0, 0):
            cp.start()

    @pl.when(i + 1 < pl.num_programs(0))
    def _():
        for cp in copies(i + 1, 0, 1 - sl):
            cp.start()

    col = lax.broadcasted_iota(I32, (N_EXPERTS, N_EXPERTS * chunk), 1)
    exp_row = lax.broadcasted_iota(I32, (N_EXPERTS, N_EXPERTS * chunk), 0)
    spread = jnp.where(col // chunk == exp_row, 1.0, 0.0).astype(BF16)
    slot = slot_ref[...]
    rank = jnp.where(slot < 0, -float(tile), slot.astype(F32)).astype(BF16)
    head = jnp.broadcast_to((srow_ref[...] & (SEG_ALIGN - 1)).astype(F32).astype(BF16), (8, N_EXPERTS))
    slots = _dot(rank, spread) + _dot(head, spread)[0:1]
    lane_slot = (lax.broadcasted_iota(I32, slots.shape, 1) & (chunk - 1)).astype(F32)

    def gathered(c, buf):
        onehot = jnp.where(slots == lane_slot + c * chunk, 1.0, 0.0).astype(BF16)
        return _dot(onehot, stage[buf])

    for cp in copies(i, 0, sl):
        cp.wait()
    out_ref[...] = x1_ref[...] + gathered(0, sl)

    def overflow(c, carry):
        for cp in copies(i, c, over):
            cp.start()
        for cp in copies(i, c, over):
            cp.wait()
        out_ref[...] += gathered(c, over)
        return carry

    lax.fori_loop(1, nchunk_ref[i], overflow, 0)


def _combine(x1, slot_t, ye, starts, nchunk):
    n, d = x1.shape
    tile, chunk = ROUTE_TILE, ROUTE_CHUNK
    n_tiles = n // tile
    return pl.pallas_call(
        functools.partial(_combine_kernel, tile=tile, chunk=chunk),
        grid_spec=pltpu.PrefetchScalarGridSpec(
            num_scalar_prefetch=2, grid=(n_tiles,),
            in_specs=[pl.BlockSpec((tile, d), lambda i, *_: (i, 0)),
                      pl.BlockSpec((tile, N_EXPERTS), lambda i, *_: (i, 0)),
                      pl.BlockSpec((None, 1, N_EXPERTS), lambda i, *_: (i, 0, 0)),
                      pl.BlockSpec(memory_space=pl.ANY)],
            out_specs=pl.BlockSpec((tile, d), lambda i, *_: (i, 0)),
            scratch_shapes=[pltpu.VMEM((3, N_EXPERTS * chunk, d), BF16), pltpu.SemaphoreType.DMA((3,))]),
        out_shape=jax.ShapeDtypeStruct((n, d), F32),
        compiler_params=_params(("arbitrary",)),
        name="combine",
    )(starts.reshape(-1), nchunk, x1, slot_t, starts.reshape(n_tiles, 1, N_EXPERTS), ye)


def _expert_rows(cap):
    need = cap + ROUTE_TILE + 2 * ROUTE_CHUNK
    return -(-need // FFN_TILE) * FFN_TILE


def _routed_ffn(groups, w):
    routes = []
    for x1, h2e, aff_t in groups:
        cap = max(1, EC_CAPACITY * x1.shape[0] // N_EXPERTS)
        slot, starts, nchunk, tot = _select(aff_t, cap)
        routes.append((slot, starts, nchunk, tot, _dispatch(slot, h2e, starts, nchunk, _expert_rows(cap))))
    yes = _ffn([r[4] for r in routes], [r[3] for r in routes], w["w_exp_gate"], w["w_exp_up"], w["w_exp_down"])
    return [_combine(x1, slot.T, ye, starts, nchunk)
            for (x1, _, _), (slot, starts, nchunk, _, _), ye in zip(groups, routes, yes)]


def _mixers(x, mem, w):
    b, s, d = x.shape
    x2 = x.reshape(b * s, d)
    kx, vx = _memkv(mem, w["mem_norm_g"], w["w_mem_kv"], w["xk_gain"])
    *qkv, p, g1 = _proj(x2, s, kx, vx, w)
    os_, ls_ = [], []
    for g in range(len(DIL_GROUPS)):
        o, lse = _attn(qkv[g], qkv[3 + g], qkv[6 + g], b, s, g)
        os_.append(o)
        ls_.append(lse)
    return _merge(x2, p, g1, os_, ls_, w)


def _layer(xs, mems, w):
    outs = _routed_ffn([_mixers(x, mem, w) for x, mem in zip(xs, mems)], w)
    return [o.reshape(x.shape) for o, x in zip(outs, xs)]


def _prepare(norm1_g, w_in, conv_w, q_norm_g, k_norm_g, mem_norm_g, w_mem_kv, xq_norm_g, xk_norm_g,
             w_br_conv, w_br_attn, w_br_xattn, w_o, norm2_g, w_router, w_exp_gate, w_exp_up, w_exp_down):
    row = lambda v: v.reshape(1, -1).astype(F32)
    head = jnp.arange(GROUP_WIDTH) // ATT_HEAD_DIM
    return {
        "norm1_g": row(norm1_g), "norm2_g": row(norm2_g), "mem_norm_g": row(mem_norm_g),
        "w_in": w_in.astype(BF16), "conv_w": conv_w.astype(F32),
        "q_gain": row(jnp.tile(q_norm_g, HEADS_PER_GROUP) * (ATT_HEAD_DIM ** -0.5)),
        "k_gain": row(jnp.tile(k_norm_g, HEADS_PER_GROUP)),
        "xq_gain": row(xq_norm_g), "xk_gain": row(xk_norm_g),
        "w_mem_kv": w_mem_kv.astype(BF16), "w_br_conv": w_br_conv.astype(BF16),
        "w_br_attn": w_br_attn.astype(BF16), "w_br_xattn": w_br_xattn.astype(BF16), "w_o": w_o.astype(BF16),
        "head_avg": jnp.where(head[:, None] == head[None, :], 1.0 / ATT_HEAD_DIM, 0.0).astype(BF16),
        "w_router3": jnp.concatenate([w_router] * 3 + [jnp.zeros((D_MODEL, LANES - 3 * N_EXPERTS), F32)],
                                     axis=1).astype(BF16),
        "w_exp_gate": w_exp_gate, "w_exp_up": w_exp_up, "w_exp_down": w_exp_down,
    }


def kernel(x_prompt, x_sample, mem_prompt, mem_sample, norm1_g, w_in, conv_w, q_norm_g, k_norm_g, mem_norm_g, w_mem_kv, xq_norm_g, xk_norm_g, w_br_conv, w_br_attn, w_br_xattn, w_o, norm2_g, w_router, w_exp_gate, w_exp_up, w_exp_down):
    per_layer = (norm1_g, w_in, conv_w, q_norm_g, k_norm_g, mem_norm_g, w_mem_kv, xq_norm_g, xk_norm_g,
                 w_br_conv, w_br_attn, w_br_xattn, w_o, norm2_g, w_router, w_exp_gate, w_exp_up, w_exp_down)
    ys = [x_prompt, x_sample]
    for layer in range(norm1_g.shape[0]):
        ys = _layer(ys, (mem_prompt, mem_sample), _prepare(*(t[layer] for t in per_layer)))
    return tuple(ys)
```

```python
import functools

import jax
import jax.numpy as jnp
from jax import lax
from jax.experimental import pallas as pl
from jax.experimental.pallas import tpu as pltpu

F32 = jnp.float32
BF16 = jnp.bfloat16
I32 = jnp.int32

D_MODEL = 1024
N_MEM = 256
CONV_WIDTH = 768
ATT_HEAD_DIM = 64
DIL_GROUPS = ((128, 1), (512, 4), (2048, 16))
HEADS_PER_GROUP = 4
N_ATT_HEADS = HEADS_PER_GROUP * len(DIL_GROUPS)
ATT_WIDTH = N_ATT_HEADS * ATT_HEAD_DIM
GROUP_WIDTH = HEADS_PER_GROUP * ATT_HEAD_DIM
ATT_RADIUS = 64
XATT_HEADS = 4
XATT_HEAD_DIM = 128
XATT_WIDTH = XATT_HEADS * XATT_HEAD_DIM
N_EXPERTS = 16
EC_CAPACITY = 2
ALIBI_MAX_EXP = 8.0
EPS = 1e-6
NEG_INF = -1e30

C_CB, C_CC, C_AQ, C_XQ, C_GATE, C_END = 0, 768, 2304, 4608, 5120, 8192

V7X_VMEM_LIMIT_BYTES = 56 * 1024 * 1024
LANES = 128
HALO = 16

PROJ_TILE = 512
MERGE_SPLIT = 2
ATT_QB = 128
ATT_UNITS = 4
ROUTE_TILE = 256
ROUTE_CHUNK = 64
SEG_ALIGN = 16
FFN_TILE = 512
FFN_COLS = 256
XE_W = D_MODEL + LANES


def _dot(a, b):
    return jnp.dot(a, b, preferred_element_type=F32)


def _dot_nt(a, b):
    return lax.dot_general(a, b, (((1,), (1,)), ((), ())), preferred_element_type=F32)


def _params(sem):
    return pltpu.CompilerParams(dimension_semantics=sem, vmem_limit_bytes=V7X_VMEM_LIMIT_BYTES)


def _full(shape):
    return pl.BlockSpec(shape, lambda *_: (0,) * len(shape))


def _resident(shape):
    return pl.BlockSpec(shape, lambda *_: (0,) * len(shape), pipeline_mode=pl.Buffered(1))


def _memkv_kernel(mem_ref, g_ref, w_ref, kg_ref, k_ref, v_ref):
    m = mem_ref[...]
    hn = (m * lax.rsqrt(jnp.mean(m * m, axis=-1, keepdims=True) + EPS) * g_ref[...]).astype(BF16)
    kv = _dot(hn, w_ref[...])
    ks = []
    for h in range(XATT_HEADS):
        kh = kv[:, h * XATT_HEAD_DIM:(h + 1) * XATT_HEAD_DIM]
        ks.append(kh * lax.rsqrt(jnp.mean(kh * kh, axis=-1, keepdims=True) + EPS) * kg_ref[...])
    k_ref[...] = jnp.concatenate(ks, axis=1).astype(BF16)
    v_ref[...] = kv[:, XATT_WIDTH:].astype(BF16)


def _memkv(mem, mem_g, w_mem_kv, xk_g):
    b, m, d = mem.shape
    out = jax.ShapeDtypeStruct((b, m, XATT_WIDTH), BF16)
    return pl.pallas_call(
        _memkv_kernel,
        grid=(b,),
        in_specs=[pl.BlockSpec((None, m, d), lambda i: (i, 0, 0)), _full((1, d)),
                  _full((d, 2 * XATT_WIDTH)), _full((1, XATT_HEAD_DIM))],
        out_specs=[pl.BlockSpec((None, m, XATT_WIDTH), lambda i: (i, 0, 0))] * 2,
        out_shape=[out, out],
        compiler_params=_params(("arbitrary",)),
        name="memkv",
    )(mem, mem_g, w_mem_kv, xk_g)


def _proj_kernel(xp_ref, x_ref, xn_ref, n1g_ref, win_ref, cw_ref, qg_ref, kg_ref, xqg_ref, kx_ref, vx_ref,
                 wbc_ref, wbx_ref, bd_ref,
                 q0_ref, q1_ref, q2_ref, k0_ref, k1_ref, k2_ref, v0_ref, v1_ref, v2_ref, p_ref, g1_ref,
                 hb_s, u_s, il_s, *, tile, tiles_per_seq):
    tin = pl.program_id(0) % tiles_per_seq
    gain = n1g_ref[...]

    def nrm(x):
        return (x * lax.rsqrt(jnp.mean(x * x, axis=-1, keepdims=True) + EPS) * gain).astype(BF16)

    hb_s[0:HALO, :] = nrm(xp_ref[...])
    hb_s[HALO:HALO + tile, :] = nrm(x_ref[...])
    hb_s[HALO + tile:, :] = nrm(xn_ref[...])
    hc = hb_s[HALO:HALO + tile, :]

    ccx = _dot(hb_s[...], win_ref[:, C_CC:C_AQ])
    u = ccx[:, :CONV_WIDTH] * ccx[:, CONV_WIDTH:]
    u_s[0:HALO, :] = u[0:HALO] * jnp.where(tin == 0, 0.0, 1.0)
    u_s[HALO:HALO + tile, :] = u[HALO:HALO + tile]
    u_s[HALO + tile:, :] = u[HALO + tile:] * jnp.where(tin == tiles_per_seq - 1, 0.0, 1.0)
    cw = cw_ref[...]
    conv = (cw[0:1] * u_s[HALO - 1:HALO - 1 + tile, :] + cw[1:2] * u_s[HALO:HALO + tile, :]
            + cw[2:3] * u_s[HALO + 1:HALO + 1 + tile, :])
    cb = _dot(hc, win_ref[:, C_CB:C_CC])
    y_conv = _dot((cb * conv).astype(BF16), wbc_ref[...])

    qkv = _dot(hc, win_ref[:, C_AQ:C_XQ])
    bd = bd_ref[...]

    def head_norm(z, g_ref):
        outs = []
        for c in range(len(DIL_GROUPS)):
            zc = z[:, c * GROUP_WIDTH:(c + 1) * GROUP_WIDTH]
            ms = _dot((zc * zc).astype(BF16), bd)
            outs.append(zc * lax.rsqrt(ms + EPS) * g_ref[...])
        return outs

    def emit(ref, val, dilation):
        if dilation == 1:
            ref[...] = val.astype(BF16)
            return
        for half in range(GROUP_WIDTH // LANES):
            il_s[half] = val[:, half * LANES:(half + 1) * LANES]
        for r in range(dilation):
            for half in range(GROUP_WIDTH // LANES):
                col = r * GROUP_WIDTH + half * LANES
                ref[:, col:col + LANES] = il_s[half, pl.ds(r, tile // dilation, stride=dilation), :].astype(BF16)

    vals = (head_norm(qkv[:, :ATT_WIDTH], qg_ref) + head_norm(qkv[:, ATT_WIDTH:2 * ATT_WIDTH], kg_ref)
            + [qkv[:, 2 * ATT_WIDTH + c * GROUP_WIDTH:2 * ATT_WIDTH + (c + 1) * GROUP_WIDTH] for c in range(len(DIL_GROUPS))])
    refs = (q0_ref, q1_ref, q2_ref, k0_ref, k1_ref, k2_ref, v0_ref, v1_ref, v2_ref)
    for n, (ref, val) in enumerate(zip(refs, vals)):
        emit(ref, val, DIL_GROUPS[n % len(DIL_GROUPS)][1])

    xq = _dot(hc, win_ref[:, C_XQ:C_GATE])
    kx = kx_ref[...]
    vx = vx_ref[...]
    outs = []
    for h in range(XATT_HEADS):
        hs = slice(h * XATT_HEAD_DIM, (h + 1) * XATT_HEAD_DIM)
        qh = xq[:, hs]
        qh = qh * lax.rsqrt(jnp.mean(qh * qh, axis=-1, keepdims=True) + EPS) * xqg_ref[...]
        s = _dot_nt(qh.astype(BF16), kx[:, hs]) * (XATT_HEAD_DIM ** -0.5)
        p = jnp.exp(s - jnp.max(s, axis=-1, keepdims=True))
        den = jnp.sum(p, axis=-1, keepdims=True)
        outs.append(_dot(p.astype(BF16), vx[:, hs]) / den)
    y_x = _dot(jnp.concatenate(outs, axis=1).astype(BF16), wbx_ref[...])

    gs = jax.nn.sigmoid(_dot(hc, win_ref[:, C_GATE:C_END]))
    p_ref[...] = (gs[:, :D_MODEL] * y_conv + gs[:, 2 * D_MODEL:] * y_x).astype(BF16)
    g1_ref[...] = gs[:, D_MODEL:2 * D_MODEL].astype(BF16)


def _proj(x2, seq_len, kx, vx, w):
    n, d = x2.shape
    tile = PROJ_TILE
    tps = seq_len // tile
    hb = tile // HALO
    n_hblk = n // HALO
    wide = jax.ShapeDtypeStruct((n, d), BF16)
    row = lambda width: pl.BlockSpec((tile, width), lambda i: (i, 0))
    dils = [dil for _, dil in DIL_GROUPS] * 3
    grp_shapes = [jax.ShapeDtypeStruct((n // dil, dil * GROUP_WIDTH), BF16) for dil in dils]
    grp_specs = [pl.BlockSpec((tile // dil, dil * GROUP_WIDTH), lambda i: (i, 0)) for dil in dils]
    return pl.pallas_call(
        functools.partial(_proj_kernel, tile=tile, tiles_per_seq=tps),
        grid=(n // tile,),
        in_specs=[
            pl.BlockSpec((HALO, d), lambda i: (jnp.maximum(i * hb - 1, 0), 0)),
            row(d),
            pl.BlockSpec((HALO, d), lambda i: (jnp.minimum((i + 1) * hb, n_hblk - 1), 0)),
            _full((1, d)), _resident((d, C_END)), _full((3, CONV_WIDTH)),
            _full((1, GROUP_WIDTH)), _full((1, GROUP_WIDTH)), _full((1, XATT_HEAD_DIM)),
            pl.BlockSpec((None, N_MEM, XATT_WIDTH), lambda i: (i // tps, 0, 0)),
            pl.BlockSpec((None, N_MEM, XATT_WIDTH), lambda i: (i // tps, 0, 0)),
            _resident((CONV_WIDTH, d)), _resident((XATT_WIDTH, d)), _full((GROUP_WIDTH, GROUP_WIDTH)),
        ],
        out_specs=grp_specs + [row(d), row(d)],
        out_shape=grp_shapes + [wide, wide],
        scratch_shapes=[pltpu.VMEM((tile + 2 * HALO, d), BF16), pltpu.VMEM((tile + 2 * HALO, CONV_WIDTH), F32),
                        pltpu.VMEM((GROUP_WIDTH // LANES, tile, LANES), F32)],
        compiler_params=_params(("arbitrary",)),
        name="proj",
    )(x2, x2, x2, w["norm1_g"], w["w_in"], w["conv_w"], w["q_gain"], w["k_gain"], w["xq_gain"], kx, vx,
      w["w_br_conv"], w["w_br_xattn"], w["head_avg"])


def _attn_kernel(q_ref, k_ref, v_ref, bias_ref, o_ref, lse_ref, *, seq_sub, lq, qb, kw, n_res):
    i = pl.program_id(2)
    lane_head = lax.broadcasted_iota(I32, (qb, GROUP_WIDTH), 1) // ATT_HEAD_DIM
    head_mask = [lane_head == h for h in range(HEADS_PER_GROUP)]
    head_mask_bf = [jnp.where(m, 1.0, 0.0).astype(BF16) for m in head_mask]

    blocks_per_iter = max(1, min(lq // qb, ATT_UNITS // n_res))

    def iteration(it, carry):
        units = []
        for b in range(blocks_per_iter):
            row0 = pl.multiple_of((it * blocks_per_iter + b) * qb, qb)
            qs = i * lq + row0
            ks = pl.multiple_of(jnp.clip(qs - ATT_RADIUS, 0, seq_sub - kw), ATT_RADIUS)
            case = jnp.where(qs == 0, 0, jnp.where(qs == seq_sub - qb, 2, 1))
            units += [(row0, ks, case, slice(r * GROUP_WIDTH, (r + 1) * GROUP_WIDTH)) for r in range(n_res)]

        sts = [_dot_nt(k_ref[pl.ds(ks, kw), cs],
                       jnp.concatenate([q_ref[pl.ds(row0, qb), cs] * m for m in head_mask_bf], axis=0))
               for row0, ks, case, cs in units]
        pts, lses = [], []
        for st, (row0, ks, case, cs) in zip(sts, units):
            pt, lse = [], []
            for h in range(HEADS_PER_GROUP):
                sh = st[:, h * qb:(h + 1) * qb] + bias_ref[case * HEADS_PER_GROUP + h]
                m = jnp.max(sh, axis=0, keepdims=True)
                p = jnp.exp(sh - m)
                den = jnp.sum(p, axis=0, keepdims=True)
                pt.append((p * (1.0 / den)).astype(BF16))
                lse.append(m + jnp.log(den))
            pts.append(jnp.concatenate(pt, axis=1))
            lses.append(lse)
        ofs = [lax.dot_general(pt, v_ref[pl.ds(ks, kw), cs], (((0,), (0,)), ((), ())), preferred_element_type=F32)
               for pt, (row0, ks, case, cs) in zip(pts, units)]
        for of, lse, (row0, ks, case, cs) in zip(ofs, lses, units):
            lse_col = jnp.concatenate(lse + [jnp.zeros((qb - HEADS_PER_GROUP, qb), F32)], axis=0).T
            o = jnp.zeros((qb, GROUP_WIDTH), F32)
            lb = jnp.zeros((qb, GROUP_WIDTH), F32)
            for h in range(HEADS_PER_GROUP):
                o = jnp.where(head_mask[h], of[h * qb:(h + 1) * qb], o)
                lb = jnp.where(head_mask[h], lse_col[:, h:h + 1], lb)
            o_ref[pl.ds(row0, qb), cs] = o.astype(BF16)
            lse_ref[pl.ds(row0, qb), cs] = lb
        return carry

    lax.fori_loop(0, lq // (qb * blocks_per_iter), iteration, 0)


def _attn_bias(dilation, qb, kw, group):
    slopes = jnp.exp2(-ALIBI_MAX_EXP * jnp.arange(1, N_ATT_HEADS + 1, dtype=F32) / N_ATT_HEADS)
    slopes = slopes[group * HEADS_PER_GROUP:(group + 1) * HEADS_PER_GROUP]
    a = jnp.arange(qb)[None, :]
    c = jnp.arange(kw)[:, None]
    tabs = []
    for off in (0, -ATT_RADIUS, qb - kw):
        delta = off + c - a
        dist = (dilation * jnp.abs(delta)).astype(F32)
        bias = -slopes[:, None, None] * dist[None]
        tabs.append(jnp.where((jnp.abs(delta) <= ATT_RADIUS)[None], bias, NEG_INF))
    return jnp.concatenate(tabs, axis=0)


def _attn(q, k, v, batch, seq_len, group):
    _, dilation = DIL_GROUPS[group]
    seq_sub = seq_len // dilation
    qb = min(ATT_QB, seq_sub)
    kw = min(qb + 2 * ATT_RADIUS, seq_sub)
    lq = min(4 * qb, seq_sub)
    n_res = min(dilation, 4)
    width = n_res * GROUP_WIDTH
    view = lambda t: t.reshape(batch, seq_sub, dilation * GROUP_WIDTH)
    qspec = pl.BlockSpec((None, lq, width), lambda b, r, i: (b, i, r))
    kspec = pl.BlockSpec((None, seq_sub, width), lambda b, r, i: (b, 0, r))
    bias = _attn_bias(dilation, qb, kw, group)
    o, lse = pl.pallas_call(
        functools.partial(_attn_kernel, seq_sub=seq_sub, lq=lq, qb=qb, kw=kw, n_res=n_res),
        grid=(batch, dilation // n_res, seq_sub // lq),
        in_specs=[qspec, kspec, kspec, _full(bias.shape)],
        out_specs=[qspec, qspec],
        out_shape=[jax.ShapeDtypeStruct((batch, seq_sub, dilation * GROUP_WIDTH), BF16),
                   jax.ShapeDtypeStruct((batch, seq_sub, dilation * GROUP_WIDTH), F32)],
        compiler_params=_params(("arbitrary", "arbitrary", "arbitrary")),
        name=f"attn_d{dilation}",
    )(view(q), view(k), view(v), bias)
    rows = batch * seq_sub
    return o.reshape(rows, dilation * GROUP_WIDTH), lse.reshape(rows, dilation * GROUP_WIDTH)


def _merge_kernel(x_ref, p_ref, g1_ref, o0_ref, o1_ref, o2_ref, l0_ref, l1_ref, l2_ref, wba_ref, wo_ref,
                  n2g_ref, wr_ref, x1_ref, h2e_ref, afft_ref, *il_s, tile):
    halves = GROUP_WIDTH // LANES
    dils = [dil for _, dil in DIL_GROUPS] * 2
    grp_refs = (o0_ref, o1_ref, o2_ref, l0_ref, l1_ref, l2_ref)

    for ref, dilation, scratch in zip(grp_refs, dils, il_s):
        for r in range(dilation if dilation > 1 else 0):
            for half in range(halves):
                col = r * GROUP_WIDTH + half * LANES
                scratch[half, pl.ds(r, tile // dilation, stride=dilation), :] = ref[:, col:col + LANES].astype(F32)

    def token_order(n, rs):
        if dils[n] == 1:
            return grp_refs[n][rs, :].astype(F32)
        return jnp.concatenate([il_s[n][half, rs, :] for half in range(halves)], axis=1)

    rows = tile // MERGE_SPLIT
    blocks = [slice(blk * rows, (blk + 1) * rows) for blk in range(MERGE_SPLIT)]

    def mixture(rs):
        o0, o1, o2, l0, l1, l2 = (token_order(n, rs) for n in range(6))
        m = jnp.maximum(jnp.maximum(l0, l1), l2)
        e0, e1, e2 = jnp.exp(l0 - m), jnp.exp(l1 - m), jnp.exp(l2 - m)
        return ((e0 * o0 + e1 * o1 + e2 * o2) / (e0 + e1 + e2)).astype(BF16)

    mixed = [mixture(rs) for rs in blocks]
    y_attn = [_dot(o, wba_ref[...]) for o in mixed]
    merged = [(p_ref[rs, :].astype(F32) + g1_ref[rs, :].astype(F32) * y).astype(BF16) for rs, y in zip(blocks, y_attn)]
    x1s = [x_ref[rs, :] + _dot(mg, wo_ref[...]) for rs, mg in zip(blocks, merged)]
    h2s = []
    for rs, x1 in zip(blocks, x1s):
        x1_ref[rs, :] = x1
        h2 = (x1 * lax.rsqrt(jnp.mean(x1 * x1, axis=-1, keepdims=True) + EPS) * n2g_ref[...]).astype(BF16)
        h2e_ref[rs, :D_MODEL] = h2
        h2s.append(h2)
    logits = [_dot(h2, wr_ref[...]) for h2 in h2s]
    for rs, lg in zip(blocks, logits):
        lane = lax.broadcasted_iota(I32, lg.shape, 1)
        first = lane < N_EXPERTS
        mx = jnp.max(jnp.where(first, lg, -jnp.inf), axis=-1, keepdims=True)
        ex = jnp.exp(lg - mx)
        a = ex / jnp.sum(jnp.where(first, ex, 0.0), axis=-1, keepdims=True)
        afft_ref[:, rs] = a.T[:N_EXPERTS, :]
        hi = a.astype(BF16).astype(F32)
        mid = (a - hi).astype(BF16).astype(F32)
        lo = (a - hi) - mid
        ext = jnp.where(first, hi, jnp.where(lane < 2 * N_EXPERTS, mid, jnp.where(lane < 3 * N_EXPERTS, lo, 0.0)))
        h2e_ref[rs, D_MODEL:] = ext.astype(BF16)


def _merge(x2, p, g1, os_, ls_, w):
    n, d = x2.shape
    tile = PROJ_TILE
    row = lambda width: pl.BlockSpec((tile, width), lambda i: (i, 0))
    grp = [pl.BlockSpec((tile // dil, dil * GROUP_WIDTH), lambda i: (i, 0)) for _, dil in DIL_GROUPS]
    return pl.pallas_call(
        functools.partial(_merge_kernel, tile=tile),
        grid=(n // tile,),
        in_specs=[row(d), row(d), row(d)] + grp + grp
                 + [_full((GROUP_WIDTH, d)), _full((d, d)), _full((1, d)), _full((d, LANES))],
        out_specs=[row(d), row(d + LANES), pl.BlockSpec((N_EXPERTS, tile), lambda i: (0, i))],
        out_shape=[jax.ShapeDtypeStruct((n, d), F32), jax.ShapeDtypeStruct((n, d + LANES), BF16),
                   jax.ShapeDtypeStruct((N_EXPERTS, n), F32)],
        scratch_shapes=[pltpu.VMEM((GROUP_WIDTH // LANES, tile, LANES), F32)] * 6,
        compiler_params=_params(("arbitrary",)),
        name="merge",
    )(x2, p, g1, *os_, *ls_, w["w_br_attn"], w["w_o"], w["norm2_g"], w["w_router3"])


def _select_kernel(aff_ref, slot_ref, starts_ref, nchunk_ref, tot_ref, *, n_tok, cap, tile):
    n_tiles = n_tok // tile

    def bisect(b, thr_bits):
        cand = thr_bits | jnp.left_shift(jnp.int32(1), 30 - b)
        cnt = jnp.sum(jnp.where(aff_ref[...] >= lax.bitcast_convert_type(cand, F32), 1.0, 0.0), axis=1, keepdims=True)
        return jnp.where(cnt >= cap, cand, thr_bits)

    thr = lax.bitcast_convert_type(lax.fori_loop(0, 31, bisect, jnp.zeros((N_EXPERTS, 1), I32)), F32)
    need = cap - jnp.sum(jnp.where(aff_ref[...] > thr, 1.0, 0.0), axis=1, keepdims=True)

    r = lax.broadcasted_iota(I32, (tile, tile), 0)
    c = lax.broadcasted_iota(I32, (tile, tile), 1)
    before = jnp.where(r < c, 1.0, 0.0).astype(BF16)

    def tile_body(i, carry):
        start, eq_seen = carry
        off = pl.multiple_of(i * tile, tile)
        a = aff_ref[:, pl.ds(off, tile)]
        eq = jnp.where(a == thr, 1.0, 0.0)
        eq_rank = eq_seen + _dot(eq.astype(BF16), before)
        sel = jnp.where(a > thr, 1.0, jnp.where(eq_rank < need, eq, 0.0))
        rank = _dot(sel.astype(BF16), before)
        slot_ref[:, pl.ds(off, tile)] = jnp.where(sel > 0.0, rank, -1.0).astype(I32)
        cnt = jnp.sum(sel, axis=1, keepdims=True)
        starts_ref[i] = jnp.broadcast_to(start, (N_EXPERTS, LANES)).astype(I32)
        head = start - jnp.floor(start * (1.0 / SEG_ALIGN)) * SEG_ALIGN
        nch = jnp.max(jnp.floor((head + cnt) * (1.0 / ROUTE_CHUNK)) + 1.0, axis=0, keepdims=True)
        nchunk_ref[i] = jnp.broadcast_to(nch, (8, LANES)).astype(I32)
        return start + cnt, eq_seen + jnp.sum(eq, axis=1, keepdims=True)

    zero = jnp.zeros((N_EXPERTS, 1), F32)
    total, _ = lax.fori_loop(0, n_tiles, tile_body, (zero, zero), unroll=4)
    tot_ref[...] = jnp.broadcast_to(total, (N_EXPERTS, LANES)).astype(I32)


def _select(aff_t, cap):
    n_exp, n = aff_t.shape
    tile = ROUTE_TILE
    n_tiles = n // tile
    slot, starts, nchunk, tot = pl.pallas_call(
        functools.partial(_select_kernel, n_tok=n, cap=cap, tile=tile),
        grid=(1,),
        in_specs=[_full((n_exp, n))],
        out_specs=[_full((n_exp, n)), _full((n_tiles, n_exp, LANES)), _full((n_tiles, 8, LANES)), _full((n_exp, LANES))],
        out_shape=[jax.ShapeDtypeStruct((n_exp, n), I32), jax.ShapeDtypeStruct((n_tiles, n_exp, LANES), I32),
                   jax.ShapeDtypeStruct((n_tiles, 8, LANES), I32), jax.ShapeDtypeStruct((n_exp, LANES), I32)],
        compiler_params=_params(("arbitrary",)),
        name="select",
    )(aff_t)
    return slot, starts[:, :, 0], nchunk[:, 0, 0], tot[:, 0]


def _seg_base(starts_ref, step, e):
    return pl.multiple_of((starts_ref[step * N_EXPERTS + e] // SEG_ALIGN) * SEG_ALIGN, SEG_ALIGN)


def _dispatch_kernel(starts_ref, nchunk_ref, slot_ref, h_ref, scol_ref, xe_hbm, stage, head, sem, *, tile, chunk):
    i = pl.program_id(0)
    sl = i % 2

    def copies(step, c, buf):
        return [pltpu.make_async_copy(
            stage.at[buf, pl.ds(e * chunk, chunk), :],
            xe_hbm.at[e, pl.ds(_seg_base(starts_ref, step, e) + c * chunk, chunk), :],
            sem.at[buf]) for e in range(N_EXPERTS)]

    @pl.when(i == 0)
    def _():
        head[...] = jnp.zeros(head.shape, BF16)

    slot = slot_ref[...]
    pos = jnp.where(slot >= 0, slot + (scol_ref[...] & (SEG_ALIGN - 1)), -1)

    def build(c, buf):
        want = lax.broadcasted_iota(I32, (chunk, tile), 0) + c * chunk
        onehot = jnp.concatenate(
            [jnp.where(pos[e:e + 1, :] == want, 1.0, 0.0).astype(BF16) for e in range(N_EXPERTS)], axis=0)
        stage[buf] = _dot(onehot, h_ref[...]).astype(BF16)

    nxt = jnp.minimum(i + 1, pl.num_programs(0) - 1)
    rel = [starts_ref[nxt * N_EXPERTS + e] - _seg_base(starts_ref, i, e) for e in range(N_EXPERTS)]

    def keep_head(e, buf):
        row = pl.multiple_of(e * chunk + ((rel[e] % chunk) // SEG_ALIGN) * SEG_ALIGN, SEG_ALIGN)
        head[e] = stage[buf, pl.ds(row, SEG_ALIGN), :]

    build(0, sl)
    for e in range(N_EXPERTS):
        stage[sl, e * chunk:e * chunk + SEG_ALIGN, :] = stage[sl, e * chunk:e * chunk + SEG_ALIGN, :] + head[e]
    for e in range(N_EXPERTS):
        keep_head(e, sl)

    @pl.when(i > 0)
    def _():
        for cp in copies(i - 1, 0, 1 - sl):
            cp.wait()

    for cp in copies(i, 0, sl):
        cp.start()

    def overflow(c, carry):
        for cp in copies(i, c - 1, sl):
            cp.wait()
        build(c, sl)
        for e in range(N_EXPERTS):
            @pl.when(rel[e] // chunk == c)
            def _():
                keep_head(e, sl)
        for cp in copies(i, c, sl):
            cp.start()
        return carry

    lax.fori_loop(1, nchunk_ref[i], overflow, 0)

    @pl.when(i == pl.num_programs(0) - 1)
    def _():
        for cp in copies(i, 0, sl):
            cp.wait()
        rows = xe_hbm.shape[1]
        zbuf = 1 - sl
        stage[zbuf] = jnp.zeros(stage.shape[1:], BF16)
        end = [_seg_base(starts_ref, i, e) + jnp.maximum(nchunk_ref[i], 1) * chunk for e in range(N_EXPERTS)]
        n_full = [(rows - end[e]) // chunk for e in range(N_EXPERTS)]

        def zero_copy(e, pos):
            return pltpu.make_async_copy(stage.at[zbuf, pl.ds(e * chunk, chunk), :],
                                         xe_hbm.at[e, pl.ds(pl.multiple_of(pos, SEG_ALIGN), chunk), :], sem.at[zbuf])

        def fill(k, carry):
            for e in range(N_EXPERTS):
                @pl.when(k < n_full[e])
                def _():
                    zero_copy(e, end[e] + k * chunk).start()
            for e in range(N_EXPERTS):
                @pl.when(k < n_full[e])
                def _():
                    zero_copy(e, end[e] + k * chunk).wait()
            return carry

        lax.fori_loop(0, functools.reduce(jnp.maximum, n_full), fill, 0)
        for e in range(N_EXPERTS):
            zero_copy(e, rows - chunk).start()
        for e in range(N_EXPERTS):
            zero_copy(e, rows - chunk).wait()


def _dispatch(slot, h2e, starts, nchunk, rows):
    n = h2e.shape[0]
    tile, chunk = ROUTE_TILE, ROUTE_CHUNK
    n_tiles = n // tile
    return pl.pallas_call(
        functools.partial(_dispatch_kernel, tile=tile, chunk=chunk),
        grid_spec=pltpu.PrefetchScalarGridSpec(
            num_scalar_prefetch=2, grid=(n_tiles,),
            in_specs=[pl.BlockSpec((N_EXPERTS, tile), lambda i, *_: (0, i)),
                      pl.BlockSpec((tile, D_MODEL + LANES), lambda i, *_: (i, 0)),
                      pl.BlockSpec((None, N_EXPERTS, 1), lambda i, *_: (i, 0, 0))],
            out_specs=pl.BlockSpec(memory_space=pl.ANY),
            scratch_shapes=[pltpu.VMEM((2, N_EXPERTS * chunk, XE_W), BF16),
                            pltpu.VMEM((N_EXPERTS, SEG_ALIGN, XE_W), BF16), pltpu.SemaphoreType.DMA((2,))]),
        out_shape=jax.ShapeDtypeStruct((N_EXPERTS, rows, XE_W), BF16),
        compiler_params=_params(("arbitrary",)),
        name="dispatch",
    )(starts.reshape(-1), nchunk, slot, h2e, starts.reshape(n_tiles, N_EXPERTS, 1))


def _ffn_kernel(tot_ref, *refs, tile, first_tile):
    n_grp = len(first_tile) - 1
    xe_refs, (wg_ref, wu_ref, wd_ref) = refs[:n_grp], refs[n_grp:n_grp + 3]
    ye_refs, (wg_s, wu_s, wd_s) = refs[n_grp + 3:2 * n_grp + 3], refs[2 * n_grp + 3:]
    e = pl.program_id(0)
    j = pl.program_id(1)

    @pl.when(j == 0)
    def _():
        wg_s[...] = wg_ref[...].astype(BF16)
        wu_s[...] = wu_ref[...].astype(BF16)
        wd_s[...] = wd_ref[...].astype(BF16)

    def run(xe_ref, ye_ref, live):
        @pl.when(live > 0)
        def _():
            x = xe_ref[:, :D_MODEL]
            pieces = xe_ref[:, D_MODEL:].astype(F32)
            lane = lax.broadcasted_iota(I32, pieces.shape, 1)
            mine = ((lane & (N_EXPERTS - 1)) == e) & (lane < 3 * N_EXPERTS)
            gate = jnp.sum(jnp.where(mine, pieces, 0.0), axis=-1, keepdims=True)
            hid = []
            for fc in range(wg_s.shape[1] // FFN_COLS):
                cs = slice(fc * FFN_COLS, (fc + 1) * FFN_COLS)
                g = _dot(x, wg_s[:, cs])
                u = _dot(x, wu_s[:, cs])
                hid.append((g * jax.nn.sigmoid(g) * u).astype(BF16))
            y = _dot(jnp.concatenate(hid, axis=1), wd_s[...]) * gate
            rows = lax.broadcasted_iota(I32, y.shape, 0)
            ye_ref[...] = jnp.where(rows < live, y, 0.0).astype(BF16)

        @pl.when(live <= 0)
        def _():
            ye_ref[...] = jnp.zeros(ye_ref.shape, BF16)

    for g in range(n_grp):
        @pl.when((j >= first_tile[g]) & (j < first_tile[g + 1]))
        def _():
            run(xe_refs[g], ye_refs[g], tot_ref[g * N_EXPERTS + e] - (j - first_tile[g]) * tile)


def _ffn(xes, tots, w_gate, w_up, w_down):
    tile = FFN_TILE
    n_exp, d, f = w_gate.shape
    n_tiles = [xe.shape[1] // tile for xe in xes]
    first_tile = [sum(n_tiles[:g]) for g in range(len(xes) + 1)]

    def xe_spec(g):
        def index(e, j, tot):
            last = jnp.maximum(tot[g * N_EXPERTS + e] - 1, 0) // tile
            return (e, jnp.clip(j - first_tile[g], 0, last), 0)
        return pl.BlockSpec((None, tile, XE_W), index)

    def ye_spec(g):
        return pl.BlockSpec((None, tile, d), lambda e, j, tot: (e, jnp.clip(j - first_tile[g], 0, n_tiles[g] - 1), 0))

    wspec = lambda a, b: pl.BlockSpec((None, a, b), lambda e, j, tot: (e, 0, 0))
    return pl.pallas_call(
        functools.partial(_ffn_kernel, tile=tile, first_tile=tuple(first_tile)),
        grid_spec=pltpu.PrefetchScalarGridSpec(
            num_scalar_prefetch=1, grid=(n_exp, first_tile[-1]),
            in_specs=[xe_spec(g) for g in range(len(xes))] + [wspec(d, f), wspec(d, f), wspec(f, d)],
            out_specs=[ye_spec(g) for g in range(len(xes))],
            scratch_shapes=[pltpu.VMEM((d, f), BF16), pltpu.VMEM((d, f), BF16), pltpu.VMEM((f, d), BF16)]),
        out_shape=[jax.ShapeDtypeStruct((n_exp, xe.shape[1], d), BF16) for xe in xes],
        compiler_params=_params(("arbitrary", "arbitrary")),
        name="ffn",
    )(jnp.concatenate(tots), *xes, w_gate, w_up, w_down)


def _combine_kernel(starts_ref, nchunk_ref, x1_ref, slot_ref, srow_ref, ye_hbm, out_ref, stage, sem, *, tile, chunk):
    i = pl.program_id(0)
    sl = i % 2
    over = 2

    def copies(step, c, buf):
        return [pltpu.make_async_copy(
            ye_hbm.at[e, pl.ds(_seg_base(starts_ref, step, e) + c * chunk, chunk), :],
            stage.at[buf, pl.ds(e * chunk, chunk), :],
            sem.at[buf]) for e in range(N_EXPERTS)]

    @pl.when(i == 0)
    def _():
        for cp in copies(0, 0, 0):
            cp.start()

    @pl.when(i + 1 < pl.num_programs(0))
    def _():
        for cp in copies(i + 1, 0, 1 - sl):
            cp.start()

    col = lax.broadcasted_iota(I32, (N_EXPERTS, N_EXPERTS * chunk), 1)
    exp_row = lax.broadcasted_iota(I32, (N_EXPERTS, N_EXPERTS * chunk), 0)
    spread = jnp.where(col // chunk == exp_row, 1.0, 0.0).astype(BF16)
    slot = slot_ref[...]
    rank = jnp.where(slot < 0, -float(tile), slot.astype(F32)).astype(BF16)
    head = jnp.broadcast_to((srow_ref[...] & (SEG_ALIGN - 1)).astype(F32).astype(BF16), (8, N_EXPERTS))
    slots = _dot(rank, spread) + _dot(head, spread)[0:1]
    lane_slot = (lax.broadcasted_iota(I32, slots.shape, 1) & (chunk - 1)).astype(F32)

    def gathered(c, buf):
        onehot = jnp.where(slots == lane_slot + c * chunk, 1.0, 0.0).astype(BF16)
        return _dot(onehot, stage[buf])

    for cp in copies(i, 0, sl):
        cp.wait()
    out_ref[...] = x1_ref[...] + gathered(0, sl)

    def overflow(c, carry):
        for cp in copies(i, c, over):
            cp.start()
        for cp in copies(i, c, over):
            cp.wait()
        out_ref[...] += gathered(c, over)
        return carry

    lax.fori_loop(1, nchunk_ref[i], overflow, 0)


def _combine(x1, slot_t, ye, starts, nchunk):
    n, d = x1.shape
    tile, chunk = ROUTE_TILE, ROUTE_CHUNK
    n_tiles = n // tile
    return pl.pallas_call(
        functools.partial(_combine_kernel, tile=tile, chunk=chunk),
        grid_spec=pltpu.PrefetchScalarGridSpec(
            num_scalar_prefetch=2, grid=(n_tiles,),
            in_specs=[pl.BlockSpec((tile, d), lambda i, *_: (i, 0)),
                      pl.BlockSpec((tile, N_EXPERTS), lambda i, *_: (i, 0)),
                      pl.BlockSpec((None, 1, N_EXPERTS), lambda i, *_: (i, 0, 0)),
                      pl.BlockSpec(memory_space=pl.ANY)],
            out_specs=pl.BlockSpec((tile, d), lambda i, *_: (i, 0)),
            scratch_shapes=[pltpu.VMEM((3, N_EXPERTS * chunk, d), BF16), pltpu.SemaphoreType.DMA((3,))]),
        out_shape=jax.ShapeDtypeStruct((n, d), F32),
        compiler_params=_params(("arbitrary",)),
        name="combine",
    )(starts.reshape(-1), nchunk, x1, slot_t, starts.reshape(n_tiles, 1, N_EXPERTS), ye)


def _expert_rows(cap):
    need = cap + ROUTE_TILE + 2 * ROUTE_CHUNK
    return -(-need // FFN_TILE) * FFN_TILE


def _routed_ffn(groups, w):
    routes = []
    for x1, h2e, aff_t in groups:
        cap = max(1, EC_CAPACITY * x1.shape[0] // N_EXPERTS)
        slot, starts, nchunk, tot = _select(aff_t, cap)
        routes.append((slot, starts, nchunk, tot, _dispatch(slot, h2e, starts, nchunk, _expert_rows(cap))))
    yes = _ffn([r[4] for r in routes], [r[3] for r in routes], w["w_exp_gate"], w["w_exp_up"], w["w_exp_down"])
    return [_combine(x1, slot.T, ye, starts, nchunk)
            for (x1, _, _), (slot, starts, nchunk, _, _), ye in zip(groups, routes, yes)]


def _mixers(x, mem, w):
    b, s, d = x.shape
    x2 = x.reshape(b * s, d)
    kx, vx = _memkv(mem, w["mem_norm_g"], w["w_mem_kv"], w["xk_gain"])
    *qkv, p, g1 = _proj(x2, s, kx, vx, w)
    os_, ls_ = [], []
    for g in range(len(DIL_GROUPS)):
        o, lse = _attn(qkv[g], qkv[3 + g], qkv[6 + g], b, s, g)
        os_.append(o)
        ls_.append(lse)
    return _merge(x2, p, g1, os_, ls_, w)


def _layer(xs, mems, w):
    outs = _routed_ffn([_mixers(x, mem, w) for x, mem in zip(xs, mems)], w)
    return [o.reshape(x.shape) for o, x in zip(outs, xs)]


def _prepare(norm1_g, w_in, conv_w, q_norm_g, k_norm_g, mem_norm_g, w_mem_kv, xq_norm_g, xk_norm_g,
             w_br_conv, w_br_attn, w_br_xattn, w_o, norm2_g, w_router, w_exp_gate, w_exp_up, w_exp_down):
    row = lambda v: v.reshape(1, -1).astype(F32)
    head = jnp.arange(GROUP_WIDTH) // ATT_HEAD_DIM
    return {
        "norm1_g": row(norm1_g), "norm2_g": row(norm2_g), "mem_norm_g": row(mem_norm_g),
        "w_in": w_in.astype(BF16), "conv_w": conv_w.astype(F32),
        "q_gain": row(jnp.tile(q_norm_g, HEADS_PER_GROUP) * (ATT_HEAD_DIM ** -0.5)),
        "k_gain": row(jnp.tile(k_norm_g, HEADS_PER_GROUP)),
        "xq_gain": row(xq_norm_g), "xk_gain": row(xk_norm_g),
        "w_mem_kv": w_mem_kv.astype(BF16), "w_br_conv": w_br_conv.astype(BF16),
        "w_br_attn": w_br_attn.astype(BF16), "w_br_xattn": w_br_xattn.astype(BF16), "w_o": w_o.astype(BF16),
        "head_avg": jnp.where(head[:, None] == head[None, :], 1.0 / ATT_HEAD_DIM, 0.0).astype(BF16),
        "w_router3": jnp.concatenate([w_router] * 3 + [jnp.zeros((D_MODEL, LANES - 3 * N_EXPERTS), F32)],
                                     axis=1).astype(BF16),
        "w_exp_gate": w_exp_gate, "w_exp_up": w_exp_up, "w_exp_down": w_exp_down,
    }


def kernel(x_prompt, x_sample, mem_prompt, mem_sample, norm1_g, w_in, conv_w, q_norm_g, k_norm_g, mem_norm_g, w_mem_kv, xq_norm_g, xk_norm_g, w_br_conv, w_br_attn, w_br_xattn, w_o, norm2_g, w_router, w_exp_gate, w_exp_up, w_exp_down):
    per_layer = (norm1_g, w_in, conv_w, q_norm_g, k_norm_g, mem_norm_g, w_mem_kv, xq_norm_g, xk_norm_g,
                 w_br_conv, w_br_attn, w_br_xattn, w_o, norm2_g, w_router, w_exp_gate, w_exp_up, w_exp_down)
    ys = [x_prompt, x_sample]
    for layer in range(norm1_g.shape[0]):
        ys = _layer(ys, (mem_prompt, mem_sample), _prepare(*(t[layer] for t in per_layer)))
    return tuple(ys)
```

```python
import functools

import jax
import jax.numpy as jnp
from jax import lax
from jax.experimental import pallas as pl
from jax.experimental.pallas import tpu as pltpu

F32 = jnp.float32
BF16 = jnp.bfloat16
I32 = jnp.int32

D_MODEL = 1024
N_MEM = 256
CONV_WIDTH = 768
ATT_HEAD_DIM = 64
DIL_GROUPS = ((128, 1), (512, 4), (2048, 16))
HEADS_PER_GROUP = 4
N_ATT_HEADS = HEADS_PER_GROUP * len(DIL_GROUPS)
ATT_WIDTH = N_ATT_HEADS * ATT_HEAD_DIM
GROUP_WIDTH = HEADS_PER_GROUP * ATT_HEAD_DIM
ATT_RADIUS = 64
XATT_HEADS = 4
XATT_HEAD_DIM = 128
XATT_WIDTH = XATT_HEADS * XATT_HEAD_DIM
N_EXPERTS = 16
EC_CAPACITY = 2
ALIBI_MAX_EXP = 8.0
EPS = 1e-6
NEG_INF = -1e30

C_CB, C_CC, C_AQ, C_XQ, C_GATE, C_END = 0, 768, 2304, 4608, 5120, 8192

V7X_VMEM_LIMIT_BYTES = 56 * 1024 * 1024
LANES = 128
HALO = 16

PROJ_TILE = 512
MERGE_SPLIT = 2
ATT_QB = 128
ATT_UNITS = 4
ROUTE_TILE = 256
ROUTE_CHUNK = 64
SEG_ALIGN = 16
FFN_TILE = 1024
FFN_COLS = 256
XE_W = D_MODEL + LANES


def _dot(a, b):
    return jnp.dot(a, b, preferred_element_type=F32)


def _dot_nt(a, b):
    return lax.dot_general(a, b, (((1,), (1,)), ((), ())), preferred_element_type=F32)


def _params(sem):
    return pltpu.CompilerParams(dimension_semantics=sem, vmem_limit_bytes=V7X_VMEM_LIMIT_BYTES)


def _full(shape):
    return pl.BlockSpec(shape, lambda *_: (0,) * len(shape))


def _resident(shape):
    return pl.BlockSpec(shape, lambda *_: (0,) * len(shape), pipeline_mode=pl.Buffered(1))


def _memkv_kernel(mem_ref, g_ref, w_ref, kg_ref, k_ref, v_ref):
    m = mem_ref[...]
    hn = (m * lax.rsqrt(jnp.mean(m * m, axis=-1, keepdims=True) + EPS) * g_ref[...]).astype(BF16)
    kv = _dot(hn, w_ref[...])
    ks = []
    for h in range(XATT_HEADS):
        kh = kv[:, h * XATT_HEAD_DIM:(h + 1) * XATT_HEAD_DIM]
        ks.append(kh * lax.rsqrt(jnp.mean(kh * kh, axis=-1, keepdims=True) + EPS) * kg_ref[...])
    k_ref[...] = jnp.concatenate(ks, axis=1).astype(BF16)
    v_ref[...] = kv[:, XATT_WIDTH:].astype(BF16)


def _memkv(mem, mem_g, w_mem_kv, xk_g):
    b, m, d = mem.shape
    out = jax.ShapeDtypeStruct((b, m, XATT_WIDTH), BF16)
    return pl.pallas_call(
        _memkv_kernel,
        grid=(b,),
        in_specs=[pl.BlockSpec((None, m, d), lambda i: (i, 0, 0)), _full((1, d)),
                  _full((d, 2 * XATT_WIDTH)), _full((1, XATT_HEAD_DIM))],
        out_specs=[pl.BlockSpec((None, m, XATT_WIDTH), lambda i: (i, 0, 0))] * 2,
        out_shape=[out, out],
        compiler_params=_params(("arbitrary",)),
        name="memkv",
    )(mem, mem_g, w_mem_kv, xk_g)


def _proj_kernel(xp_ref, x_ref, xn_ref, n1g_ref, win_ref, cw_ref, qg_ref, kg_ref, xqg_ref, kx_ref, vx_ref,
                 wbc_ref, wbx_ref, bd_ref,
                 q0_ref, q1_ref, q2_ref, k0_ref, k1_ref, k2_ref, v0_ref, v1_ref, v2_ref, p_ref, g1_ref,
                 hb_s, u_s, il_s, *, tile, tiles_per_seq):
    tin = pl.program_id(0) % tiles_per_seq
    gain = n1g_ref[...]

    def nrm(x):
        return (x * lax.rsqrt(jnp.mean(x * x, axis=-1, keepdims=True) + EPS) * gain).astype(BF16)

    hb_s[0:HALO, :] = nrm(xp_ref[...])
    hb_s[HALO:HALO + tile, :] = nrm(x_ref[...])
    hb_s[HALO + tile:, :] = nrm(xn_ref[...])
    hc = hb_s[HALO:HALO + tile, :]

    ccx = _dot(hb_s[...], win_ref[:, C_CC:C_AQ])
    u = ccx[:, :CONV_WIDTH] * ccx[:, CONV_WIDTH:]
    u_s[0:HALO, :] = u[0:HALO] * jnp.where(tin == 0, 0.0, 1.0)
    u_s[HALO:HALO + tile, :] = u[HALO:HALO + tile]
    u_s[HALO + tile:, :] = u[HALO + tile:] * jnp.where(tin == tiles_per_seq - 1, 0.0, 1.0)
    cw = cw_ref[...]
    conv = (cw[0:1] * u_s[HALO - 1:HALO - 1 + tile, :] + cw[1:2] * u_s[HALO:HALO + tile, :]
            + cw[2:3] * u_s[HALO + 1:HALO + 1 + tile, :])
    cb = _dot(hc, win_ref[:, C_CB:C_CC])
    y_conv = _dot((cb * conv).astype(BF16), wbc_ref[...])

    qkv = _dot(hc, win_ref[:, C_AQ:C_XQ])
    bd = bd_ref[...]

    def head_norm(z, g_ref):
        outs = []
        for c in range(len(DIL_GROUPS)):
            zc = z[:, c * GROUP_WIDTH:(c + 1) * GROUP_WIDTH]
            ms = _dot((zc * zc).astype(BF16), bd)
            outs.append(zc * lax.rsqrt(ms + EPS) * g_ref[...])
        return outs

    def emit(ref, val, dilation):
        if dilation == 1:
            ref[...] = val.astype(BF16)
            return
        for half in range(GROUP_WIDTH // LANES):
            il_s[half] = val[:, half * LANES:(half + 1) * LANES]
        for r in range(dilation):
            for half in range(GROUP_WIDTH // LANES):
                col = r * GROUP_WIDTH + half * LANES
                ref[:, col:col + LANES] = il_s[half, pl.ds(r, tile // dilation, stride=dilation), :].astype(BF16)

    vals = (head_norm(qkv[:, :ATT_WIDTH], qg_ref) + head_norm(qkv[:, ATT_WIDTH:2 * ATT_WIDTH], kg_ref)
            + [qkv[:, 2 * ATT_WIDTH + c * GROUP_WIDTH:2 * ATT_WIDTH + (c + 1) * GROUP_WIDTH] for c in range(len(DIL_GROUPS))])
    refs = (q0_ref, q1_ref, q2_ref, k0_ref, k1_ref, k2_ref, v0_ref, v1_ref, v2_ref)
    for n, (ref, val) in enumerate(zip(refs, vals)):
        emit(ref, val, DIL_GROUPS[n % len(DIL_GROUPS)][1])

    xq = _dot(hc, win_ref[:, C_XQ:C_GATE])
    kx = kx_ref[...]
    vx = vx_ref[...]
    outs = []
    for h in range(XATT_HEADS):
        hs = slice(h * XATT_HEAD_DIM, (h + 1) * XATT_HEAD_DIM)
        qh = xq[:, hs]
        qh = qh * lax.rsqrt(jnp.mean(qh * qh, axis=-1, keepdims=True) + EPS) * xqg_ref[...]
        s = _dot_nt(qh.astype(BF16), kx[:, hs]) * (XATT_HEAD_DIM ** -0.5)
        p = jnp.exp(s - jnp.max(s, axis=-1, keepdims=True))
        den = jnp.sum(p, axis=-1, keepdims=True)
        outs.append(_dot(p.astype(BF16), vx[:, hs]) / den)
    y_x = _dot(jnp.concatenate(outs, axis=1).astype(BF16), wbx_ref[...])

    gs = jax.nn.sigmoid(_dot(hc, win_ref[:, C_GATE:C_END]))
    p_ref[...] = (gs[:, :D_MODEL] * y_conv + gs[:, 2 * D_MODEL:] * y_x).astype(BF16)
    g1_ref[...] = gs[:, D_MODEL:2 * D_MODEL].astype(BF16)


def _proj(x2, seq_len, kx, vx, w):
    n, d = x2.shape
    tile = PROJ_TILE
    tps = seq_len // tile
    hb = tile // HALO
    n_hblk = n // HALO
    wide = jax.ShapeDtypeStruct((n, d), BF16)
    row = lambda width: pl.BlockSpec((tile, width), lambda i: (i, 0))
    dils = [dil for _, dil in DIL_GROUPS] * 3
    grp_shapes = [jax.ShapeDtypeStruct((n // dil, dil * GROUP_WIDTH), BF16) for dil in dils]
    grp_specs = [pl.BlockSpec((tile // dil, dil * GROUP_WIDTH), lambda i: (i, 0)) for dil in dils]
    return pl.pallas_call(
        functools.partial(_proj_kernel, tile=tile, tiles_per_seq=tps),
        grid=(n // tile,),
        in_specs=[
            pl.BlockSpec((HALO, d), lambda i: (jnp.maximum(i * hb - 1, 0), 0)),
            row(d),
            pl.BlockSpec((HALO, d), lambda i: (jnp.minimum((i + 1) * hb, n_hblk - 1), 0)),
            _full((1, d)), _resident((d, C_END)), _full((3, CONV_WIDTH)),
            _full((1, GROUP_WIDTH)), _full((1, GROUP_WIDTH)), _full((1, XATT_HEAD_DIM)),
            pl.BlockSpec((None, N_MEM, XATT_WIDTH), lambda i: (i // tps, 0, 0)),
            pl.BlockSpec((None, N_MEM, XATT_WIDTH), lambda i: (i // tps, 0, 0)),
            _resident((CONV_WIDTH, d)), _resident((XATT_WIDTH, d)), _full((GROUP_WIDTH, GROUP_WIDTH)),
        ],
        out_specs=grp_specs + [row(d), row(d)],
        out_shape=grp_shapes + [wide, wide],
        scratch_shapes=[pltpu.VMEM((tile + 2 * HALO, d), BF16), pltpu.VMEM((tile + 2 * HALO, CONV_WIDTH), F32),
                        pltpu.VMEM((GROUP_WIDTH // LANES, tile, LANES), F32)],
        compiler_params=_params(("arbitrary",)),
        name="proj",
    )(x2, x2, x2, w["norm1_g"], w["w_in"], w["conv_w"], w["q_gain"], w["k_gain"], w["xq_gain"], kx, vx,
      w["w_br_conv"], w["w_br_xattn"], w["head_avg"])


def _attn_kernel(q_ref, k_ref, v_ref, bias_ref, o_ref, lse_ref, *, seq_sub, lq, qb, kw, n_res):
    i = pl.program_id(2)
    lane_head = lax.broadcasted_iota(I32, (qb, GROUP_WIDTH), 1) // ATT_HEAD_DIM
    head_mask = [lane_head == h for h in range(HEADS_PER_GROUP)]
    head_mask_bf = [jnp.where(m, 1.0, 0.0).astype(BF16) for m in head_mask]

    def block(sb, carry):
        row0 = pl.multiple_of(sb * qb, qb)
        qs = i * lq + row0
        ks = pl.multiple_of(jnp.clip(qs - ATT_RADIUS, 0, seq_sub - kw), ATT_RADIUS)
        case = jnp.where(qs == 0, 0, jnp.where(qs == seq_sub - qb, 2, 1))
        for r in range(n_res):
            cs = slice(r * GROUP_WIDTH, (r + 1) * GROUP_WIDTH)
            q = q_ref[pl.ds(row0, qb), cs]
            kk = k_ref[pl.ds(ks, kw), cs]
            vv = v_ref[pl.ds(ks, kw), cs]
            s = _dot_nt(jnp.concatenate([q * m for m in head_mask_bf], axis=0), kk)
            ps, inv_den, lse = [], [], []
            for h in range(HEADS_PER_GROUP):
                sh = s[h * qb:(h + 1) * qb] + bias_ref[case * HEADS_PER_GROUP + h]
                m = jnp.max(sh, axis=-1, keepdims=True)
                p = jnp.exp(sh - m)
                den = jnp.sum(p, axis=-1, keepdims=True)
                ps.append(p.astype(BF16))
                inv_den.append(1.0 / den)
                lse.append(m + jnp.log(den))
            of = _dot(jnp.concatenate(ps, axis=0), vv)
            o = jnp.zeros((qb, GROUP_WIDTH), F32)
            lb = jnp.zeros((qb, GROUP_WIDTH), F32)
            for h in range(HEADS_PER_GROUP):
                o = jnp.where(head_mask[h], of[h * qb:(h + 1) * qb] * inv_den[h], o)
                lb = jnp.where(head_mask[h], lse[h], lb)
            o_ref[pl.ds(row0, qb), cs] = o.astype(BF16)
            lse_ref[pl.ds(row0, qb), cs] = lb
        return carry

    lax.fori_loop(0, lq // qb, block, 0, unroll=max(1, min(lq // qb, ATT_UNITS // n_res)))


def _attn_bias(dilation, qb, kw, group):
    slopes = jnp.exp2(-ALIBI_MAX_EXP * jnp.arange(1, N_ATT_HEADS + 1, dtype=F32) / N_ATT_HEADS)
    slopes = slopes[group * HEADS_PER_GROUP:(group + 1) * HEADS_PER_GROUP]
    a = jnp.arange(qb)[:, None]
    c = jnp.arange(kw)[None, :]
    tabs = []
    for off in (0, -ATT_RADIUS, qb - kw):
        delta = off + c - a
        dist = (dilation * jnp.abs(delta)).astype(F32)
        bias = -slopes[:, None, None] * dist[None]
        tabs.append(jnp.where((jnp.abs(delta) <= ATT_RADIUS)[None], bias, NEG_INF))
    return jnp.concatenate(tabs, axis=0)


def _attn(q, k, v, batch, seq_len, group):
    _, dilation = DIL_GROUPS[group]
    seq_sub = seq_len // dilation
    qb = min(ATT_QB, seq_sub)
    kw = min(qb + 2 * ATT_RADIUS, seq_sub)
    lq = min(4 * qb, seq_sub)
    n_res = min(dilation, 4)
    width = n_res * GROUP_WIDTH
    view = lambda t: t.reshape(batch, seq_sub, dilation * GROUP_WIDTH)
    qspec = pl.BlockSpec((None, lq, width), lambda b, r, i: (b, i, r))
    kspec = pl.BlockSpec((None, seq_sub, width), lambda b, r, i: (b, 0, r))
    bias = _attn_bias(dilation, qb, kw, group)
    o, lse = pl.pallas_call(
        functools.partial(_attn_kernel, seq_sub=seq_sub, lq=lq, qb=qb, kw=kw, n_res=n_res),
        grid=(batch, dilation // n_res, seq_sub // lq),
        in_specs=[qspec, kspec, kspec, _full(bias.shape)],
        out_specs=[qspec, qspec],
        out_shape=[jax.ShapeDtypeStruct((batch, seq_sub, dilation * GROUP_WIDTH), BF16),
                   jax.ShapeDtypeStruct((batch, seq_sub, dilation * GROUP_WIDTH), F32)],
        compiler_params=_params(("arbitrary", "arbitrary", "arbitrary")),
        name=f"attn_d{dilation}",
    )(view(q), view(k), view(v), bias)
    rows = batch * seq_sub
    return o.reshape(rows, dilation * GROUP_WIDTH), lse.reshape(rows, dilation * GROUP_WIDTH)


def _merge_kernel(x_ref, p_ref, g1_ref, o0_ref, o1_ref, o2_ref, l0_ref, l1_ref, l2_ref, wba_ref, wo_ref,
                  n2g_ref, wr_ref, x1_ref, h2e_ref, afft_ref, *il_s, tile):
    halves = GROUP_WIDTH // LANES
    dils = [dil for _, dil in DIL_GROUPS] * 2
    grp_refs = (o0_ref, o1_ref, o2_ref, l0_ref, l1_ref, l2_ref)

    for ref, dilation, scratch in zip(grp_refs, dils, il_s):
        for r in range(dilation if dilation > 1 else 0):
            for half in range(halves):
                col = r * GROUP_WIDTH + half * LANES
                scratch[half, pl.ds(r, tile // dilation, stride=dilation), :] = ref[:, col:col + LANES].astype(F32)

    def token_order(n, rs):
        if dils[n] == 1:
            return grp_refs[n][rs, :].astype(F32)
        return jnp.concatenate([il_s[n][half, rs, :] for half in range(halves)], axis=1)

    rows = tile // MERGE_SPLIT
    blocks = [slice(blk * rows, (blk + 1) * rows) for blk in range(MERGE_SPLIT)]

    def mixture(rs):
        o0, o1, o2, l0, l1, l2 = (token_order(n, rs) for n in range(6))
        m = jnp.maximum(jnp.maximum(l0, l1), l2)
        e0, e1, e2 = jnp.exp(l0 - m), jnp.exp(l1 - m), jnp.exp(l2 - m)
        return ((e0 * o0 + e1 * o1 + e2 * o2) / (e0 + e1 + e2)).astype(BF16)

    mixed = [mixture(rs) for rs in blocks]
    y_attn = [_dot(o, wba_ref[...]) for o in mixed]
    merged = [(p_ref[rs, :].astype(F32) + g1_ref[rs, :].astype(F32) * y).astype(BF16) for rs, y in zip(blocks, y_attn)]
    x1s = [x_ref[rs, :] + _dot(mg, wo_ref[...]) for rs, mg in zip(blocks, merged)]
    h2s = []
    for rs, x1 in zip(blocks, x1s):
        x1_ref[rs, :] = x1
        h2 = (x1 * lax.rsqrt(jnp.mean(x1 * x1, axis=-1, keepdims=True) + EPS) * n2g_ref[...]).astype(BF16)
        h2e_ref[rs, :D_MODEL] = h2
        h2s.append(h2)
    logits = [_dot(h2, wr_ref[...]) for h2 in h2s]
    for rs, lg in zip(blocks, logits):
        lane = lax.broadcasted_iota(I32, lg.shape, 1)
        first = lane < N_EXPERTS
        mx = jnp.max(jnp.where(first, lg, -jnp.inf), axis=-1, keepdims=True)
        ex = jnp.exp(lg - mx)
        a = ex / jnp.sum(jnp.where(first, ex, 0.0), axis=-1, keepdims=True)
        afft_ref[:, rs] = a.T[:N_EXPERTS, :]
        hi = a.astype(BF16).astype(F32)
        mid = (a - hi).astype(BF16).astype(F32)
        lo = (a - hi) - mid
        ext = jnp.where(first, hi, jnp.where(lane < 2 * N_EXPERTS, mid, jnp.where(lane < 3 * N_EXPERTS, lo, 0.0)))
        h2e_ref[rs, D_MODEL:] = ext.astype(BF16)


def _merge(x2, p, g1, os_, ls_, w):
    n, d = x2.shape
    tile = PROJ_TILE
    row = lambda width: pl.BlockSpec((tile, width), lambda i: (i, 0))
    grp = [pl.BlockSpec((tile // dil, dil * GROUP_WIDTH), lambda i: (i, 0)) for _, dil in DIL_GROUPS]
    return pl.pallas_call(
        functools.partial(_merge_kernel, tile=tile),
        grid=(n // tile,),
        in_specs=[row(d), row(d), row(d)] + grp + grp
                 + [_full((GROUP_WIDTH, d)), _full((d, d)), _full((1, d)), _full((d, LANES))],
        out_specs=[row(d), row(d + LANES), pl.BlockSpec((N_EXPERTS, tile), lambda i: (0, i))],
        out_shape=[jax.ShapeDtypeStruct((n, d), F32), jax.ShapeDtypeStruct((n, d + LANES), BF16),
                   jax.ShapeDtypeStruct((N_EXPERTS, n), F32)],
        scratch_shapes=[pltpu.VMEM((GROUP_WIDTH // LANES, tile, LANES), F32)] * 6,
        compiler_params=_params(("arbitrary",)),
        name="merge",
    )(x2, p, g1, *os_, *ls_, w["w_br_attn"], w["w_o"], w["norm2_g"], w["w_router3"])


def _select_kernel(aff_ref, slot_ref, starts_ref, nchunk_ref, tot_ref, *, n_tok, cap, tile):
    n_tiles = n_tok // tile

    def bisect(b, thr_bits):
        cand = thr_bits | jnp.left_shift(jnp.int32(1), 30 - b)
        cnt = jnp.sum(jnp.where(aff_ref[...] >= lax.bitcast_convert_type(cand, F32), 1.0, 0.0), axis=1, keepdims=True)
        return jnp.where(cnt >= cap, cand, thr_bits)

    thr = lax.bitcast_convert_type(lax.fori_loop(0, 31, bisect, jnp.zeros((N_EXPERTS, 1), I32)), F32)
    need = cap - jnp.sum(jnp.where(aff_ref[...] > thr, 1.0, 0.0), axis=1, keepdims=True)

    r = lax.broadcasted_iota(I32, (tile, tile), 0)
    c = lax.broadcasted_iota(I32, (tile, tile), 1)
    before = jnp.where(r < c, 1.0, 0.0).astype(BF16)

    def tile_body(i, carry):
        start, eq_seen = carry
        off = pl.multiple_of(i * tile, tile)
        a = aff_ref[:, pl.ds(off, tile)]
        eq = jnp.where(a == thr, 1.0, 0.0)
        eq_rank = eq_seen + _dot(eq.astype(BF16), before)
        sel = jnp.where(a > thr, 1.0, jnp.where(eq_rank < need, eq, 0.0))
        rank = _dot(sel.astype(BF16), before)
        slot_ref[:, pl.ds(off, tile)] = jnp.where(sel > 0.0, rank, -1.0).astype(I32)
        cnt = jnp.sum(sel, axis=1, keepdims=True)
        starts_ref[i] = jnp.broadcast_to(start, (N_EXPERTS, LANES)).astype(I32)
        head = start - jnp.floor(start * (1.0 / SEG_ALIGN)) * SEG_ALIGN
        nch = jnp.max(jnp.floor((head + cnt) * (1.0 / ROUTE_CHUNK)) + 1.0, axis=0, keepdims=True)
        nchunk_ref[i] = jnp.broadcast_to(nch, (8, LANES)).astype(I32)
        return start + cnt, eq_seen + jnp.sum(eq, axis=1, keepdims=True)

    zero = jnp.zeros((N_EXPERTS, 1), F32)
    total, _ = lax.fori_loop(0, n_tiles, tile_body, (zero, zero), unroll=4)
    tot_ref[...] = jnp.broadcast_to(total, (N_EXPERTS, LANES)).astype(I32)


def _select(aff_t, cap):
    n_exp, n = aff_t.shape
    tile = ROUTE_TILE
    n_tiles = n // tile
    slot, starts, nchunk, tot = pl.pallas_call(
        functools.partial(_select_kernel, n_tok=n, cap=cap, tile=tile),
        grid=(1,),
        in_specs=[_full((n_exp, n))],
        out_specs=[_full((n_exp, n)), _full((n_tiles, n_exp, LANES)), _full((n_tiles, 8, LANES)), _full((n_exp, LANES))],
        out_shape=[jax.ShapeDtypeStruct((n_exp, n), I32), jax.ShapeDtypeStruct((n_tiles, n_exp, LANES), I32),
                   jax.ShapeDtypeStruct((n_tiles, 8, LANES), I32), jax.ShapeDtypeStruct((n_exp, LANES), I32)],
        compiler_params=_params(("arbitrary",)),
        name="select",
    )(aff_t)
    return slot, starts[:, :, 0], nchunk[:, 0, 0], tot[:, 0]


def _seg_base(starts_ref, step, e):
    return pl.multiple_of((starts_ref[step * N_EXPERTS + e] // SEG_ALIGN) * SEG_ALIGN, SEG_ALIGN)


def _dispatch_kernel(starts_ref, nchunk_ref, slot_ref, h_ref, scol_ref, xe_hbm, stage, head, sem, *, tile, chunk):
    i = pl.program_id(0)
    sl = i % 2

    def copies(step, c, buf):
        return [pltpu.make_async_copy(
            stage.at[buf, pl.ds(e * chunk, chunk), :],
            xe_hbm.at[e, pl.ds(_seg_base(starts_ref, step, e) + c * chunk, chunk), :],
            sem.at[buf]) for e in range(N_EXPERTS)]

    @pl.when(i == 0)
    def _():
        head[...] = jnp.zeros(head.shape, BF16)

    slot = slot_ref[...]
    pos = jnp.where(slot >= 0, slot + (scol_ref[...] & (SEG_ALIGN - 1)), -1)

    def build(c, buf):
        want = lax.broadcasted_iota(I32, (chunk, tile), 0) + c * chunk
        onehot = jnp.concatenate(
            [jnp.where(pos[e:e + 1, :] == want, 1.0, 0.0).astype(BF16) for e in range(N_EXPERTS)], axis=0)
        stage[buf] = _dot(onehot, h_ref[...]).astype(BF16)

    nxt = jnp.minimum(i + 1, pl.num_programs(0) - 1)
    rel = [starts_ref[nxt * N_EXPERTS + e] - _seg_base(starts_ref, i, e) for e in range(N_EXPERTS)]

    def keep_head(e, buf):
        row = pl.multiple_of(e * chunk + ((rel[e] % chunk) // SEG_ALIGN) * SEG_ALIGN, SEG_ALIGN)
        head[e] = stage[buf, pl.ds(row, SEG_ALIGN), :]

    build(0, sl)
    for e in range(N_EXPERTS):
        stage[sl, e * chunk:e * chunk + SEG_ALIGN, :] = stage[sl, e * chunk:e * chunk + SEG_ALIGN, :] + head[e]
    for e in range(N_EXPERTS):
        keep_head(e, sl)

    def wait_batch(buf):
        for _ in range(N_EXPERTS):
            pltpu.make_async_copy(stage.at[buf, pl.ds(0, chunk), :], xe_hbm.at[0, pl.ds(0, chunk), :], sem.at[buf]).wait()

    @pl.when(i > 0)
    def _():
        wait_batch(1 - sl)

    for cp in copies(i, 0, sl):
        cp.start()

    def overflow(c, carry):
        wait_batch(sl)
        build(c, sl)
        for e in range(N_EXPERTS):
            @pl.when(rel[e] // chunk == c)
            def _():
                keep_head(e, sl)
        for cp in copies(i, c, sl):
            cp.start()
        return carry

    lax.fori_loop(1, nchunk_ref[i], overflow, 0)

    @pl.when(i == pl.num_programs(0) - 1)
    def _():
        wait_batch(sl)
        rows = xe_hbm.shape[1]
        zbuf = 1 - sl
        stage[zbuf] = jnp.zeros(stage.shape[1:], BF16)
        end = [_seg_base(starts_ref, i, e) + jnp.maximum(nchunk_ref[i], 1) * chunk for e in range(N_EXPERTS)]
        n_full = [(rows - end[e]) // chunk for e in range(N_EXPERTS)]

        def zero_copy(e, pos):
            return pltpu.make_async_copy(stage.at[zbuf, pl.ds(e * chunk, chunk), :],
                                         xe_hbm.at[e, pl.ds(pl.multiple_of(pos, SEG_ALIGN), chunk), :], sem.at[zbuf])

        def fill(k, carry):
            for e in range(N_EXPERTS):
                @pl.when(k < n_full[e])
                def _():
                    zero_copy(e, end[e] + k * chunk).start()
            for e in range(N_EXPERTS):
                @pl.when(k < n_full[e])
                def _():
                    zero_copy(e, end[e] + k * chunk).wait()
            return carry

        lax.fori_loop(0, functools.reduce(jnp.maximum, n_full), fill, 0)
        for e in range(N_EXPERTS):
            zero_copy(e, rows - chunk).start()
        for e in range(N_EXPERTS):
            zero_copy(e, rows - chunk).wait()


def _dispatch(slot, h2e, starts, nchunk, rows):
    n = h2e.shape[0]
    tile, chunk = ROUTE_TILE, ROUTE_CHUNK
    n_tiles = n // tile
    return pl.pallas_call(
        functools.partial(_dispatch_kernel, tile=tile, chunk=chunk),
        grid_spec=pltpu.PrefetchScalarGridSpec(
            num_scalar_prefetch=2, grid=(n_tiles,),
            in_specs=[pl.BlockSpec((N_EXPERTS, tile), lambda i, *_: (0, i)),
                      pl.BlockSpec((tile, D_MODEL + LANES), lambda i, *_: (i, 0)),
                      pl.BlockSpec((None, N_EXPERTS, 1), lambda i, *_: (i, 0, 0))],
            out_specs=pl.BlockSpec(memory_space=pl.ANY),
            scratch_shapes=[pltpu.VMEM((2, N_EXPERTS * chunk, XE_W), BF16),
                            pltpu.VMEM((N_EXPERTS, SEG_ALIGN, XE_W), BF16), pltpu.SemaphoreType.DMA((2,))]),
        out_shape=jax.ShapeDtypeStruct((N_EXPERTS, rows, XE_W), BF16),
        compiler_params=_params(("arbitrary",)),
        name="dispatch",
    )(starts.reshape(-1), nchunk, slot, h2e, starts.reshape(n_tiles, N_EXPERTS, 1))


def _ffn_kernel(tot_ref, *refs, tile, first_tile):
    n_grp = len(first_tile) - 1
    xe_refs, (wg_ref, wu_ref, wd_ref) = refs[:n_grp], refs[n_grp:n_grp + 3]
    ye_refs, (wg_s, wu_s, wd_s) = refs[n_grp + 3:2 * n_grp + 3], refs[2 * n_grp + 3:]
    e = pl.program_id(0)
    j = pl.program_id(1)

    @pl.when(j == 0)
    def _():
        wg_s[...] = wg_ref[...].astype(BF16)
        wu_s[...] = wu_ref[...].astype(BF16)
        wd_s[...] = wd_ref[...].astype(BF16)

    def run(xe_ref, ye_ref, live):
        @pl.when(live > 0)
        def _():
            x = xe_ref[:, :D_MODEL]
            pieces = xe_ref[:, D_MODEL:].astype(F32)
            lane = lax.broadcasted_iota(I32, pieces.shape, 1)
            mine = ((lane & (N_EXPERTS - 1)) == e) & (lane < 3 * N_EXPERTS)
            gate = jnp.sum(jnp.where(mine, pieces, 0.0), axis=-1, keepdims=True)
            hid = []
            for fc in range(wg_s.shape[1] // FFN_COLS):
                cs = slice(fc * FFN_COLS, (fc + 1) * FFN_COLS)
                g = _dot(x, wg_s[:, cs])
                u = _dot(x, wu_s[:, cs])
                hid.append((g * jax.nn.sigmoid(g) * u).astype(BF16))
            y = _dot(jnp.concatenate(hid, axis=1), wd_s[...]) * gate
            rows = lax.broadcasted_iota(I32, y.shape, 0)
            ye_ref[...] = jnp.where(rows < live, y, 0.0).astype(BF16)

        @pl.when(live <= 0)
        def _():
            ye_ref[...] = jnp.zeros(ye_ref.shape, BF16)

    for g in range(n_grp):
        @pl.when((j >= first_tile[g]) & (j < first_tile[g + 1]))
        def _():
            run(xe_refs[g], ye_refs[g], tot_ref[g * N_EXPERTS + e] - (j - first_tile[g]) * tile)


def _ffn(xes, tots, w_gate, w_up, w_down):
    tile = FFN_TILE
    n_exp, d, f = w_gate.shape
    n_tiles = [xe.shape[1] // tile for xe in xes]
    first_tile = [sum(n_tiles[:g]) for g in range(len(xes) + 1)]

    def xe_spec(g):
        def index(e, j, tot):
            last = jnp.maximum(tot[g * N_EXPERTS + e] - 1, 0) // tile
            return (e, jnp.clip(j - first_tile[g], 0, last), 0)
        return pl.BlockSpec((None, tile, XE_W), index)

    def ye_spec(g):
        return pl.BlockSpec((None, tile, d), lambda e, j, tot: (e, jnp.clip(j - first_tile[g], 0, n_tiles[g] - 1), 0))

    wspec = lambda a, b: pl.BlockSpec((None, a, b), lambda e, j, tot: (e, 0, 0))
    return pl.pallas_call(
        functools.partial(_ffn_kernel, tile=tile, first_tile=tuple(first_tile)),
        grid_spec=pltpu.PrefetchScalarGridSpec(
            num_scalar_prefetch=1, grid=(n_exp, first_tile[-1]),
            in_specs=[xe_spec(g) for g in range(len(xes))] + [wspec(d, f), wspec(d, f), wspec(f, d)],
            out_specs=[ye_spec(g) for g in range(len(xes))],
            scratch_shapes=[pltpu.VMEM((d, f), BF16), pltpu.VMEM((d, f), BF16), pltpu.VMEM((f, d), BF16)]),
        out_shape=[jax.ShapeDtypeStruct((n_exp, xe.shape[1], d), BF16) for xe in xes],
        compiler_params=_params(("arbitrary", "arbitrary")),
        name="ffn",
    )(jnp.concatenate(tots), *xes, w_gate, w_up, w_down)


def _combine_kernel(starts_ref, nchunk_ref, x1_ref, slot_ref, srow_ref, ye_hbm, out_ref, stage, sem, *, tile, chunk):
    i = pl.program_id(0)
    sl = i % 2
    over = 2

    def copies(step, c, buf):
        return [pltpu.make_async_copy(
            ye_hbm.at[e, pl.ds(_seg_base(starts_ref, step, e) + c * chunk, chunk), :],
            stage.at[buf, pl.ds(e * chunk, chunk), :],
            sem.at[buf]) for e in range(N_EXPERTS)]

    @pl.when(i == 0)
    def _():
        for cp in copies(0, 0, 0):
            cp.start()

    @pl.when(i + 1 < pl.num_programs(0))
    def _():
        for cp in copies(i + 1, 0, 1 - sl):
            cp.start()

    col = lax.broadcasted_iota(I32, (N_EXPERTS, N_EXPERTS * chunk), 1)
    exp_row = lax.broadcasted_iota(I32, (N_EXPERTS, N_EXPERTS * chunk), 0)
    spread = jnp.where(col // chunk == exp_row, 1.0, 0.0).astype(BF16)
    slot = slot_ref[...]
    rank = jnp.where(slot < 0, -float(tile), slot.astype(F32)).astype(BF16)
    head = jnp.broadcast_to((srow_ref[...] & (SEG_ALIGN - 1)).astype(F32).astype(BF16), (8, N_EXPERTS))
    slots = _dot(rank, spread) + _dot(head, spread)[0:1]
    lane_slot = (lax.broadcasted_iota(I32, slots.shape, 1) & (chunk - 1)).astype(F32)

    def gathered(c, buf):
        onehot = jnp.where(slots == lane_slot + c * chunk, 1.0, 0.0).astype(BF16)
        return _dot(onehot, stage[buf])

    def wait_batch(buf):
        for _ in range(N_EXPERTS):
            pltpu.make_async_copy(ye_hbm.at[0, pl.ds(0, chunk), :], stage.at[buf, pl.ds(0, chunk), :], sem.at[buf]).wait()

    wait_batch(sl)
    out_ref[...] = x1_ref[...] + gathered(0, sl)

    def overflow(c, carry):
        for cp in copies(i, c, over):
            cp.start()
        wait_batch(over)
        out_ref[...] += gathered(c, over)
        return carry

    lax.fori_loop(1, nchunk_ref[i], overflow, 0)


def _combine(x1, slot_t, ye, starts, nchunk):
    n, d = x1.shape
    tile, chunk = ROUTE_TILE, ROUTE_CHUNK
    n_tiles = n // tile
    return pl.pallas_call(
        functools.partial(_combine_kernel, tile=tile, chunk=chunk),
        grid_spec=pltpu.PrefetchScalarGridSpec(
            num_scalar_prefetch=2, grid=(n_tiles,),
            in_specs=[pl.BlockSpec((tile, d), lambda i, *_: (i, 0)),
                      pl.BlockSpec((tile, N_EXPERTS), lambda i, *_: (i, 0)),
                      pl.BlockSpec((None, 1, N_EXPERTS), lambda i, *_: (i, 0, 0)),
                      pl.BlockSpec(memory_space=pl.ANY)],
            out_specs=pl.BlockSpec((tile, d), lambda i, *_: (i, 0)),
            scratch_shapes=[pltpu.VMEM((3, N_EXPERTS * chunk, d), BF16), pltpu.SemaphoreType.DMA((3,))]),
        out_shape=jax.ShapeDtypeStruct((n, d), F32),
        compiler_params=_params(("arbitrary",)),
        name="combine",
    )(starts.reshape(-1), nchunk, x1, slot_t, starts.reshape(n_tiles, 1, N_EXPERTS), ye)


def _expert_rows(cap):
    need = cap + ROUTE_TILE + 2 * ROUTE_CHUNK
    return -(-need // FFN_TILE) * FFN_TILE


def _routed_ffn(groups, w):
    routes = []
    for x1, h2e, aff_t in groups:
        cap = max(1, EC_CAPACITY * x1.shape[0] // N_EXPERTS)
        slot, starts, nchunk, tot = _select(aff_t, cap)
        routes.append((slot, starts, nchunk, tot, _dispatch(slot, h2e, starts, nchunk, _expert_rows(cap))))
    yes = _ffn([r[4] for r in routes], [r[3] for r in routes], w["w_exp_gate"], w["w_exp_up"], w["w_exp_down"])
    return [_combine(x1, slot.T, ye, starts, nchunk)
            for (x1, _, _), (slot, starts, nchunk, _, _), ye in zip(groups, routes, yes)]


def _mixers(x, mem, w):
    b, s, d = x.shape
    x2 = x.reshape(b * s, d)
    kx, vx = _memkv(mem, w["mem_norm_g"], w["w_mem_kv"], w["xk_gain"])
    *qkv, p, g1 = _proj(x2, s, kx, vx, w)
    os_, ls_ = [], []
    for g in range(len(DIL_GROUPS)):
        o, lse = _attn(qkv[g], qkv[3 + g], qkv[6 + g], b, s, g)
        os_.append(o)
        ls_.append(lse)
    return _merge(x2, p, g1, os_, ls_, w)


def _layer(xs, mems, w):
    outs = _routed_ffn([_mixers(x, mem, w) for x, mem in zip(xs, mems)], w)
    return [o.reshape(x.shape) for o, x in zip(outs, xs)]


def _prepare(norm1_g, w_in, conv_w, q_norm_g, k_norm_g, mem_norm_g, w_mem_kv, xq_norm_g, xk_norm_g,
             w_br_conv, w_br_attn, w_br_xattn, w_o, norm2_g, w_router, w_exp_gate, w_exp_up, w_exp_down):
    row = lambda v: v.reshape(1, -1).astype(F32)
    head = jnp.arange(GROUP_WIDTH) // ATT_HEAD_DIM
    return {
        "norm1_g": row(norm1_g), "norm2_g": row(norm2_g), "mem_norm_g": row(mem_norm_g),
        "w_in": w_in.astype(BF16), "conv_w": conv_w.astype(F32),
        "q_gain": row(jnp.tile(q_norm_g, HEADS_PER_GROUP) * (ATT_HEAD_DIM ** -0.5)),
        "k_gain": row(jnp.tile(k_norm_g, HEADS_PER_GROUP)),
        "xq_gain": row(xq_norm_g), "xk_gain": row(xk_norm_g),
        "w_mem_kv": w_mem_kv.astype(BF16), "w_br_conv": w_br_conv.astype(BF16),
        "w_br_attn": w_br_attn.astype(BF16), "w_br_xattn": w_br_xattn.astype(BF16), "w_o": w_o.astype(BF16),
        "head_avg": jnp.where(head[:, None] == head[None, :], 1.0 / ATT_HEAD_DIM, 0.0).astype(BF16),
        "w_router3": jnp.concatenate([w_router] * 3 + [jnp.zeros((D_MODEL, LANES - 3 * N_EXPERTS), F32)],
                                     axis=1).astype(BF16),
        "w_exp_gate": w_exp_gate, "w_exp_up": w_exp_up, "w_exp_down": w_exp_down,
    }


def kernel(x_prompt, x_sample, mem_prompt, mem_sample, norm1_g, w_in, conv_w, q_norm_g, k_norm_g, mem_norm_g, w_mem_kv, xq_norm_g, xk_norm_g, w_br_conv, w_br_attn, w_br_xattn, w_o, norm2_g, w_router, w_exp_gate, w_exp_up, w_exp_down):
    per_layer = (norm1_g, w_in, conv_w, q_norm_g, k_norm_g, mem_norm_g, w_mem_kv, xq_norm_g, xk_norm_g,
                 w_br_conv, w_br_attn, w_br_xattn, w_o, norm2_g, w_router, w_exp_gate, w_exp_up, w_exp_down)
    ys = [x_prompt, x_sample]
    for layer in range(norm1_g.shape[0]):
        ys = _layer(ys, (mem_prompt, mem_sample), _prepare(*(t[layer] for t in per_layer)))
    return tuple(ys)
```

```python
import functools

import jax
import jax.numpy as jnp
from jax import lax
from jax.experimental import pallas as pl
from jax.experimental.pallas import tpu as pltpu

F32 = jnp.float32
BF16 = jnp.bfloat16
I32 = jnp.int32

D_MODEL = 1024
N_MEM = 256
CONV_WIDTH = 768
ATT_HEAD_DIM = 64
DIL_GROUPS = ((128, 1), (512, 4), (2048, 16))
HEADS_PER_GROUP = 4
N_ATT_HEADS = HEADS_PER_GROUP * len(DIL_GROUPS)
ATT_WIDTH = N_ATT_HEADS * ATT_HEAD_DIM
GROUP_WIDTH = HEADS_PER_GROUP * ATT_HEAD_DIM
ATT_RADIUS = 64
XATT_HEADS = 4
XATT_HEAD_DIM = 128
XATT_WIDTH = XATT_HEADS * XATT_HEAD_DIM
N_EXPERTS = 16
EC_CAPACITY = 2
ALIBI_MAX_EXP = 8.0
EPS = 1e-6
NEG_INF = -1e30

C_CB, C_CC, C_AQ, C_XQ, C_GATE, C_END = 0, 768, 2304, 4608, 5120, 8192

V7X_VMEM_LIMIT_BYTES = 56 * 1024 * 1024
LANES = 128
HALO = 16

PROJ_TILE = 512
MERGE_SPLIT = 2
ATT_QB = 128
ATT_UNITS = 4
ROUTE_TILE = 256
ROUTE_CHUNK = 64
SEG_ALIGN = 16
FFN_TILE = 1024
FFN_COLS = 256
XE_W = D_MODEL + LANES


def _dot(a, b):
    return jnp.dot(a, b, preferred_element_type=F32)


def _dot_nt(a, b):
    return lax.dot_general(a, b, (((1,), (1,)), ((), ())), preferred_element_type=F32)


def _params(sem):
    return pltpu.CompilerParams(dimension_semantics=sem, vmem_limit_bytes=V7X_VMEM_LIMIT_BYTES)


def _full(shape):
    return pl.BlockSpec(shape, lambda *_: (0,) * len(shape))


def _resident(shape):
    return pl.BlockSpec(shape, lambda *_: (0,) * len(shape), pipeline_mode=pl.Buffered(1))


def _memkv_kernel(mem_ref, g_ref, w_ref, kg_ref, k_ref, v_ref):
    m = mem_ref[...]
    hn = (m * lax.rsqrt(jnp.mean(m * m, axis=-1, keepdims=True) + EPS) * g_ref[...]).astype(BF16)
    kv = _dot(hn, w_ref[...])
    ks = []
    for h in range(XATT_HEADS):
        kh = kv[:, h * XATT_HEAD_DIM:(h + 1) * XATT_HEAD_DIM]
        ks.append(kh * lax.rsqrt(jnp.mean(kh * kh, axis=-1, keepdims=True) + EPS) * kg_ref[...])
    k_ref[...] = jnp.concatenate(ks, axis=1).astype(BF16)
    v_ref[...] = kv[:, XATT_WIDTH:].astype(BF16)


def _memkv(mem, mem_g, w_mem_kv, xk_g):
    b, m, d = mem.shape
    out = jax.ShapeDtypeStruct((b, m, XATT_WIDTH), BF16)
    return pl.pallas_call(
        _memkv_kernel,
        grid=(b,),
        in_specs=[pl.BlockSpec((None, m, d), lambda i: (i, 0, 0)), _full((1, d)),
                  _full((d, 2 * XATT_WIDTH)), _full((1, XATT_HEAD_DIM))],
        out_specs=[pl.BlockSpec((None, m, XATT_WIDTH), lambda i: (i, 0, 0))] * 2,
        out_shape=[out, out],
        compiler_params=_params(("arbitrary",)),
        name="memkv",
    )(mem, mem_g, w_mem_kv, xk_g)


def _proj_kernel(xp_ref, x_ref, xn_ref, n1g_ref, win_ref, cw_ref, qg_ref, kg_ref, xqg_ref, kx_ref, vx_ref,
                 wbc_ref, wbx_ref, bd_ref,
                 q0_ref, q1_ref, q2_ref, k0_ref, k1_ref, k2_ref, v0_ref, v1_ref, v2_ref, p_ref, g1_ref,
                 hb_s, u_s, il_s, *, tile, tiles_per_seq):
    tin = pl.program_id(0) % tiles_per_seq
    gain = n1g_ref[...]

    def nrm(x):
        return (x * lax.rsqrt(jnp.mean(x * x, axis=-1, keepdims=True) + EPS) * gain).astype(BF16)

    hb_s[0:HALO, :] = nrm(xp_ref[...])
    hb_s[HALO:HALO + tile, :] = nrm(x_ref[...])
    hb_s[HALO + tile:, :] = nrm(xn_ref[...])
    hc = hb_s[HALO:HALO + tile, :]

    ccx = _dot(hb_s[...], win_ref[:, C_CC:C_AQ])
    u = ccx[:, :CONV_WIDTH] * ccx[:, CONV_WIDTH:]
    u_s[0:HALO, :] = u[0:HALO] * jnp.where(tin == 0, 0.0, 1.0)
    u_s[HALO:HALO + tile, :] = u[HALO:HALO + tile]
    u_s[HALO + tile:, :] = u[HALO + tile:] * jnp.where(tin == tiles_per_seq - 1, 0.0, 1.0)
    cw = cw_ref[...]
    conv = (cw[0:1] * u_s[HALO - 1:HALO - 1 + tile, :] + cw[1:2] * u_s[HALO:HALO + tile, :]
            + cw[2:3] * u_s[HALO + 1:HALO + 1 + tile, :])
    cb = _dot(hc, win_ref[:, C_CB:C_CC])
    y_conv = _dot((cb * conv).astype(BF16), wbc_ref[...])

    qkv = _dot(hc, win_ref[:, C_AQ:C_XQ])
    bd = bd_ref[...]

    def head_norm(z, g_ref):
        outs = []
        for c in range(len(DIL_GROUPS)):
            zc = z[:, c * GROUP_WIDTH:(c + 1) * GROUP_WIDTH]
            ms = _dot((zc * zc).astype(BF16), bd)
            outs.append(zc * lax.rsqrt(ms + EPS) * g_ref[...])
        return outs

    def emit(ref, val, dilation):
        if dilation == 1:
            ref[...] = val.astype(BF16)
            return
        for half in range(GROUP_WIDTH // LANES):
            il_s[half] = val[:, half * LANES:(half + 1) * LANES]
        for r in range(dilation):
            for half in range(GROUP_WIDTH // LANES):
                col = r * GROUP_WIDTH + half * LANES
                ref[:, col:col + LANES] = il_s[half, pl.ds(r, tile // dilation, stride=dilation), :].astype(BF16)

    vals = (head_norm(qkv[:, :ATT_WIDTH], qg_ref) + head_norm(qkv[:, ATT_WIDTH:2 * ATT_WIDTH], kg_ref)
            + [qkv[:, 2 * ATT_WIDTH + c * GROUP_WIDTH:2 * ATT_WIDTH + (c + 1) * GROUP_WIDTH] for c in range(len(DIL_GROUPS))])
    refs = (q0_ref, q1_ref, q2_ref, k0_ref, k1_ref, k2_ref, v0_ref, v1_ref, v2_ref)
    for n, (ref, val) in enumerate(zip(refs, vals)):
        emit(ref, val, DIL_GROUPS[n % len(DIL_GROUPS)][1])

    xq = _dot(hc, win_ref[:, C_XQ:C_GATE])
    kx = kx_ref[...]
    vx = vx_ref[...]
    outs = []
    for h in range(XATT_HEADS):
        hs = slice(h * XATT_HEAD_DIM, (h + 1) * XATT_HEAD_DIM)
        qh = xq[:, hs]
        qh = qh * lax.rsqrt(jnp.mean(qh * qh, axis=-1, keepdims=True) + EPS) * xqg_ref[...]
        s = _dot_nt(qh.astype(BF16), kx[:, hs]) * (XATT_HEAD_DIM ** -0.5)
        p = jnp.exp(s - jnp.max(s, axis=-1, keepdims=True))
        den = jnp.sum(p, axis=-1, keepdims=True)
        outs.append(_dot(p.astype(BF16), vx[:, hs]) / den)
    y_x = _dot(jnp.concatenate(outs, axis=1).astype(BF16), wbx_ref[...])

    gs = jax.nn.sigmoid(_dot(hc, win_ref[:, C_GATE:C_END]))
    p_ref[...] = (gs[:, :D_MODEL] * y_conv + gs[:, 2 * D_MODEL:] * y_x).astype(BF16)
    g1_ref[...] = gs[:, D_MODEL:2 * D_MODEL].astype(BF16)


def _proj(x2, seq_len, kx, vx, w):
    n, d = x2.shape
    tile = PROJ_TILE
    tps = seq_len // tile
    hb = tile // HALO
    n_hblk = n // HALO
    wide = jax.ShapeDtypeStruct((n, d), BF16)
    row = lambda width: pl.BlockSpec((tile, width), lambda i: (i, 0))
    dils = [dil for _, dil in DIL_GROUPS] * 3
    grp_shapes = [jax.ShapeDtypeStruct((n // dil, dil * GROUP_WIDTH), BF16) for dil in dils]
    grp_specs = [pl.BlockSpec((tile // dil, dil * GROUP_WIDTH), lambda i: (i, 0)) for dil in dils]
    return pl.pallas_call(
        functools.partial(_proj_kernel, tile=tile, tiles_per_seq=tps),
        grid=(n // tile,),
        in_specs=[
            pl.BlockSpec((HALO, d), lambda i: (jnp.maximum(i * hb - 1, 0), 0)),
            row(d),
            pl.BlockSpec((HALO, d), lambda i: (jnp.minimum((i + 1) * hb, n_hblk - 1), 0)),
            _full((1, d)), _resident((d, C_END)), _full((3, CONV_WIDTH)),
            _full((1, GROUP_WIDTH)), _full((1, GROUP_WIDTH)), _full((1, XATT_HEAD_DIM)),
            pl.BlockSpec((None, N_MEM, XATT_WIDTH), lambda i: (i // tps, 0, 0)),
            pl.BlockSpec((None, N_MEM, XATT_WIDTH), lambda i: (i // tps, 0, 0)),
            _resident((CONV_WIDTH, d)), _resident((XATT_WIDTH, d)), _full((GROUP_WIDTH, GROUP_WIDTH)),
        ],
        out_specs=grp_specs + [row(d), row(d)],
        out_shape=grp_shapes + [wide, wide],
        scratch_shapes=[pltpu.VMEM((tile + 2 * HALO, d), BF16), pltpu.VMEM((tile + 2 * HALO, CONV_WIDTH), F32),
                        pltpu.VMEM((GROUP_WIDTH // LANES, tile, LANES), F32)],
        compiler_params=_params(("arbitrary",)),
        name="proj",
    )(x2, x2, x2, w["norm1_g"], w["w_in"], w["conv_w"], w["q_gain"], w["k_gain"], w["xq_gain"], kx, vx,
      w["w_br_conv"], w["w_br_xattn"], w["head_avg"])


def _attn_kernel(q_ref, k_ref, v_ref, bias_ref, o_ref, lse_ref, *, seq_sub, lq, qb, kw, n_res):
    i = pl.program_id(2)
    lane_head = lax.broadcasted_iota(I32, (qb, GROUP_WIDTH), 1) // ATT_HEAD_DIM
    head_mask = [lane_head == h for h in range(HEADS_PER_GROUP)]
    head_mask_bf = [jnp.where(m, 1.0, 0.0).astype(BF16) for m in head_mask]

    def block(sb, carry):
        row0 = pl.multiple_of(sb * qb, qb)
        qs = i * lq + row0
        ks = pl.multiple_of(jnp.clip(qs - ATT_RADIUS, 0, seq_sub - kw), ATT_RADIUS)
        case = jnp.where(qs == 0, 0, jnp.where(qs == seq_sub - qb, 2, 1))
        for r in range(n_res):
            cs = slice(r * GROUP_WIDTH, (r + 1) * GROUP_WIDTH)
            q = q_ref[pl.ds(row0, qb), cs]
            kk = k_ref[pl.ds(ks, kw), cs]
            vv = v_ref[pl.ds(ks, kw), cs]
            s = _dot_nt(jnp.concatenate([q * m for m in head_mask_bf], axis=0), kk)
            ps, inv_den, lse = [], [], []
            for h in range(HEADS_PER_GROUP):
                sh = s[h * qb:(h + 1) * qb] + bias_ref[case * HEADS_PER_GROUP + h]
                m = jnp.max(sh, axis=-1, keepdims=True)
                p = jnp.exp(sh - m)
                den = jnp.sum(p, axis=-1, keepdims=True)
                ps.append(p.astype(BF16))
                inv_den.append(1.0 / den)
                lse.append(m + jnp.log(den))
            of = _dot(jnp.concatenate(ps, axis=0), vv)
            o = jnp.zeros((qb, GROUP_WIDTH), F32)
            lb = jnp.zeros((qb, GROUP_WIDTH), F32)
            for h in range(HEADS_PER_GROUP):
                o = jnp.where(head_mask[h], of[h * qb:(h + 1) * qb] * inv_den[h], o)
                lb = jnp.where(head_mask[h], lse[h], lb)
            o_ref[pl.ds(row0, qb), cs] = o.astype(BF16)
            lse_ref[pl.ds(row0, qb), cs] = lb
        return carry

    lax.fori_loop(0, lq // qb, block, 0, unroll=max(1, min(lq // qb, ATT_UNITS // n_res)))


def _attn_bias(dilation, qb, kw, group):
    slopes = jnp.exp2(-ALIBI_MAX_EXP * jnp.arange(1, N_ATT_HEADS + 1, dtype=F32) / N_ATT_HEADS)
    slopes = slopes[group * HEADS_PER_GROUP:(group + 1) * HEADS_PER_GROUP]
    a = jnp.arange(qb)[:, None]
    c = jnp.arange(kw)[None, :]
    tabs = []
    for off in (0, -ATT_RADIUS, qb - kw):
        delta = off + c - a
        dist = (dilation * jnp.abs(delta)).astype(F32)
        bias = -slopes[:, None, None] * dist[None]
        tabs.append(jnp.where((jnp.abs(delta) <= ATT_RADIUS)[None], bias, NEG_INF))
    return jnp.concatenate(tabs, axis=0)


def _attn(q, k, v, batch, seq_len, group):
    _, dilation = DIL_GROUPS[group]
    seq_sub = seq_len // dilation
    qb = min(ATT_QB, seq_sub)
    kw = min(qb + 2 * ATT_RADIUS, seq_sub)
    lq = min(4 * qb, seq_sub)
    n_res = min(dilation, 4)
    width = n_res * GROUP_WIDTH
    view = lambda t: t.reshape(batch, seq_sub, dilation * GROUP_WIDTH)
    qspec = pl.BlockSpec((None, lq, width), lambda b, r, i: (b, i, r))
    kspec = pl.BlockSpec((None, seq_sub, width), lambda b, r, i: (b, 0, r))
    bias = _attn_bias(dilation, qb, kw, group)
    o, lse = pl.pallas_call(
        functools.partial(_attn_kernel, seq_sub=seq_sub, lq=lq, qb=qb, kw=kw, n_res=n_res),
        grid=(batch, dilation // n_res, seq_sub // lq),
        in_specs=[qspec, kspec, kspec, _full(bias.shape)],
        out_specs=[qspec, qspec],
        out_shape=[jax.ShapeDtypeStruct((batch, seq_sub, dilation * GROUP_WIDTH), BF16),
                   jax.ShapeDtypeStruct((batch, seq_sub, dilation * GROUP_WIDTH), F32)],
        compiler_params=_params(("arbitrary", "arbitrary", "arbitrary")),
        name=f"attn_d{dilation}",
    )(view(q), view(k), view(v), bias)
    rows = batch * seq_sub
    return o.reshape(rows, dilation * GROUP_WIDTH), lse.reshape(rows, dilation * GROUP_WIDTH)


def _merge_kernel(x_ref, p_ref, g1_ref, o0_ref, o1_ref, o2_ref, l0_ref, l1_ref, l2_ref, wba_ref, wo_ref,
                  n2g_ref, wr_ref, x1_ref, h2e_ref, afft_ref, *il_s, tile):
    halves = GROUP_WIDTH // LANES
    dils = [dil for _, dil in DIL_GROUPS] * 2
    grp_refs = (o0_ref, o1_ref, o2_ref, l0_ref, l1_ref, l2_ref)

    for ref, dilation, scratch in zip(grp_refs, dils, il_s):
        for r in range(dilation if dilation > 1 else 0):
            for half in range(halves):
                col = r * GROUP_WIDTH + half * LANES
                scratch[half, pl.ds(r, tile // dilation, stride=dilation), :] = ref[:, col:col + LANES].astype(F32)

    def token_order(n, rs):
        if dils[n] == 1:
            return grp_refs[n][rs, :].astype(F32)
        return jnp.concatenate([il_s[n][half, rs, :] for half in range(halves)], axis=1)

    rows = tile // MERGE_SPLIT
    blocks = [slice(blk * rows, (blk + 1) * rows) for blk in range(MERGE_SPLIT)]

    def mixture(rs):
        o0, o1, o2, l0, l1, l2 = (token_order(n, rs) for n in range(6))
        m = jnp.maximum(jnp.maximum(l0, l1), l2)
        e0, e1, e2 = jnp.exp(l0 - m), jnp.exp(l1 - m), jnp.exp(l2 - m)
        return ((e0 * o0 + e1 * o1 + e2 * o2) / (e0 + e1 + e2)).astype(BF16)

    mixed = [mixture(rs) for rs in blocks]
    y_attn = [_dot(o, wba_ref[...]) for o in mixed]
    merged = [(p_ref[rs, :].astype(F32) + g1_ref[rs, :].astype(F32) * y).astype(BF16) for rs, y in zip(blocks, y_attn)]
    x1s = [x_ref[rs, :] + _dot(mg, wo_ref[...]) for rs, mg in zip(blocks, merged)]
    h2s = []
    for rs, x1 in zip(blocks, x1s):
        x1_ref[rs, :] = x1
        h2 = (x1 * lax.rsqrt(jnp.mean(x1 * x1, axis=-1, keepdims=True) + EPS) * n2g_ref[...]).astype(BF16)
        h2e_ref[rs, :D_MODEL] = h2
        h2s.append(h2)
    logits = [_dot(h2, wr_ref[...]) for h2 in h2s]
    for rs, lg in zip(blocks, logits):
        lane = lax.broadcasted_iota(I32, lg.shape, 1)
        first = lane < N_EXPERTS
        mx = jnp.max(jnp.where(first, lg, -jnp.inf), axis=-1, keepdims=True)
        ex = jnp.exp(lg - mx)
        a = ex / jnp.sum(jnp.where(first, ex, 0.0), axis=-1, keepdims=True)
        afft_ref[:, rs] = a.T[:N_EXPERTS, :]
        hi = a.astype(BF16).astype(F32)
        mid = (a - hi).astype(BF16).astype(F32)
        lo = (a - hi) - mid
        ext = jnp.where(first, hi, jnp.where(lane < 2 * N_EXPERTS, mid, jnp.where(lane < 3 * N_EXPERTS, lo, 0.0)))
        h2e_ref[rs, D_MODEL:] = ext.astype(BF16)


def _merge(x2, p, g1, os_, ls_, w):
    n, d = x2.shape
    tile = PROJ_TILE
    row = lambda width: pl.BlockSpec((tile, width), lambda i: (i, 0))
    grp = [pl.BlockSpec((tile // dil, dil * GROUP_WIDTH), lambda i: (i, 0)) for _, dil in DIL_GROUPS]
    return pl.pallas_call(
        functools.partial(_merge_kernel, tile=tile),
        grid=(n // tile,),
        in_specs=[row(d), row(d), row(d)] + grp + grp
                 + [_full((GROUP_WIDTH, d)), _full((d, d)), _full((1, d)), _full((d, LANES))],
        out_specs=[row(d), row(d + LANES), pl.BlockSpec((N_EXPERTS, tile), lambda i: (0, i))],
        out_shape=[jax.ShapeDtypeStruct((n, d), F32), jax.ShapeDtypeStruct((n, d + LANES), BF16),
                   jax.ShapeDtypeStruct((N_EXPERTS, n), F32)],
        scratch_shapes=[pltpu.VMEM((GROUP_WIDTH // LANES, tile, LANES), F32)] * 6,
        compiler_params=_params(("arbitrary",)),
        name="merge",
    )(x2, p, g1, *os_, *ls_, w["w_br_attn"], w["w_o"], w["norm2_g"], w["w_router3"])


def _select_kernel(aff_ref, slot_ref, starts_ref, nchunk_ref, tot_ref, *, n_tok, cap, tile):
    n_tiles = n_tok // tile

    def bisect(b, thr_bits):
        cand = thr_bits | jnp.left_shift(jnp.int32(1), 30 - b)
        cnt = jnp.sum(jnp.where(aff_ref[...] >= lax.bitcast_convert_type(cand, F32), 1.0, 0.0), axis=1, keepdims=True)
        return jnp.where(cnt >= cap, cand, thr_bits)

    thr = lax.bitcast_convert_type(lax.fori_loop(0, 31, bisect, jnp.zeros((N_EXPERTS, 1), I32)), F32)
    need = cap - jnp.sum(jnp.where(aff_ref[...] > thr, 1.0, 0.0), axis=1, keepdims=True)

    r = lax.broadcasted_iota(I32, (tile, tile), 0)
    c = lax.broadcasted_iota(I32, (tile, tile), 1)
    before = jnp.where(r < c, 1.0, 0.0).astype(BF16)

    def tile_body(i, carry):
        start, eq_seen = carry
        off = pl.multiple_of(i * tile, tile)
        a = aff_ref[:, pl.ds(off, tile)]
        eq = jnp.where(a == thr, 1.0, 0.0)
        eq_rank = eq_seen + _dot(eq.astype(BF16), before)
        sel = jnp.where(a > thr, 1.0, jnp.where(eq_rank < need, eq, 0.0))
        rank = _dot(sel.astype(BF16), before)
        slot_ref[:, pl.ds(off, tile)] = jnp.where(sel > 0.0, rank, -1.0).astype(I32)
        cnt = jnp.sum(sel, axis=1, keepdims=True)
        starts_ref[i] = jnp.broadcast_to(start, (N_EXPERTS, LANES)).astype(I32)
        head = start - jnp.floor(start * (1.0 / SEG_ALIGN)) * SEG_ALIGN
        nch = jnp.max(jnp.floor((head + cnt) * (1.0 / ROUTE_CHUNK)) + 1.0, axis=0, keepdims=True)
        nchunk_ref[i] = jnp.broadcast_to(nch, (8, LANES)).astype(I32)
        return start + cnt, eq_seen + jnp.sum(eq, axis=1, keepdims=True)

    zero = jnp.zeros((N_EXPERTS, 1), F32)
    total, _ = lax.fori_loop(0, n_tiles, tile_body, (zero, zero), unroll=4)
    tot_ref[...] = jnp.broadcast_to(total, (N_EXPERTS, LANES)).astype(I32)


def _select(aff_t, cap):
    n_exp, n = aff_t.shape
    tile = ROUTE_TILE
    n_tiles = n // tile
    slot, starts, nchunk, tot = pl.pallas_call(
        functools.partial(_select_kernel, n_tok=n, cap=cap, tile=tile),
        grid=(1,),
        in_specs=[_full((n_exp, n))],
        out_specs=[_full((n_exp, n)), _full((n_tiles, n_exp, LANES)), _full((n_tiles, 8, LANES)), _full((n_exp, LANES))],
        out_shape=[jax.ShapeDtypeStruct((n_exp, n), I32), jax.ShapeDtypeStruct((n_tiles, n_exp, LANES), I32),
                   jax.ShapeDtypeStruct((n_tiles, 8, LANES), I32), jax.ShapeDtypeStruct((n_exp, LANES), I32)],
        compiler_params=_params(("arbitrary",)),
        name="select",
    )(aff_t)
    return slot, starts[:, :, 0], nchunk[:, 0, 0], tot[:, 0]


def _seg_base(starts_ref, step, e):
    return pl.multiple_of((starts_ref[step * N_EXPERTS + e] // SEG_ALIGN) * SEG_ALIGN, SEG_ALIGN)


def _dispatch_kernel(starts_ref, nchunk_ref, slot_ref, h_ref, scol_ref, xe_hbm, stage, head, sem, *, tile, chunk):
    i = pl.program_id(0)
    sl = i % 2

    def copies(step, c, buf):
        return [pltpu.make_async_copy(
            stage.at[buf, pl.ds(e * chunk, chunk), :],
            xe_hbm.at[e, pl.ds(_seg_base(starts_ref, step, e) + c * chunk, chunk), :],
            sem.at[buf]) for e in range(N_EXPERTS)]

    @pl.when(i == 0)
    def _():
        head[...] = jnp.zeros(head.shape, BF16)

    slot = slot_ref[...]
    pos = jnp.where(slot >= 0, slot + (scol_ref[...] & (SEG_ALIGN - 1)), -1)

    def build(c, buf):
        want = lax.broadcasted_iota(I32, (chunk, tile), 0) + c * chunk
        onehot = jnp.concatenate(
            [jnp.where(pos[e:e + 1, :] == want, 1.0, 0.0).astype(BF16) for e in range(N_EXPERTS)], axis=0)
        stage[buf] = _dot(onehot, h_ref[...]).astype(BF16)

    nxt = jnp.minimum(i + 1, pl.num_programs(0) - 1)
    rel = [starts_ref[nxt * N_EXPERTS + e] - _seg_base(starts_ref, i, e) for e in range(N_EXPERTS)]

    def keep_head(e, buf):
        row = pl.multiple_of(e * chunk + ((rel[e] % chunk) // SEG_ALIGN) * SEG_ALIGN, SEG_ALIGN)
        head[e] = stage[buf, pl.ds(row, SEG_ALIGN), :]

    build(0, sl)
    for e in range(N_EXPERTS):
        stage[sl, e * chunk:e * chunk + SEG_ALIGN, :] = stage[sl, e * chunk:e * chunk + SEG_ALIGN, :] + head[e]
    for e in range(N_EXPERTS):
        keep_head(e, sl)

    @pl.when(i > 0)
    def _():
        for cp in copies(i - 1, 0, 1 - sl):
            cp.wait()

    for cp in copies(i, 0, sl):
        cp.start()

    def overflow(c, carry):
        for cp in copies(i, c - 1, sl):
            cp.wait()
        build(c, sl)
        for e in range(N_EXPERTS):
            @pl.when(rel[e] // chunk == c)
            def _():
                keep_head(e, sl)
        for cp in copies(i, c, sl):
            cp.start()
        return carry

    lax.fori_loop(1, nchunk_ref[i], overflow, 0)

    @pl.when(i == pl.num_programs(0) - 1)
    def _():
        for cp in copies(i, 0, sl):
            cp.wait()
        rows = xe_hbm.shape[1]
        zbuf = 1 - sl
        stage[zbuf] = jnp.zeros(stage.shape[1:], BF16)
        end = [_seg_base(starts_ref, i, e) + jnp.maximum(nchunk_ref[i], 1) * chunk for e in range(N_EXPERTS)]
        n_full = [(rows - end[e]) // chunk for e in range(N_EXPERTS)]

        def zero_copy(e, pos):
            return pltpu.make_async_copy(stage.at[zbuf, pl.ds(e * chunk, chunk), :],
                                         xe_hbm.at[e, pl.ds(pl.multiple_of(pos, SEG_ALIGN), chunk), :], sem.at[zbuf])

        def fill(k, carry):
            for e in range(N_EXPERTS):
                @pl.when(k < n_full[e])
                def _():
                    zero_copy(e, end[e] + k * chunk).start()
            for e in range(N_EXPERTS):
                @pl.when(k < n_full[e])
                def _():
                    zero_copy(e, end[e] + k * chunk).wait()
            return carry

        lax.fori_loop(0, functools.reduce(jnp.maximum, n_full), fill, 0)
        for e in range(N_EXPERTS):
            zero_copy(e, rows - chunk).start()
        for e in range(N_EXPERTS):
            zero_copy(e, rows - chunk).wait()


def _dispatch(slot, h2e, starts, nchunk, rows):
    n = h2e.shape[0]
    tile, chunk = ROUTE_TILE, ROUTE_CHUNK
    n_tiles = n // tile
    return pl.pallas_call(
        functools.partial(_dispatch_kernel, tile=tile, chunk=chunk),
        grid_spec=pltpu.PrefetchScalarGridSpec(
            num_scalar_prefetch=2, grid=(n_tiles,),
            in_specs=[pl.BlockSpec((N_EXPERTS, tile), lambda i, *_: (0, i)),
                      pl.BlockSpec((tile, D_MODEL + LANES), lambda i, *_: (i, 0)),
                      pl.BlockSpec((None, N_EXPERTS, 1), lambda i, *_: (i, 0, 0))],
            out_specs=pl.BlockSpec(memory_space=pl.ANY),
            scratch_shapes=[pltpu.VMEM((2, N_EXPERTS * chunk, XE_W), BF16),
                            pltpu.VMEM((N_EXPERTS, SEG_ALIGN, XE_W), BF16), pltpu.SemaphoreType.DMA((2,))]),
        out_shape=jax.ShapeDtypeStruct((N_EXPERTS, rows, XE_W), BF16),
        compiler_params=_params(("arbitrary",)),
        name="dispatch",
    )(starts.reshape(-1), nchunk, slot, h2e, starts.reshape(n_tiles, N_EXPERTS, 1))


def _ffn_kernel(tot_ref, *refs, tile, first_tile):
    n_grp = len(first_tile) - 1
    xe_refs, (wg_ref, wu_ref, wd_ref) = refs[:n_grp], refs[n_grp:n_grp + 3]
    ye_refs, (wg_s, wu_s, wd_s) = refs[n_grp + 3:2 * n_grp + 3], refs[2 * n_grp + 3:]
    e = pl.program_id(0)
    j = pl.program_id(1)

    @pl.when(j == 0)
    def _():
        wg_s[...] = wg_ref[...].astype(BF16)
        wu_s[...] = wu_ref[...].astype(BF16)
        wd_s[...] = wd_ref[...].astype(BF16)

    def run(xe_ref, ye_ref, live):
        @pl.when(live > 0)
        def _():
            x = xe_ref[:, :D_MODEL]
            pieces = xe_ref[:, D_MODEL:].astype(F32)
            lane = lax.broadcasted_iota(I32, pieces.shape, 1)
            mine = ((lane & (N_EXPERTS - 1)) == e) & (lane < 3 * N_EXPERTS)
            gate = jnp.sum(jnp.where(mine, pieces, 0.0), axis=-1, keepdims=True)
            hid = []
            for fc in range(wg_s.shape[1] // FFN_COLS):
                cs = slice(fc * FFN_COLS, (fc + 1) * FFN_COLS)
                g = _dot(x, wg_s[:, cs])
                u = _dot(x, wu_s[:, cs])
                hid.append((g * jax.nn.sigmoid(g) * u).astype(BF16))
            y = _dot(jnp.concatenate(hid, axis=1), wd_s[...]) * gate
            rows = lax.broadcasted_iota(I32, y.shape, 0)
            ye_ref[...] = jnp.where(rows < live, y, 0.0).astype(BF16)

        @pl.when(live <= 0)
        def _():
            ye_ref[...] = jnp.zeros(ye_ref.shape, BF16)

    for g in range(n_grp):
        @pl.when((j >= first_tile[g]) & (j < first_tile[g + 1]))
        def _():
            run(xe_refs[g], ye_refs[g], tot_ref[g * N_EXPERTS + e] - (j - first_tile[g]) * tile)


def _ffn(xes, tots, w_gate, w_up, w_down):
    tile = FFN_TILE
    n_exp, d, f = w_gate.shape
    n_tiles = [xe.shape[1] // tile for xe in xes]
    first_tile = [sum(n_tiles[:g]) for g in range(len(xes) + 1)]

    def xe_spec(g):
        def index(e, j, tot):
            last = jnp.maximum(tot[g * N_EXPERTS + e] - 1, 0) // tile
            return (e, jnp.clip(j - first_tile[g], 0, last), 0)
        return pl.BlockSpec((None, tile, XE_W), index)

    def ye_spec(g):
        return pl.BlockSpec((None, tile, d), lambda e, j, tot: (e, jnp.clip(j - first_tile[g], 0, n_tiles[g] - 1), 0))

    wspec = lambda a, b: pl.BlockSpec((None, a, b), lambda e, j, tot: (e, 0, 0))
    return pl.pallas_call(
        functools.partial(_ffn_kernel, tile=tile, first_tile=tuple(first_tile)),
        grid_spec=pltpu.PrefetchScalarGridSpec(
            num_scalar_prefetch=1, grid=(n_exp, first_tile[-1]),
            in_specs=[xe_spec(g) for g in range(len(xes))] + [wspec(d, f), wspec(d, f), wspec(f, d)],
            out_specs=[ye_spec(g) for g in range(len(xes))],
            scratch_shapes=[pltpu.VMEM((d, f), BF16), pltpu.VMEM((d, f), BF16), pltpu.VMEM((f, d), BF16)]),
        out_shape=[jax.ShapeDtypeStruct((n_exp, xe.shape[1], d), BF16) for xe in xes],
        compiler_params=_params(("arbitrary", "arbitrary")),
        name="ffn",
    )(jnp.concatenate(tots), *xes, w_gate, w_up, w_down)


def _combine_kernel(starts_ref, nchunk_ref, x1_ref, slot_ref, srow_ref, ye_hbm, out_ref, stage, sem, *, tile, chunk):
    i = pl.program_id(0)
    sl = i % 2
    over = 2

    def copies(step, c, buf):
        return [pltpu.make_async_copy(
            ye_hbm.at[e, pl.ds(_seg_base(starts_ref, step, e) + c * chunk, chunk), :],
            stage.at[buf, pl.ds(e * chunk, chunk), :],
            sem.at[buf]) for e in range(N_EXPERTS)]

    @pl.when(i == 0)
    def _():
        for cp in copies(0, 0, 0):
            cp.start()

    @pl.when(i + 1 < pl.num_programs(0))
    def _():
        for cp in copies(i + 1, 0, 1 - sl):
            cp.start()

    col = lax.broadcasted_iota(I32, (N_EXPERTS, N_EXPERTS * chunk), 1)
    exp_row = lax.broadcasted_iota(I32, (N_EXPERTS, N_EXPERTS * chunk), 0)
    spread = jnp.where(col // chunk == exp_row, 1.0, 0.0).astype(BF16)
    slot = slot_ref[...]
    rank = jnp.where(slot < 0, -float(tile), slot.astype(F32)).astype(BF16)
    head = jnp.broadcast_to((srow_ref[...] & (SEG_ALIGN - 1)).astype(F32).astype(BF16), (8, N_EXPERTS))
    slots = _dot(rank, spread) + _dot(head, spread)[0:1]
    lane_slot = (lax.broadcasted_iota(I32, slots.shape, 1) & (chunk - 1)).astype(F32)

    def gathered(c, buf):
        onehot = jnp.where(slots == lane_slot + c * chunk, 1.0, 0.0).astype(BF16)
        return _dot(onehot, stage[buf])

    for cp in copies(i, 0, sl):
        cp.wait()
    out_ref[...] = x1_ref[...] + gathered(0, sl)

    def overflow(c, carry):
        for cp in copies(i, c, over):
            cp.start()
        for cp in copies(i, c, over):
            cp.wait()
        out_ref[...] += gathered(c, over)
        return carry

    lax.fori_loop(1, nchunk_ref[i], overflow, 0)


def _combine(x1, slot_t, ye, starts, nchunk):
    n, d = x1.shape
    tile, chunk = ROUTE_TILE, ROUTE_CHUNK
    n_tiles = n // tile
    return pl.pallas_call(
        functools.partial(_combine_kernel, tile=tile, chunk=chunk),
        grid_spec=pltpu.PrefetchScalarGridSpec(
            num_scalar_prefetch=2, grid=(n_tiles,),
            in_specs=[pl.BlockSpec((tile, d), lambda i, *_: (i, 0)),
                      pl.BlockSpec((tile, N_EXPERTS), lambda i, *_: (i, 0)),
                      pl.BlockSpec((None, 1, N_EXPERTS), lambda i, *_: (i, 0, 0)),
                      pl.BlockSpec(memory_space=pl.ANY)],
            out_specs=pl.BlockSpec((tile, d), lambda i, *_: (i, 0)),
            scratch_shapes=[pltpu.VMEM((3, N_EXPERTS * chunk, d), BF16), pltpu.SemaphoreType.DMA((3,))]),
        out_shape=jax.ShapeDtypeStruct((n, d), F32),
        compiler_params=_params(("arbitrary",)),
        name="combine",
    )(starts.reshape(-1), nchunk, x1, slot_t, starts.reshape(n_tiles, 1, N_EXPERTS), ye)


def _expert_rows(cap):
    need = cap + ROUTE_TILE + 2 * ROUTE_CHUNK
    return -(-need // FFN_TILE) * FFN_TILE


def _routed_ffn(groups, w):
    routes = []
    for x1, h2e, aff_t in groups:
        cap = max(1, EC_CAPACITY * x1.shape[0] // N_EXPERTS)
        slot, starts, nchunk, tot = _select(aff_t, cap)
        routes.append((slot, starts, nchunk, tot, _dispatch(slot, h2e, starts, nchunk, _expert_rows(cap))))
    yes = _ffn([r[4] for r in routes], [r[3] for r in routes], w["w_exp_gate"], w["w_exp_up"], w["w_exp_down"])
    return [_combine(x1, slot.T, ye, starts, nchunk)
            for (x1, _, _), (slot, starts, nchunk, _, _), ye in zip(groups, routes, yes)]


def _mixers(x, mem, w):
    b, s, d = x.shape
    x2 = x.reshape(b * s, d)
    kx, vx = _memkv(mem, w["mem_norm_g"], w["w_mem_kv"], w["xk_gain"])
    *qkv, p, g1 = _proj(x2, s, kx, vx, w)
    os_, ls_ = [], []
    for g in range(len(DIL_GROUPS)):
        o, lse = _attn(qkv[g], qkv[3 + g], qkv[6 + g], b, s, g)
        os_.append(o)
        ls_.append(lse)
    return _merge(x2, p, g1, os_, ls_, w)


def _layer(xs, mems, w):
    outs = _routed_ffn([_mixers(x, mem, w) for x, mem in zip(xs, mems)], w)
    return [o.reshape(x.shape) for o, x in zip(outs, xs)]


def _prepare(norm1_g, w_in, conv_w, q_norm_g, k_norm_g, mem_norm_g, w_mem_kv, xq_norm_g, xk_norm_g,
             w_br_conv, w_br_attn, w_br_xattn, w_o, norm2_g, w_router, w_exp_gate, w_exp_up, w_exp_down):
    row = lambda v: v.reshape(1, -1).astype(F32)
    head = jnp.arange(GROUP_WIDTH) // ATT_HEAD_DIM
    return {
        "norm1_g": row(norm1_g), "norm2_g": row(norm2_g), "mem_norm_g": row(mem_norm_g),
        "w_in": w_in.astype(BF16), "conv_w": conv_w.astype(F32),
        "q_gain": row(jnp.tile(q_norm_g, HEADS_PER_GROUP) * (ATT_HEAD_DIM ** -0.5)),
        "k_gain": row(jnp.tile(k_norm_g, HEADS_PER_GROUP)),
        "xq_gain": row(xq_norm_g), "xk_gain": row(xk_norm_g),
        "w_mem_kv": w_mem_kv.astype(BF16), "w_br_conv": w_br_conv.astype(BF16),
        "w_br_attn": w_br_attn.astype(BF16), "w_br_xattn": w_br_xattn.astype(BF16), "w_o": w_o.astype(BF16),
        "head_avg": jnp.where(head[:, None] == head[None, :], 1.0 / ATT_HEAD_DIM, 0.0).astype(BF16),
        "w_router3": jnp.concatenate([w_router] * 3 + [jnp.zeros((D_MODEL, LANES - 3 * N_EXPERTS), F32)],
                                     axis=1).astype(BF16),
        "w_exp_gate": w_exp_gate, "w_exp_up": w_exp_up, "w_exp_down": w_exp_down,
    }


def kernel(x_prompt, x_sample, mem_prompt, mem_sample, norm1_g, w_in, conv_w, q_norm_g, k_norm_g, mem_norm_g, w_mem_kv, xq_norm_g, xk_norm_g, w_br_conv, w_br_attn, w_br_xattn, w_o, norm2_g, w_router, w_exp_gate, w_exp_up, w_exp_down):
    per_layer = (norm1_g, w_in, conv_w, q_norm_g, k_norm_g, mem_norm_g, w_mem_kv, xq_norm_g, xk_norm_g,
                 w_br_conv, w_br_attn, w_br_xattn, w_o, norm2_g, w_router, w_exp_gate, w_exp_up, w_exp_down)
    ys = [x_prompt, x_sample]
    for layer in range(norm1_g.shape[0]):
        ys = _layer(ys, (mem_prompt, mem_sample), _prepare(*(t[layer] for t in per_layer)))
    return tuple(ys)
```

```python
import functools

import jax
import jax.numpy as jnp
from jax import lax
from jax.experimental import pallas as pl
from jax.experimental.pallas import tpu as pltpu

F32 = jnp.float32
BF16 = jnp.bfloat16
I32 = jnp.int32

D_MODEL = 1024
N_MEM = 256
CONV_WIDTH = 768
ATT_HEAD_DIM = 64
DIL_GROUPS = ((128, 1), (512, 4), (2048, 16))
HEADS_PER_GROUP = 4
N_ATT_HEADS = HEADS_PER_GROUP * len(DIL_GROUPS)
ATT_WIDTH = N_ATT_HEADS * ATT_HEAD_DIM
GROUP_WIDTH = HEADS_PER_GROUP * ATT_HEAD_DIM
ATT_RADIUS = 64
XATT_HEADS = 4
XATT_HEAD_DIM = 128
XATT_WIDTH = XATT_HEADS * XATT_HEAD_DIM
N_EXPERTS = 16
EC_CAPACITY = 2
ALIBI_MAX_EXP = 8.0
EPS = 1e-6
NEG_INF = -1e30

C_CB, C_CC, C_AQ, C_XQ, C_GATE, C_END = 0, 768, 2304, 4608, 5120, 8192

V7X_VMEM_LIMIT_BYTES = 56 * 1024 * 1024
LANES = 128
HALO = 16

PROJ_TILE = 512
MERGE_SPLIT = 2
ATT_QB = 128
ATT_UNITS = 4
ROUTE_TILE = 256
ROUTE_CHUNK = 64
SEG_ALIGN = 16
FFN_TILE = 512
FFN_COLS = 256
XE_W = D_MODEL + LANES


def _dot(a, b):
    return jnp.dot(a, b, preferred_element_type=F32)


def _dot_nt(a, b):
    return lax.dot_general(a, b, (((1,), (1,)), ((), ())), preferred_element_type=F32)


def _params(sem):
    return pltpu.CompilerParams(dimension_semantics=sem, vmem_limit_bytes=V7X_VMEM_LIMIT_BYTES)


def _full(shape):
    return pl.BlockSpec(shape, lambda *_: (0,) * len(shape))


def _resident(shape):
    return pl.BlockSpec(shape, lambda *_: (0,) * len(shape), pipeline_mode=pl.Buffered(1))


def _memkv_kernel(mem_ref, g_ref, w_ref, kg_ref, k_ref, v_ref):
    m = mem_ref[...]
    hn = (m * lax.rsqrt(jnp.mean(m * m, axis=-1, keepdims=True) + EPS) * g_ref[...]).astype(BF16)
    kv = _dot(hn, w_ref[...])
    ks = []
    for h in range(XATT_HEADS):
        kh = kv[:, h * XATT_HEAD_DIM:(h + 1) * XATT_HEAD_DIM]
        ks.append(kh * lax.rsqrt(jnp.mean(kh * kh, axis=-1, keepdims=True) + EPS) * kg_ref[...])
    k_ref[...] = jnp.concatenate(ks, axis=1).astype(BF16)
    v_ref[...] = kv[:, XATT_WIDTH:].astype(BF16)


def _memkv(mem, mem_g, w_mem_kv, xk_g):
    b, m, d = mem.shape
    out = jax.ShapeDtypeStruct((b, m, XATT_WIDTH), BF16)
    return pl.pallas_call(
        _memkv_kernel,
        grid=(b,),
        in_specs=[pl.BlockSpec((None, m, d), lambda i: (i, 0, 0)), _full((1, d)),
                  _full((d, 2 * XATT_WIDTH)), _full((1, XATT_HEAD_DIM))],
        out_specs=[pl.BlockSpec((None, m, XATT_WIDTH), lambda i: (i, 0, 0))] * 2,
        out_shape=[out, out],
        compiler_params=_params(("arbitrary",)),
        name="memkv",
    )(mem, mem_g, w_mem_kv, xk_g)


def _proj_kernel(xp_ref, x_ref, xn_ref, n1g_ref, win_ref, cw_ref, qg_ref, kg_ref, xqg_ref, kx_ref, vx_ref,
                 wbc_ref, wbx_ref, bd_ref,
                 q0_ref, q1_ref, q2_ref, k0_ref, k1_ref, k2_ref, v0_ref, v1_ref, v2_ref, p_ref, g1_ref,
                 hb_s, u_s, il_s, *, tile, tiles_per_seq):
    tin = pl.program_id(0) % tiles_per_seq
    gain = n1g_ref[...]

    def nrm(x):
        return (x * lax.rsqrt(jnp.mean(x * x, axis=-1, keepdims=True) + EPS) * gain).astype(BF16)

    hb_s[0:HALO, :] = nrm(xp_ref[...])
    hb_s[HALO:HALO + tile, :] = nrm(x_ref[...])
    hb_s[HALO + tile:, :] = nrm(xn_ref[...])
    hc = hb_s[HALO:HALO + tile, :]

    ccx = _dot(hb_s[...], win_ref[:, C_CC:C_AQ])
    u = ccx[:, :CONV_WIDTH] * ccx[:, CONV_WIDTH:]
    u_s[0:HALO, :] = u[0:HALO] * jnp.where(tin == 0, 0.0, 1.0)
    u_s[HALO:HALO + tile, :] = u[HALO:HALO + tile]
    u_s[HALO + tile:, :] = u[HALO + tile:] * jnp.where(tin == tiles_per_seq - 1, 0.0, 1.0)
    cw = cw_ref[...]
    conv = (cw[0:1] * u_s[HALO - 1:HALO - 1 + tile, :] + cw[1:2] * u_s[HALO:HALO + tile, :]
            + cw[2:3] * u_s[HALO + 1:HALO + 1 + tile, :])
    cb = _dot(hc, win_ref[:, C_CB:C_CC])
    y_conv = _dot((cb * conv).astype(BF16), wbc_ref[...])

    qkv = _dot(hc, win_ref[:, C_AQ:C_XQ])
    bd = bd_ref[...]

    def head_norm(z, g_ref):
        outs = []
        for c in range(len(DIL_GROUPS)):
            zc = z[:, c * GROUP_WIDTH:(c + 1) * GROUP_WIDTH]
            ms = _dot((zc * zc).astype(BF16), bd)
            outs.append(zc * lax.rsqrt(ms + EPS) * g_ref[...])
        return outs

    def emit(ref, val, dilation):
        if dilation == 1:
            ref[...] = val.astype(BF16)
            return
        for half in range(GROUP_WIDTH // LANES):
            il_s[half] = val[:, half * LANES:(half + 1) * LANES]
        for r in range(dilation):
            for half in range(GROUP_WIDTH // LANES):
                col = r * GROUP_WIDTH + half * LANES
                ref[:, col:col + LANES] = il_s[half, pl.ds(r, tile // dilation, stride=dilation), :].astype(BF16)

    vals = (head_norm(qkv[:, :ATT_WIDTH], qg_ref) + head_norm(qkv[:, ATT_WIDTH:2 * ATT_WIDTH], kg_ref)
            + [qkv[:, 2 * ATT_WIDTH + c * GROUP_WIDTH:2 * ATT_WIDTH + (c + 1) * GROUP_WIDTH] for c in range(len(DIL_GROUPS))])
    refs = (q0_ref, q1_ref, q2_ref, k0_ref, k1_ref, k2_ref, v0_ref, v1_ref, v2_ref)
    for n, (ref, val) in enumerate(zip(refs, vals)):
        emit(ref, val, DIL_GROUPS[n % len(DIL_GROUPS)][1])

    xq = _dot(hc, win_ref[:, C_XQ:C_GATE])
    kx = kx_ref[...]
    vx = vx_ref[...]
    outs = []
    for h in range(XATT_HEADS):
        hs = slice(h * XATT_HEAD_DIM, (h + 1) * XATT_HEAD_DIM)
        qh = xq[:, hs]
        qh = qh * lax.rsqrt(jnp.mean(qh * qh, axis=-1, keepdims=True) + EPS) * xqg_ref[...]
        s = _dot_nt(qh.astype(BF16), kx[:, hs]) * (XATT_HEAD_DIM ** -0.5)
        p = jnp.exp(s - jnp.max(s, axis=-1, keepdims=True))
        den = jnp.sum(p, axis=-1, keepdims=True)
        outs.append(_dot(p.astype(BF16), vx[:, hs]) / den)
    y_x = _dot(jnp.concatenate(outs, axis=1).astype(BF16), wbx_ref[...])

    gs = jax.nn.sigmoid(_dot(hc, win_ref[:, C_GATE:C_END]))
    p_ref[...] = (gs[:, :D_MODEL] * y_conv + gs[:, 2 * D_MODEL:] * y_x).astype(BF16)
    g1_ref[...] = gs[:, D_MODEL:2 * D_MODEL].astype(BF16)


def _proj(x2, seq_len, kx, vx, w):
    n, d = x2.shape
    tile = PROJ_TILE
    tps = seq_len // tile
    hb = tile // HALO
    n_hblk = n // HALO
    wide = jax.ShapeDtypeStruct((n, d), BF16)
    row = lambda width: pl.BlockSpec((tile, width), lambda i: (i, 0))
    dils = [dil for _, dil in DIL_GROUPS] * 3
    grp_shapes = [jax.ShapeDtypeStruct((n // dil, dil * GROUP_WIDTH), BF16) for dil in dils]
    grp_specs = [pl.BlockSpec((tile // dil, dil * GROUP_WIDTH), lambda i: (i, 0)) for dil in dils]
    return pl.pallas_call(
        functools.partial(_proj_kernel, tile=tile, tiles_per_seq=tps),
        grid=(n // tile,),
        in_specs=[
            pl.BlockSpec((HALO, d), lambda i: (jnp.maximum(i * hb - 1, 0), 0)),
            row(d),
            pl.BlockSpec((HALO, d), lambda i: (jnp.minimum((i + 1) * hb, n_hblk - 1), 0)),
            _full((1, d)), _resident((d, C_END)), _full((3, CONV_WIDTH)),
            _full((1, GROUP_WIDTH)), _full((1, GROUP_WIDTH)), _full((1, XATT_HEAD_DIM)),
            pl.BlockSpec((None, N_MEM, XATT_WIDTH), lambda i: (i // tps, 0, 0)),
            pl.BlockSpec((None, N_MEM, XATT_WIDTH), lambda i: (i // tps, 0, 0)),
            _resident((CONV_WIDTH, d)), _resident((XATT_WIDTH, d)), _full((GROUP_WIDTH, GROUP_WIDTH)),
        ],
        out_specs=grp_specs + [row(d), row(d)],
        out_shape=grp_shapes + [wide, wide],
        scratch_shapes=[pltpu.VMEM((tile + 2 * HALO, d), BF16), pltpu.VMEM((tile + 2 * HALO, CONV_WIDTH), F32),
                        pltpu.VMEM((GROUP_WIDTH // LANES, tile, LANES), F32)],
        compiler_params=_params(("arbitrary",)),
        name="proj",
    )(x2, x2, x2, w["norm1_g"], w["w_in"], w["conv_w"], w["q_gain"], w["k_gain"], w["xq_gain"], kx, vx,
      w["w_br_conv"], w["w_br_xattn"], w["head_avg"])


def _attn_kernel(q_ref, k_ref, v_ref, bias_ref, o_ref, lse_ref, *, seq_sub, lq, qb, kw, n_res):
    i = pl.program_id(2)
    lane_head = lax.broadcasted_iota(I32, (qb, GROUP_WIDTH), 1) // ATT_HEAD_DIM
    head_mask = [lane_head == h for h in range(HEADS_PER_GROUP)]
    head_mask_bf = [jnp.where(m, 1.0, 0.0).astype(BF16) for m in head_mask]

    def block(sb, carry):
        row0 = pl.multiple_of(sb * qb, qb)
        qs = i * lq + row0
        ks = pl.multiple_of(jnp.clip(qs - ATT_RADIUS, 0, seq_sub - kw), ATT_RADIUS)
        case = jnp.where(qs == 0, 0, jnp.where(qs == seq_sub - qb, 2, 1))
        for r in range(n_res):
            cs = slice(r * GROUP_WIDTH, (r + 1) * GROUP_WIDTH)
            q = q_ref[pl.ds(row0, qb), cs]
            kk = k_ref[pl.ds(ks, kw), cs]
            vv = v_ref[pl.ds(ks, kw), cs]
            s = _dot_nt(jnp.concatenate([q * m for m in head_mask_bf], axis=0), kk)
            ps, inv_den, lse = [], [], []
            for h in range(HEADS_PER_GROUP):
                sh = s[h * qb:(h + 1) * qb] + bias_ref[case * HEADS_PER_GROUP + h]
                m = jnp.max(sh, axis=-1, keepdims=True)
                p = jnp.exp(sh - m)
                den = jnp.sum(p, axis=-1, keepdims=True)
                ps.append(p.astype(BF16))
                inv_den.append(1.0 / den)
                lse.append(m + jnp.log(den))
            of = _dot(jnp.concatenate(ps, axis=0), vv)
            o = jnp.zeros((qb, GROUP_WIDTH), F32)
            lb = jnp.zeros((qb, GROUP_WIDTH), F32)
            for h in range(HEADS_PER_GROUP):
                o = jnp.where(head_mask[h], of[h * qb:(h + 1) * qb] * inv_den[h], o)
                lb = jnp.where(head_mask[h], lse[h], lb)
            o_ref[pl.ds(row0, qb), cs] = o.astype(BF16)
            lse_ref[pl.ds(row0, qb), cs] = lb
        return carry

    lax.fori_loop(0, lq // qb, block, 0, unroll=max(1, min(lq // qb, ATT_UNITS // n_res)))


def _attn_bias(dilation, qb, kw, group):
    slopes = jnp.exp2(-ALIBI_MAX_EXP * jnp.arange(1, N_ATT_HEADS + 1, dtype=F32) / N_ATT_HEADS)
    slopes = slopes[group * HEADS_PER_GROUP:(group + 1) * HEADS_PER_GROUP]
    a = jnp.arange(qb)[:, None]
    c = jnp.arange(kw)[None, :]
    tabs = []
    for off in (0, -ATT_RADIUS, qb - kw):
        delta = off + c - a
        dist = (dilation * jnp.abs(delta)).astype(F32)
        bias = -slopes[:, None, None] * dist[None]
        tabs.append(jnp.where((jnp.abs(delta) <= ATT_RADIUS)[None], bias, NEG_INF))
    return jnp.concatenate(tabs, axis=0)


def _attn(q, k, v, batch, seq_len, group):
    _, dilation = DIL_GROUPS[group]
    seq_sub = seq_len // dilation
    qb = min(ATT_QB, seq_sub)
    kw = min(qb + 2 * ATT_RADIUS, seq_sub)
    lq = min(4 * qb, seq_sub)
    n_res = min(dilation, 4)
    width = n_res * GROUP_WIDTH
    view = lambda t: t.reshape(batch, seq_sub, dilation * GROUP_WIDTH)
    qspec = pl.BlockSpec((None, lq, width), lambda b, r, i: (b, i, r))
    kspec = pl.BlockSpec((None, seq_sub, width), lambda b, r, i: (b, 0, r))
    bias = _attn_bias(dilation, qb, kw, group)
    o, lse = pl.pallas_call(
        functools.partial(_attn_kernel, seq_sub=seq_sub, lq=lq, qb=qb, kw=kw, n_res=n_res),
        grid=(batch, dilation // n_res, seq_sub // lq),
        in_specs=[qspec, kspec, kspec, _full(bias.shape)],
        out_specs=[qspec, qspec],
        out_shape=[jax.ShapeDtypeStruct((batch, seq_sub, dilation * GROUP_WIDTH), BF16),
                   jax.ShapeDtypeStruct((batch, seq_sub, dilation * GROUP_WIDTH), F32)],
        compiler_params=_params(("arbitrary", "arbitrary", "arbitrary")),
        name=f"attn_d{dilation}",
    )(view(q), view(k), view(v), bias)
    rows = batch * seq_sub
    return o.reshape(rows, dilation * GROUP_WIDTH), lse.reshape(rows, dilation * GROUP_WIDTH)


def _merge_kernel(x_ref, p_ref, g1_ref, o0_ref, o1_ref, o2_ref, l0_ref, l1_ref, l2_ref, wba_ref, wo_ref,
                  n2g_ref, wr_ref, x1_ref, h2e_ref, afft_ref, *il_s, tile):
    halves = GROUP_WIDTH // LANES
    dils = [dil for _, dil in DIL_GROUPS] * 2
    grp_refs = (o0_ref, o1_ref, o2_ref, l0_ref, l1_ref, l2_ref)

    for ref, dilation, scratch in zip(grp_refs, dils, il_s):
        for r in range(dilation if dilation > 1 else 0):
            for half in range(halves):
                col = r * GROUP_WIDTH + half * LANES
                scratch[half, pl.ds(r, tile // dilation, stride=dilation), :] = ref[:, col:col + LANES].astype(F32)

    def token_order(n, rs):
        if dils[n] == 1:
            return grp_refs[n][rs, :].astype(F32)
        return jnp.concatenate([il_s[n][half, rs, :] for half in range(halves)], axis=1)

    rows = tile // MERGE_SPLIT
    blocks = [slice(blk * rows, (blk + 1) * rows) for blk in range(MERGE_SPLIT)]

    def mixture(rs):
        o0, o1, o2, l0, l1, l2 = (token_order(n, rs) for n in range(6))
        m = jnp.maximum(jnp.maximum(l0, l1), l2)
        e0, e1, e2 = jnp.exp(l0 - m), jnp.exp(l1 - m), jnp.exp(l2 - m)
        return ((e0 * o0 + e1 * o1 + e2 * o2) / (e0 + e1 + e2)).astype(BF16)

    mixed = [mixture(rs) for rs in blocks]
    y_attn = [_dot(o, wba_ref[...]) for o in mixed]
    merged = [(p_ref[rs, :].astype(F32) + g1_ref[rs, :].astype(F32) * y).astype(BF16) for rs, y in zip(blocks, y_attn)]
    x1s = [x_ref[rs, :] + _dot(mg, wo_ref[...]) for rs, mg in zip(blocks, merged)]
    h2s = []
    for rs, x1 in zip(blocks, x1s):
        x1_ref[rs, :] = x1
        h2 = (x1 * lax.rsqrt(jnp.mean(x1 * x1, axis=-1, keepdims=True) + EPS) * n2g_ref[...]).astype(BF16)
        h2e_ref[rs, :D_MODEL] = h2
        h2s.append(h2)
    logits = [_dot(h2, wr_ref[...]) for h2 in h2s]
    for rs, lg in zip(blocks, logits):
        lane = lax.broadcasted_iota(I32, lg.shape, 1)
        first = lane < N_EXPERTS
        mx = jnp.max(jnp.where(first, lg, -jnp.inf), axis=-1, keepdims=True)
        ex = jnp.exp(lg - mx)
        a = ex / jnp.sum(jnp.where(first, ex, 0.0), axis=-1, keepdims=True)
        afft_ref[:, rs] = a.T[:N_EXPERTS, :]
        hi = a.astype(BF16).astype(F32)
        mid = (a - hi).astype(BF16).astype(F32)
        lo = (a - hi) - mid
        ext = jnp.where(first, hi, jnp.where(lane < 2 * N_EXPERTS, mid, jnp.where(lane < 3 * N_EXPERTS, lo, 0.0)))
        h2e_ref[rs, D_MODEL:] = ext.astype(BF16)


def _merge(x2, p, g1, os_, ls_, w):
    n, d = x2.shape
    tile = PROJ_TILE
    row = lambda width: pl.BlockSpec((tile, width), lambda i: (i, 0))
    grp = [pl.BlockSpec((tile // dil, dil * GROUP_WIDTH), lambda i: (i, 0)) for _, dil in DIL_GROUPS]
    return pl.pallas_call(
        functools.partial(_merge_kernel, tile=tile),
        grid=(n // tile,),
        in_specs=[row(d), row(d), row(d)] + grp + grp
                 + [_full((GROUP_WIDTH, d)), _full((d, d)), _full((1, d)), _full((d, LANES))],
        out_specs=[row(d), row(d + LANES), pl.BlockSpec((N_EXPERTS, tile), lambda i: (0, i))],
        out_shape=[jax.ShapeDtypeStruct((n, d), F32), jax.ShapeDtypeStruct((n, d + LANES), BF16),
                   jax.ShapeDtypeStruct((N_EXPERTS, n), F32)],
        scratch_shapes=[pltpu.VMEM((GROUP_WIDTH // LANES, tile, LANES), F32)] * 6,
        compiler_params=_params(("arbitrary",)),
        name="merge",
    )(x2, p, g1, *os_, *ls_, w["w_br_attn"], w["w_o"], w["norm2_g"], w["w_router3"])


def _select_kernel(aff_ref, slot_ref, starts_ref, nchunk_ref, tot_ref, *, n_tok, cap, tile):
    n_tiles = n_tok // tile

    def bisect(b, thr_bits):
        cand = thr_bits | jnp.left_shift(jnp.int32(1), 30 - b)
        cnt = jnp.sum(jnp.where(aff_ref[...] >= lax.bitcast_convert_type(cand, F32), 1.0, 0.0), axis=1, keepdims=True)
        return jnp.where(cnt >= cap, cand, thr_bits)

    thr = lax.bitcast_convert_type(lax.fori_loop(0, 31, bisect, jnp.zeros((N_EXPERTS, 1), I32)), F32)
    need = cap - jnp.sum(jnp.where(aff_ref[...] > thr, 1.0, 0.0), axis=1, keepdims=True)

    r = lax.broadcasted_iota(I32, (tile, tile), 0)
    c = lax.broadcasted_iota(I32, (tile, tile), 1)
    before = jnp.where(r < c, 1.0, 0.0).astype(BF16)

    def tile_body(i, carry):
        start, eq_seen = carry
        off = pl.multiple_of(i * tile, tile)
        a = aff_ref[:, pl.ds(off, tile)]
        eq = jnp.where(a == thr, 1.0, 0.0)
        eq_rank = eq_seen + _dot(eq.astype(BF16), before)
        sel = jnp.where(a > thr, 1.0, jnp.where(eq_rank < need, eq, 0.0))
        rank = _dot(sel.astype(BF16), before)
        slot_ref[:, pl.ds(off, tile)] = jnp.where(sel > 0.0, rank, -1.0).astype(I32)
        cnt = jnp.sum(sel, axis=1, keepdims=True)
        starts_ref[i] = jnp.broadcast_to(start, (N_EXPERTS, LANES)).astype(I32)
        head = start - jnp.floor(start * (1.0 / SEG_ALIGN)) * SEG_ALIGN
        nch = jnp.max(jnp.floor((head + cnt) * (1.0 / ROUTE_CHUNK)) + 1.0, axis=0, keepdims=True)
        nchunk_ref[i] = jnp.broadcast_to(nch, (8, LANES)).astype(I32)
        return start + cnt, eq_seen + jnp.sum(eq, axis=1, keepdims=True)

    zero = jnp.zeros((N_EXPERTS, 1), F32)
    total, _ = lax.fori_loop(0, n_tiles, tile_body, (zero, zero), unroll=4)
    tot_ref[...] = jnp.broadcast_to(total, (N_EXPERTS, LANES)).astype(I32)


def _select(aff_t, cap):
    n_exp, n = aff_t.shape
    tile = ROUTE_TILE
    n_tiles = n // tile
    slot, starts, nchunk, tot = pl.pallas_call(
        functools.partial(_select_kernel, n_tok=n, cap=cap, tile=tile),
        grid=(1,),
        in_specs=[_full((n_exp, n))],
        out_specs=[_full((n_exp, n)), _full((n_tiles, n_exp, LANES)), _full((n_tiles, 8, LANES)), _full((n_exp, LANES))],
        out_shape=[jax.ShapeDtypeStruct((n_exp, n), I32), jax.ShapeDtypeStruct((n_tiles, n_exp, LANES), I32),
                   jax.ShapeDtypeStruct((n_tiles, 8, LANES), I32), jax.ShapeDtypeStruct((n_exp, LANES), I32)],
        compiler_params=_params(("arbitrary",)),
        name="select",
    )(aff_t)
    return slot, starts[:, :, 0], nchunk[:, 0, 0], tot[:, 0]


def _seg_base(starts_ref, step, e):
    return pl.multiple_of((starts_ref[step * N_EXPERTS + e] // SEG_ALIGN) * SEG_ALIGN, SEG_ALIGN)


def _dispatch_kernel(starts_ref, nchunk_ref, slot_ref, h_ref, scol_ref, xe_hbm, stage, head, sem, *, tile, chunk):
    i = pl.program_id(0)
    sl = i % 2

    def copies(step, c, buf):
        return [pltpu.make_async_copy(
            stage.at[buf, pl.ds(e * chunk, chunk), :],
            xe_hbm.at[e, pl.ds(_seg_base(starts_ref, step, e) + c * chunk, chunk), :],
            sem.at[buf]) for e in range(N_EXPERTS)]

    @pl.when(i == 0)
    def _():
        head[...] = jnp.zeros(head.shape, BF16)

    slot = slot_ref[...]
    pos = jnp.where(slot >= 0, slot + (scol_ref[...] & (SEG_ALIGN - 1)), -1)

    def build(c, buf):
        want = lax.broadcasted_iota(I32, (chunk, tile), 0) + c * chunk
        onehot = jnp.concatenate(
            [jnp.where(pos[e:e + 1, :] == want, 1.0, 0.0).astype(BF16) for e in range(N_EXPERTS)], axis=0)
        stage[buf] = _dot(onehot, h_ref[...]).astype(BF16)

    nxt = jnp.minimum(i + 1, pl.num_programs(0) - 1)
    rel = [starts_ref[nxt * N_EXPERTS + e] - _seg_base(starts_ref, i, e) for e in range(N_EXPERTS)]

    def keep_head(e, buf):
        row = pl.multiple_of(e * chunk + ((rel[e] % chunk) // SEG_ALIGN) * SEG_ALIGN, SEG_ALIGN)
        head[e] = stage[buf, pl.ds(row, SEG_ALIGN), :]

    build(0, sl)
    for e in range(N_EXPERTS):
        stage[sl, e * chunk:e * chunk + SEG_ALIGN, :] = stage[sl, e * chunk:e * chunk + SEG_ALIGN, :] + head[e]
    for e in range(N_EXPERTS):
        keep_head(e, sl)

    @pl.when(i > 0)
    def _():
        for cp in copies(i - 1, 0, 1 - sl):
            cp.wait()

    for cp in copies(i, 0, sl):
        cp.start()

    def overflow(c, carry):
        for cp in copies(i, c - 1, sl):
            cp.wait()
        build(c, sl)
        for e in range(N_EXPERTS):
            @pl.when(rel[e] // chunk == c)
            def _():
                keep_head(e, sl)
        for cp in copies(i, c, sl):
            cp.start()
        return carry

    lax.fori_loop(1, nchunk_ref[i], overflow, 0)

    @pl.when(i == pl.num_programs(0) - 1)
    def _():
        for cp in copies(i, 0, sl):
            cp.wait()
        rows = xe_hbm.shape[1]
        zbuf = 1 - sl
        stage[zbuf] = jnp.zeros(stage.shape[1:], BF16)
        end = [_seg_base(starts_ref, i, e) + jnp.maximum(nchunk_ref[i], 1) * chunk for e in range(N_EXPERTS)]
        n_full = [(rows - end[e]) // chunk for e in range(N_EXPERTS)]

        def zero_copy(e, pos):
            return pltpu.make_async_copy(stage.at[zbuf, pl.ds(e * chunk, chunk), :],
                                         xe_hbm.at[e, pl.ds(pl.multiple_of(pos, SEG_ALIGN), chunk), :], sem.at[zbuf])

        def fill(wait):
            def body(k, carry):
                for e in range(N_EXPERTS):
                    @pl.when(k < n_full[e])
                    def _():
                        cp = zero_copy(e, end[e] + k * chunk)
                        cp.wait() if wait else cp.start()
                return carry
            return body

        most = functools.reduce(jnp.maximum, n_full)
        lax.fori_loop(0, most, fill(wait=False), 0)
        lax.fori_loop(0, most, fill(wait=True), 0)
        for e in range(N_EXPERTS):
            zero_copy(e, rows - chunk).start()
        for e in range(N_EXPERTS):
            zero_copy(e, rows - chunk).wait()


def _dispatch(slot, h2e, starts, nchunk, rows):
    n = h2e.shape[0]
    tile, chunk = ROUTE_TILE, ROUTE_CHUNK
    n_tiles = n // tile
    return pl.pallas_call(
        functools.partial(_dispatch_kernel, tile=tile, chunk=chunk),
        grid_spec=pltpu.PrefetchScalarGridSpec(
            num_scalar_prefetch=2, grid=(n_tiles,),
            in_specs=[pl.BlockSpec((N_EXPERTS, tile), lambda i, *_: (0, i)),
                      pl.BlockSpec((tile, D_MODEL + LANES), lambda i, *_: (i, 0)),
                      pl.BlockSpec((None, N_EXPERTS, 1), lambda i, *_: (i, 0, 0))],
            out_specs=pl.BlockSpec(memory_space=pl.ANY),
            scratch_shapes=[pltpu.VMEM((2, N_EXPERTS * chunk, XE_W), BF16),
                            pltpu.VMEM((N_EXPERTS, SEG_ALIGN, XE_W), BF16), pltpu.SemaphoreType.DMA((2,))]),
        out_shape=jax.ShapeDtypeStruct((N_EXPERTS, rows, XE_W), BF16),
        compiler_params=_params(("arbitrary",)),
        name="dispatch",
    )(starts.reshape(-1), nchunk, slot, h2e, starts.reshape(n_tiles, N_EXPERTS, 1))


def _ffn_kernel(tot_ref, *refs, tile, first_tile):
    n_grp = len(first_tile) - 1
    xe_refs, (wg_ref, wu_ref, wd_ref) = refs[:n_grp], refs[n_grp:n_grp + 3]
    ye_refs, (wg_s, wu_s, wd_s) = refs[n_grp + 3:2 * n_grp + 3], refs[2 * n_grp + 3:]
    e = pl.program_id(0)
    j = pl.program_id(1)

    def weights(fresh):
        def load(w_ref, w_s, idx):
            if not fresh:
                return w_s[idx]
            w = w_ref[idx].astype(BF16)
            w_s[idx] = w
            return w
        return load

    def run(xe_ref, ye_ref, live, fresh):
        load = weights(fresh)

        @pl.when(live > 0)
        def _():
            x = xe_ref[:, :D_MODEL]
            pieces = xe_ref[:, D_MODEL:].astype(F32)
            lane = lax.broadcasted_iota(I32, pieces.shape, 1)
            mine = ((lane & (N_EXPERTS - 1)) == e) & (lane < 3 * N_EXPERTS)
            gate = jnp.sum(jnp.where(mine, pieces, 0.0), axis=-1, keepdims=True)
            hid = []
            for fc in range(wg_s.shape[1] // FFN_COLS):
                cs = (slice(None), slice(fc * FFN_COLS, (fc + 1) * FFN_COLS))
                g = _dot(x, load(wg_ref, wg_s, cs))
                u = _dot(x, load(wu_ref, wu_s, cs))
                hid.append((g * jax.nn.sigmoid(g) * u).astype(BF16))
            y = _dot(jnp.concatenate(hid, axis=1), load(wd_ref, wd_s, (slice(None), slice(None)))) * gate
            rows = lax.broadcasted_iota(I32, y.shape, 0)
            ye_ref[...] = jnp.where(rows < live, y, 0.0).astype(BF16)

        @pl.when(live <= 0)
        def _():
            ye_ref[...] = jnp.zeros(ye_ref.shape, BF16)

    for g in range(n_grp):
        live = tot_ref[g * N_EXPERTS + e] - (j - first_tile[g]) * tile
        if g == 0:
            @pl.when(j == 0)
            def _():
                run(xe_refs[0], ye_refs[0], live, fresh=True)

        @pl.when((j >= max(first_tile[g], 1)) & (j < first_tile[g + 1]))
        def _():
            run(xe_refs[g], ye_refs[g], live, fresh=False)


def _ffn(xes, tots, w_gate, w_up, w_down):
    tile = FFN_TILE
    n_exp, d, f = w_gate.shape
    n_tiles = [xe.shape[1] // tile for xe in xes]
    first_tile = [sum(n_tiles[:g]) for g in range(len(xes) + 1)]

    def xe_spec(g):
        def index(e, j, tot):
            last = jnp.maximum(tot[g * N_EXPERTS + e] - 1, 0) // tile
            return (e, jnp.clip(j - first_tile[g], 0, last), 0)
        return pl.BlockSpec((None, tile, XE_W), index)

    def ye_spec(g):
        return pl.BlockSpec((None, tile, d), lambda e, j, tot: (e, jnp.clip(j - first_tile[g], 0, n_tiles[g] - 1), 0))

    wspec = lambda a, b: pl.BlockSpec((None, a, b), lambda e, j, tot: (e, 0, 0))
    return pl.pallas_call(
        functools.partial(_ffn_kernel, tile=tile, first_tile=tuple(first_tile)),
        grid_spec=pltpu.PrefetchScalarGridSpec(
            num_scalar_prefetch=1, grid=(n_exp, first_tile[-1]),
            in_specs=[xe_spec(g) for g in range(len(xes))] + [wspec(d, f), wspec(d, f), wspec(f, d)],
            out_specs=[ye_spec(g) for g in range(len(xes))],
            scratch_shapes=[pltpu.VMEM((d, f), BF16), pltpu.VMEM((d, f), BF16), pltpu.VMEM((f, d), BF16)]),
        out_shape=[jax.ShapeDtypeStruct((n_exp, xe.shape[1], d), BF16) for xe in xes],
        compiler_params=_params(("arbitrary", "arbitrary")),
        name="ffn",
    )(jnp.concatenate(tots), *xes, w_gate, w_up, w_down)


def _combine_kernel(starts_ref, nchunk_ref, x1_ref, slot_ref, srow_ref, ye_hbm, out_ref, stage, sem, *, tile, chunk):
    i = pl.program_id(0)
    sl = i % 2
    over = 2

    def copies(step, c, buf):
        return [pltpu.make_async_copy(
            ye_hbm.at[e, pl.ds(_seg_base(starts_ref, step, e) + c * chunk, chunk), :],
            stage.at[buf, pl.ds(e * chunk, chunk), :],
            sem.at[buf]) for e in range(N_EXPERTS)]

    @pl.when(i == 0)
    def _():
        for cp in copies(0, 0, 0):
            cp.start()

    @pl.when(i + 1 < pl.num_programs(0))
    def _():
        for cp in copies(i + 1, 0, 1 - sl):
            cp.start()

    col = lax.broadcasted_iota(I32, (N_EXPERTS, N_EXPERTS * chunk), 1)
    exp_row = lax.broadcasted_iota(I32, (N_EXPERTS, N_EXPERTS * chunk), 0)
    spread = jnp.where(col // chunk == exp_row, 1.0, 0.0).astype(BF16)
    slot = slot_ref[...]
    rank = jnp.where(slot < 0, -float(tile), slot.astype(F32)).astype(BF16)
    head = jnp.broadcast_to((srow_ref[...] & (SEG_ALIGN - 1)).astype(F32).astype(BF16), (8, N_EXPERTS))
    slots = _dot(rank, spread) + _dot(head, spread)[0:1]
    lane_slot = (lax.broadcasted_iota(I32, slots.shape, 1) & (chunk - 1)).astype(F32)

    def gathered(c, buf):
        onehot = jnp.where(slots == lane_slot + c * chunk, 1.0, 0.0).astype(BF16)
        return _dot(onehot, stage[buf])

    for cp in copies(i, 0, sl):
        cp.wait()
    out_ref[...] = x1_ref[...] + gathered(0, sl)

    def overflow(c, carry):
        for cp in copies(i, c, over):
            cp.start()
        for cp in copies(i, c, over):
            cp.wait()
        out_ref[...] += gathered(c, over)
        return carry

    lax.fori_loop(1, nchunk_ref[i], overflow, 0)


def _combine(x1, slot_t, ye, starts, nchunk):
    n, d = x1.shape
    tile, chunk = ROUTE_TILE, ROUTE_CHUNK
    n_tiles = n // tile
    return pl.pallas_call(
        functools.partial(_combine_kernel, tile=tile, chunk=chunk),
        grid_spec=pltpu.PrefetchScalarGridSpec(
            num_scalar_prefetch=2, grid=(n_tiles,),
            in_specs=[pl.BlockSpec((tile, d), lambda i, *_: (i, 0)),
                      pl.BlockSpec((tile, N_EXPERTS), lambda i, *_: (i, 0)),
                      pl.BlockSpec((None, 1, N_EXPERTS), lambda i, *_: (i, 0, 0)),
                      pl.BlockSpec(memory_space=pl.ANY)],
            out_specs=pl.BlockSpec((tile, d), lambda i, *_: (i, 0)),
            scratch_shapes=[pltpu.VMEM((3, N_EXPERTS * chunk, d), BF16), pltpu.SemaphoreType.DMA((3,))]),
        out_shape=jax.ShapeDtypeStruct((n, d), F32),
        compiler_params=_params(("arbitrary",)),
        name="combine",
    )(starts.reshape(-1), nchunk, x1, slot_t, starts.reshape(n_tiles, 1, N_EXPERTS), ye)


def _expert_rows(cap):
    need = cap + ROUTE_TILE + 2 * ROUTE_CHUNK
    return -(-need // FFN_TILE) * FFN_TILE


def _routed_ffn(groups, w):
    routes = []
    for x1, h2e, aff_t in groups:
        cap = max(1, EC_CAPACITY * x1.shape[0] // N_EXPERTS)
        slot, starts, nchunk, tot = _select(aff_t, cap)
        routes.append((slot, starts, nchunk, tot, _dispatch(slot, h2e, starts, nchunk, _expert_rows(cap))))
    yes = _ffn([r[4] for r in routes], [r[3] for r in routes], w["w_exp_gate"], w["w_exp_up"], w["w_exp_down"])
    return [_combine(x1, slot.T, ye, starts, nchunk)
            for (x1, _, _), (slot, starts, nchunk, _, _), ye in zip(groups, routes, yes)]


def _mixers(x, mem, w):
    b, s, d = x.shape
    x2 = x.reshape(b * s, d)
    kx, vx = _memkv(mem, w["mem_norm_g"], w["w_mem_kv"], w["xk_gain"])
    *qkv, p, g1 = _proj(x2, s, kx, vx, w)
    os_, ls_ = [], []
    for g in range(len(DIL_GROUPS)):
        o, lse = _attn(qkv[g], qkv[3 + g], qkv[6 + g], b, s, g)
        os_.append(o)
        ls_.append(lse)
    return _merge(x2, p, g1, os_, ls_, w)


def _layer(xs, mems, w):
    outs = _routed_ffn([_mixers(x, mem, w) for x, mem in zip(xs, mems)], w)
    return [o.reshape(x.shape) for o, x in zip(outs, xs)]


def _prepare(norm1_g, w_in, conv_w, q_norm_g, k_norm_g, mem_norm_g, w_mem_kv, xq_norm_g, xk_norm_g,
             w_br_conv, w_br_attn, w_br_xattn, w_o, norm2_g, w_router, w_exp_gate, w_exp_up, w_exp_down):
    row = lambda v: v.reshape(1, -1).astype(F32)
    head = jnp.arange(GROUP_WIDTH) // ATT_HEAD_DIM
    return {
        "norm1_g": row(norm1_g), "norm2_g": row(norm2_g), "mem_norm_g": row(mem_norm_g),
        "w_in": w_in.astype(BF16), "conv_w": conv_w.astype(F32),
        "q_gain": row(jnp.tile(q_norm_g, HEADS_PER_GROUP) * (ATT_HEAD_DIM ** -0.5)),
        "k_gain": row(jnp.tile(k_norm_g, HEADS_PER_GROUP)),
        "xq_gain": row(xq_norm_g), "xk_gain": row(xk_norm_g),
        "w_mem_kv": w_mem_kv.astype(BF16), "w_br_conv": w_br_conv.astype(BF16),
        "w_br_attn": w_br_attn.astype(BF16), "w_br_xattn": w_br_xattn.astype(BF16), "w_o": w_o.astype(BF16),
        "head_avg": jnp.where(head[:, None] == head[None, :], 1.0 / ATT_HEAD_DIM, 0.0).astype(BF16),
        "w_router3": jnp.concatenate([w_router] * 3 + [jnp.zeros((D_MODEL, LANES - 3 * N_EXPERTS), F32)],
                                     axis=1).astype(BF16),
        "w_exp_gate": w_exp_gate, "w_exp_up": w_exp_up, "w_exp_down": w_exp_down,
    }


def kernel(x_prompt, x_sample, mem_prompt, mem_sample, norm1_g, w_in, conv_w, q_norm_g, k_norm_g, mem_norm_g, w_mem_kv, xq_norm_g, xk_norm_g, w_br_conv, w_br_attn, w_br_xattn, w_o, norm2_g, w_router, w_exp_gate, w_exp_up, w_exp_down):
    per_layer = (norm1_g, w_in, conv_w, q_norm_g, k_norm_g, mem_norm_g, w_mem_kv, xq_norm_g, xk_norm_g,
                 w_br_conv, w_br_attn, w_br_xattn, w_o, norm2_g, w_router, w_exp_gate, w_exp_up, w_exp_down)
    ys = [x_prompt, x_sample]
    for layer in range(norm1_g.shape[0]):
        ys = _layer(ys, (mem_prompt, mem_sample), _prepare(*(t[layer] for t in per_layer)))
    return tuple(ys)
```

```python
import functools

import jax
import jax.numpy as jnp
from jax import lax
from jax.experimental import pallas as pl
from jax.experimental.pallas import tpu as pltpu

F32 = jnp.float32
BF16 = jnp.bfloat16
I32 = jnp.int32

D_MODEL = 1024
N_MEM = 256
CONV_WIDTH = 768
ATT_HEAD_DIM = 64
DIL_GROUPS = ((128, 1), (512, 4), (2048, 16))
HEADS_PER_GROUP = 4
N_ATT_HEADS = HEADS_PER_GROUP * len(DIL_GROUPS)
ATT_WIDTH = N_ATT_HEADS * ATT_HEAD_DIM
GROUP_WIDTH = HEADS_PER_GROUP * ATT_HEAD_DIM
ATT_RADIUS = 64
XATT_HEADS = 4
XATT_HEAD_DIM = 128
XATT_WIDTH = XATT_HEADS * XATT_HEAD_DIM
N_EXPERTS = 16
EC_CAPACITY = 2
ALIBI_MAX_EXP = 8.0
EPS = 1e-6
NEG_INF = -1e30

C_CB, C_CC, C_AQ, C_XQ, C_GATE, C_END = 0, 768, 2304, 4608, 5120, 8192

V7X_VMEM_LIMIT_BYTES = 56 * 1024 * 1024
LANES = 128
HALO = 16

PROJ_TILE = 512
MERGE_SPLIT = 2
ATT_QB = 128
ATT_UNITS = 16
ROUTE_TILE = 256
ROUTE_CHUNK = 64
SEG_ALIGN = 16
FFN_TILE = 512
FFN_COLS = 256
XE_W = D_MODEL + LANES


def _dot(a, b):
    return jnp.dot(a, b, preferred_element_type=F32)


def _dot_nt(a, b):
    return lax.dot_general(a, b, (((1,), (1,)), ((), ())), preferred_element_type=F32)


def _params(sem):
    return pltpu.CompilerParams(dimension_semantics=sem, vmem_limit_bytes=V7X_VMEM_LIMIT_BYTES)


def _full(shape):
    return pl.BlockSpec(shape, lambda *_: (0,) * len(shape))


def _resident(shape):
    return pl.BlockSpec(shape, lambda *_: (0,) * len(shape), pipeline_mode=pl.Buffered(1))


def _memkv_kernel(mem_ref, g_ref, w_ref, kg_ref, k_ref, v_ref):
    m = mem_ref[...]
    hn = (m * lax.rsqrt(jnp.mean(m * m, axis=-1, keepdims=True) + EPS) * g_ref[...]).astype(BF16)
    kv = _dot(hn, w_ref[...])
    ks = []
    for h in range(XATT_HEADS):
        kh = kv[:, h * XATT_HEAD_DIM:(h + 1) * XATT_HEAD_DIM]
        ks.append(kh * lax.rsqrt(jnp.mean(kh * kh, axis=-1, keepdims=True) + EPS) * kg_ref[...])
    k_ref[...] = jnp.concatenate(ks, axis=1).astype(BF16)
    v_ref[...] = kv[:, XATT_WIDTH:].astype(BF16)


def _memkv(mem, mem_g, w_mem_kv, xk_g):
    b, m, d = mem.shape
    out = jax.ShapeDtypeStruct((b, m, XATT_WIDTH), BF16)
    return pl.pallas_call(
        _memkv_kernel,
        grid=(b,),
        in_specs=[pl.BlockSpec((None, m, d), lambda i: (i, 0, 0)), _full((1, d)),
                  _full((d, 2 * XATT_WIDTH)), _full((1, XATT_HEAD_DIM))],
        out_specs=[pl.BlockSpec((None, m, XATT_WIDTH), lambda i: (i, 0, 0))] * 2,
        out_shape=[out, out],
        compiler_params=_params(("arbitrary",)),
        name="memkv",
    )(mem, mem_g, w_mem_kv, xk_g)


def _proj_kernel(xp_ref, x_ref, xn_ref, n1g_ref, win_ref, cw_ref, qg_ref, kg_ref, xqg_ref, kx_ref, vx_ref,
                 wbc_ref, wbx_ref, bd_ref,
                 q0_ref, q1_ref, q2_ref, k0_ref, k1_ref, k2_ref, v0_ref, v1_ref, v2_ref, p_ref, g1_ref,
                 hb_s, u_s, il_s, *, tile, tiles_per_seq):
    tin = pl.program_id(0) % tiles_per_seq
    gain = n1g_ref[...]

    def nrm(x):
        return (x * lax.rsqrt(jnp.mean(x * x, axis=-1, keepdims=True) + EPS) * gain).astype(BF16)

    hb_s[0:HALO, :] = nrm(xp_ref[...])
    hb_s[HALO:HALO + tile, :] = nrm(x_ref[...])
    hb_s[HALO + tile:, :] = nrm(xn_ref[...])
    hc = hb_s[HALO:HALO + tile, :]

    ccx = _dot(hb_s[...], win_ref[:, C_CC:C_AQ])
    u = ccx[:, :CONV_WIDTH] * ccx[:, CONV_WIDTH:]
    u_s[0:HALO, :] = u[0:HALO] * jnp.where(tin == 0, 0.0, 1.0)
    u_s[HALO:HALO + tile, :] = u[HALO:HALO + tile]
    u_s[HALO + tile:, :] = u[HALO + tile:] * jnp.where(tin == tiles_per_seq - 1, 0.0, 1.0)
    cw = cw_ref[...]
    conv = (cw[0:1] * u_s[HALO - 1:HALO - 1 + tile, :] + cw[1:2] * u_s[HALO:HALO + tile, :]
            + cw[2:3] * u_s[HALO + 1:HALO + 1 + tile, :])
    cb = _dot(hc, win_ref[:, C_CB:C_CC])
    y_conv = _dot((cb * conv).astype(BF16), wbc_ref[...])

    qkv = _dot(hc, win_ref[:, C_AQ:C_XQ])
    bd = bd_ref[...]

    def head_norm(z, g_ref):
        outs = []
        for c in range(len(DIL_GROUPS)):
            zc = z[:, c * GROUP_WIDTH:(c + 1) * GROUP_WIDTH]
            ms = _dot((zc * zc).astype(BF16), bd)
            outs.append(zc * lax.rsqrt(ms + EPS) * g_ref[...])
        return outs

    def emit(ref, val, dilation):
        if dilation == 1:
            ref[...] = val.astype(BF16)
            return
        for half in range(GROUP_WIDTH // LANES):
            il_s[half] = val[:, half * LANES:(half + 1) * LANES]
        for r in range(dilation):
            for half in range(GROUP_WIDTH // LANES):
                col = r * GROUP_WIDTH + half * LANES
                ref[:, col:col + LANES] = il_s[half, pl.ds(r, tile // dilation, stride=dilation), :].astype(BF16)

    vals = (head_norm(qkv[:, :ATT_WIDTH], qg_ref) + head_norm(qkv[:, ATT_WIDTH:2 * ATT_WIDTH], kg_ref)
            + [qkv[:, 2 * ATT_WIDTH + c * GROUP_WIDTH:2 * ATT_WIDTH + (c + 1) * GROUP_WIDTH] for c in range(len(DIL_GROUPS))])
    refs = (q0_ref, q1_ref, q2_ref, k0_ref, k1_ref, k2_ref, v0_ref, v1_ref, v2_ref)
    for n, (ref, val) in enumerate(zip(refs, vals)):
        emit(ref, val, DIL_GROUPS[n % len(DIL_GROUPS)][1])

    xq = _dot(hc, win_ref[:, C_XQ:C_GATE])
    kx = kx_ref[...]
    vx = vx_ref[...]
    outs = []
    for h in range(XATT_HEADS):
        hs = slice(h * XATT_HEAD_DIM, (h + 1) * XATT_HEAD_DIM)
        qh = xq[:, hs]
        qh = qh * lax.rsqrt(jnp.mean(qh * qh, axis=-1, keepdims=True) + EPS) * xqg_ref[...]
        s = _dot_nt(qh.astype(BF16), kx[:, hs]) * (XATT_HEAD_DIM ** -0.5)
        p = jnp.exp(s - jnp.max(s, axis=-1, keepdims=True))
        den = jnp.sum(p, axis=-1, keepdims=True)
        outs.append(_dot(p.astype(BF16), vx[:, hs]) / den)
    y_x = _dot(jnp.concatenate(outs, axis=1).astype(BF16), wbx_ref[...])

    gs = jax.nn.sigmoid(_dot(hc, win_ref[:, C_GATE:C_END]))
    p_ref[...] = (gs[:, :D_MODEL] * y_conv + gs[:, 2 * D_MODEL:] * y_x).astype(BF16)
    g1_ref[...] = gs[:, D_MODEL:2 * D_MODEL].astype(BF16)


def _proj(x2, seq_len, kx, vx, w):
    n, d = x2.shape
    tile = PROJ_TILE
    tps = seq_len // tile
    hb = tile // HALO
    n_hblk = n // HALO
    wide = jax.ShapeDtypeStruct((n, d), BF16)
    row = lambda width: pl.BlockSpec((tile, width), lambda i: (i, 0))
    dils = [dil for _, dil in DIL_GROUPS] * 3
    grp_shapes = [jax.ShapeDtypeStruct((n // dil, dil * GROUP_WIDTH), BF16) for dil in dils]
    grp_specs = [pl.BlockSpec((tile // dil, dil * GROUP_WIDTH), lambda i: (i, 0)) for dil in dils]
    return pl.pallas_call(
        functools.partial(_proj_kernel, tile=tile, tiles_per_seq=tps),
        grid=(n // tile,),
        in_specs=[
            pl.BlockSpec((HALO, d), lambda i: (jnp.maximum(i * hb - 1, 0), 0)),
            row(d),
            pl.BlockSpec((HALO, d), lambda i: (jnp.minimum((i + 1) * hb, n_hblk - 1), 0)),
            _full((1, d)), _resident((d, C_END)), _full((3, CONV_WIDTH)),
            _full((1, GROUP_WIDTH)), _full((1, GROUP_WIDTH)), _full((1, XATT_HEAD_DIM)),
            pl.BlockSpec((None, N_MEM, XATT_WIDTH), lambda i: (i // tps, 0, 0)),
            pl.BlockSpec((None, N_MEM, XATT_WIDTH), lambda i: (i // tps, 0, 0)),
            _resident((CONV_WIDTH, d)), _resident((XATT_WIDTH, d)), _full((GROUP_WIDTH, GROUP_WIDTH)),
        ],
        out_specs=grp_specs + [row(d), row(d)],
        out_shape=grp_shapes + [wide, wide],
        scratch_shapes=[pltpu.VMEM((tile + 2 * HALO, d), BF16), pltpu.VMEM((tile + 2 * HALO, CONV_WIDTH), F32),
                        pltpu.VMEM((GROUP_WIDTH // LANES, tile, LANES), F32)],
        compiler_params=_params(("arbitrary",)),
        name="proj",
    )(x2, x2, x2, w["norm1_g"], w["w_in"], w["conv_w"], w["q_gain"], w["k_gain"], w["xq_gain"], kx, vx,
      w["w_br_conv"], w["w_br_xattn"], w["head_avg"])


def _attn_kernel(q_ref, k_ref, v_ref, bias_ref, o_ref, lse_ref, *, seq_sub, lq, qb, kw, n_res):
    i = pl.program_id(2)
    lane_head = lax.broadcasted_iota(I32, (qb, GROUP_WIDTH), 1) // ATT_HEAD_DIM
    head_mask = [lane_head == h for h in range(HEADS_PER_GROUP)]
    head_mask_bf = [jnp.where(m, 1.0, 0.0).astype(BF16) for m in head_mask]

    def block(sb, carry):
        row0 = pl.multiple_of(sb * qb, qb)
        qs = i * lq + row0
        ks = pl.multiple_of(jnp.clip(qs - ATT_RADIUS, 0, seq_sub - kw), ATT_RADIUS)
        case = jnp.where(qs == 0, 0, jnp.where(qs == seq_sub - qb, 2, 1))
        for r in range(n_res):
            cs = slice(r * GROUP_WIDTH, (r + 1) * GROUP_WIDTH)
            q = q_ref[pl.ds(row0, qb), cs]
            kk = k_ref[pl.ds(ks, kw), cs]
            vv = v_ref[pl.ds(ks, kw), cs]
            s = _dot_nt(jnp.concatenate([q * m for m in head_mask_bf], axis=0), kk)
            ps, inv_den, lse = [], [], []
            for h in range(HEADS_PER_GROUP):
                sh = s[h * qb:(h + 1) * qb] + bias_ref[case * HEADS_PER_GROUP + h]
                m = jnp.max(sh, axis=-1, keepdims=True)
                p = jnp.exp(sh - m)
                den = jnp.sum(p, axis=-1, keepdims=True)
                ps.append(p.astype(BF16))
                inv_den.append(1.0 / den)
                lse.append(m + jnp.log(den))
            of = _dot(jnp.concatenate(ps, axis=0), vv)
            o = jnp.zeros((qb, GROUP_WIDTH), F32)
            lb = jnp.zeros((qb, GROUP_WIDTH), F32)
            for h in range(HEADS_PER_GROUP):
                o = jnp.where(head_mask[h], of[h * qb:(h + 1) * qb] * inv_den[h], o)
                lb = jnp.where(head_mask[h], lse[h], lb)
            o_ref[pl.ds(row0, qb), cs] = o.astype(BF16)
            lse_ref[pl.ds(row0, qb), cs] = lb
        return carry

    lax.fori_loop(0, lq // qb, block, 0, unroll=max(1, min(lq // qb, ATT_UNITS // n_res)))


def _attn_bias(dilation, qb, kw, group):
    slopes = jnp.exp2(-ALIBI_MAX_EXP * jnp.arange(1, N_ATT_HEADS + 1, dtype=F32) / N_ATT_HEADS)
    slopes = slopes[group * HEADS_PER_GROUP:(group + 1) * HEADS_PER_GROUP]
    a = jnp.arange(qb)[:, None]
    c = jnp.arange(kw)[None, :]
    tabs = []
    for off in (0, -ATT_RADIUS, qb - kw):
        delta = off + c - a
        dist = (dilation * jnp.abs(delta)).astype(F32)
        bias = -slopes[:, None, None] * dist[None]
        tabs.append(jnp.where((jnp.abs(delta) <= ATT_RADIUS)[None], bias, NEG_INF))
    return jnp.concatenate(tabs, axis=0)


def _attn(q, k, v, batch, seq_len, group):
    _, dilation = DIL_GROUPS[group]
    seq_sub = seq_len // dilation
    qb = min(ATT_QB, seq_sub)
    kw = min(qb + 2 * ATT_RADIUS, seq_sub)
    lq = min(ATT_UNITS * qb, seq_sub)
    n_res = min(dilation, 8)
    width = n_res * GROUP_WIDTH
    view = lambda t: t.reshape(batch, seq_sub, dilation * GROUP_WIDTH)
    qspec = pl.BlockSpec((None, lq, width), lambda b, r, i: (b, i, r))
    kspec = pl.BlockSpec((None, seq_sub, width), lambda b, r, i: (b, 0, r))
    bias = _attn_bias(dilation, qb, kw, group)
    o, lse = pl.pallas_call(
        functools.partial(_attn_kernel, seq_sub=seq_sub, lq=lq, qb=qb, kw=kw, n_res=n_res),
        grid=(batch, dilation // n_res, seq_sub // lq),
        in_specs=[qspec, kspec, kspec, _full(bias.shape)],
        out_specs=[qspec, qspec],
        out_shape=[jax.ShapeDtypeStruct((batch, seq_sub, dilation * GROUP_WIDTH), BF16),
                   jax.ShapeDtypeStruct((batch, seq_sub, dilation * GROUP_WIDTH), F32)],
        compiler_params=_params(("arbitrary", "arbitrary", "arbitrary")),
        name=f"attn_d{dilation}",
    )(view(q), view(k), view(v), bias)
    rows = batch * seq_sub
    return o.reshape(rows, dilation * GROUP_WIDTH), lse.reshape(rows, dilation * GROUP_WIDTH)


def _merge_kernel(x_ref, p_ref, g1_ref, o0_ref, o1_ref, o2_ref, l0_ref, l1_ref, l2_ref, wba_ref, wo_ref,
                  n2g_ref, wr_ref, x1_ref, h2e_ref, afft_ref, *il_s, tile):
    halves = GROUP_WIDTH // LANES
    dils = [dil for _, dil in DIL_GROUPS] * 2
    grp_refs = (o0_ref, o1_ref, o2_ref, l0_ref, l1_ref, l2_ref)

    for ref, dilation, scratch in zip(grp_refs, dils, il_s):
        for r in range(dilation if dilation > 1 else 0):
            for half in range(halves):
                col = r * GROUP_WIDTH + half * LANES
                scratch[half, pl.ds(r, tile // dilation, stride=dilation), :] = ref[:, col:col + LANES].astype(F32)

    def token_order(n, rs):
        if dils[n] == 1:
            return grp_refs[n][rs, :].astype(F32)
        return jnp.concatenate([il_s[n][half, rs, :] for half in range(halves)], axis=1)

    rows = tile // MERGE_SPLIT
    blocks = [slice(blk * rows, (blk + 1) * rows) for blk in range(MERGE_SPLIT)]

    def mixture(rs):
        o0, o1, o2, l0, l1, l2 = (token_order(n, rs) for n in range(6))
        m = jnp.maximum(jnp.maximum(l0, l1), l2)
        e0, e1, e2 = jnp.exp(l0 - m), jnp.exp(l1 - m), jnp.exp(l2 - m)
        return ((e0 * o0 + e1 * o1 + e2 * o2) / (e0 + e1 + e2)).astype(BF16)

    mixed = [mixture(rs) for rs in blocks]
    y_attn = [_dot(o, wba_ref[...]) for o in mixed]
    merged = [(p_ref[rs, :].astype(F32) + g1_ref[rs, :].astype(F32) * y).astype(BF16) for rs, y in zip(blocks, y_attn)]
    x1s = [x_ref[rs, :] + _dot(mg, wo_ref[...]) for rs, mg in zip(blocks, merged)]
    h2s = []
    for rs, x1 in zip(blocks, x1s):
        x1_ref[rs, :] = x1
        h2 = (x1 * lax.rsqrt(jnp.mean(x1 * x1, axis=-1, keepdims=True) + EPS) * n2g_ref[...]).astype(BF16)
        h2e_ref[rs, :D_MODEL] = h2
        h2s.append(h2)
    logits = [_dot(h2, wr_ref[...]) for h2 in h2s]
    for rs, lg in zip(blocks, logits):
        lane = lax.broadcasted_iota(I32, lg.shape, 1)
        first = lane < N_EXPERTS
        mx = jnp.max(jnp.where(first, lg, -jnp.inf), axis=-1, keepdims=True)
        ex = jnp.exp(lg - mx)
        a = ex / jnp.sum(jnp.where(first, ex, 0.0), axis=-1, keepdims=True)
        afft_ref[:, rs] = a.T[:N_EXPERTS, :]
        hi = a.astype(BF16).astype(F32)
        mid = (a - hi).astype(BF16).astype(F32)
        lo = (a - hi) - mid
        ext = jnp.where(first, hi, jnp.where(lane < 2 * N_EXPERTS, mid, jnp.where(lane < 3 * N_EXPERTS, lo, 0.0)))
        h2e_ref[rs, D_MODEL:] = ext.astype(BF16)


def _merge(x2, p, g1, os_, ls_, w):
    n, d = x2.shape
    tile = PROJ_TILE
    row = lambda width: pl.BlockSpec((tile, width), lambda i: (i, 0))
    grp = [pl.BlockSpec((tile // dil, dil * GROUP_WIDTH), lambda i: (i, 0)) for _, dil in DIL_GROUPS]
    return pl.pallas_call(
        functools.partial(_merge_kernel, tile=tile),
        grid=(n // tile,),
        in_specs=[row(d), row(d), row(d)] + grp + grp
                 + [_full((GROUP_WIDTH, d)), _full((d, d)), _full((1, d)), _full((d, LANES))],
        out_specs=[row(d), row(d + LANES), pl.BlockSpec((N_EXPERTS, tile), lambda i: (0, i))],
        out_shape=[jax.ShapeDtypeStruct((n, d), F32), jax.ShapeDtypeStruct((n, d + LANES), BF16),
                   jax.ShapeDtypeStruct((N_EXPERTS, n), F32)],
        scratch_shapes=[pltpu.VMEM((GROUP_WIDTH // LANES, tile, LANES), F32)] * 6,
        compiler_params=_params(("arbitrary",)),
        name="merge",
    )(x2, p, g1, *os_, *ls_, w["w_br_attn"], w["w_o"], w["norm2_g"], w["w_router3"])


def _select_kernel(aff_ref, slot_ref, starts_ref, nchunk_ref, tot_ref, *, n_tok, cap, tile):
    n_tiles = n_tok // tile

    def bisect(b, thr_bits):
        cand = thr_bits | jnp.left_shift(jnp.int32(1), 30 - b)
        cnt = jnp.sum(jnp.where(aff_ref[...] >= lax.bitcast_convert_type(cand, F32), 1.0, 0.0), axis=1, keepdims=True)
        return jnp.where(cnt >= cap, cand, thr_bits)

    thr = lax.bitcast_convert_type(lax.fori_loop(0, 31, bisect, jnp.zeros((N_EXPERTS, 1), I32)), F32)
    need = cap - jnp.sum(jnp.where(aff_ref[...] > thr, 1.0, 0.0), axis=1, keepdims=True)

    r = lax.broadcasted_iota(I32, (tile, tile), 0)
    c = lax.broadcasted_iota(I32, (tile, tile), 1)
    before = jnp.where(r < c, 1.0, 0.0).astype(BF16)

    def tile_body(i, carry):
        start, eq_seen = carry
        off = pl.multiple_of(i * tile, tile)
        a = aff_ref[:, pl.ds(off, tile)]
        eq = jnp.where(a == thr, 1.0, 0.0)
        eq_rank = eq_seen + _dot(eq.astype(BF16), before)
        sel = jnp.where(a > thr, 1.0, jnp.where(eq_rank < need, eq, 0.0))
        rank = _dot(sel.astype(BF16), before)
        slot_ref[:, pl.ds(off, tile)] = jnp.where(sel > 0.0, rank, -1.0).astype(I32)
        cnt = jnp.sum(sel, axis=1, keepdims=True)
        starts_ref[i] = jnp.broadcast_to(start, (N_EXPERTS, LANES)).astype(I32)
        head = start - jnp.floor(start * (1.0 / SEG_ALIGN)) * SEG_ALIGN
        nch = jnp.max(jnp.floor((head + cnt) * (1.0 / ROUTE_CHUNK)) + 1.0, axis=0, keepdims=True)
        nchunk_ref[i] = jnp.broadcast_to(nch, (8, LANES)).astype(I32)
        return start + cnt, eq_seen + jnp.sum(eq, axis=1, keepdims=True)

    zero = jnp.zeros((N_EXPERTS, 1), F32)
    total, _ = lax.fori_loop(0, n_tiles, tile_body, (zero, zero), unroll=4)
    tot_ref[...] = jnp.broadcast_to(total, (N_EXPERTS, LANES)).astype(I32)


def _select(aff_t, cap):
    n_exp, n = aff_t.shape
    tile = ROUTE_TILE
    n_tiles = n // tile
    slot, starts, nchunk, tot = pl.pallas_call(
        functools.partial(_select_kernel, n_tok=n, cap=cap, tile=tile),
        grid=(1,),
        in_specs=[_full((n_exp, n))],
        out_specs=[_full((n_exp, n)), _full((n_tiles, n_exp, LANES)), _full((n_tiles, 8, LANES)), _full((n_exp, LANES))],
        out_shape=[jax.ShapeDtypeStruct((n_exp, n), I32), jax.ShapeDtypeStruct((n_tiles, n_exp, LANES), I32),
                   jax.ShapeDtypeStruct((n_tiles, 8, LANES), I32), jax.ShapeDtypeStruct((n_exp, LANES), I32)],
        compiler_params=_params(("arbitrary",)),
        name="select",
    )(aff_t)
    return slot, starts[:, :, 0], nchunk[:, 0, 0], tot[:, 0]


def _seg_base(starts_ref, step, e):
    return pl.multiple_of((starts_ref[step * N_EXPERTS + e] // SEG_ALIGN) * SEG_ALIGN, SEG_ALIGN)


def _dispatch_kernel(starts_ref, nchunk_ref, slot_ref, h_ref, scol_ref, xe_hbm, stage, head, sem, *, tile, chunk):
    i = pl.program_id(0)
    sl = i % 2

    def copies(step, c, buf):
        return [pltpu.make_async_copy(
            stage.at[buf, pl.ds(e * chunk, chunk), :],
            xe_hbm.at[e, pl.ds(_seg_base(starts_ref, step, e) + c * chunk, chunk), :],
            sem.at[buf]) for e in range(N_EXPERTS)]

    @pl.when(i == 0)
    def _():
        head[...] = jnp.zeros(head.shape, BF16)

    slot = slot_ref[...]
    pos = jnp.where(slot >= 0, slot + (scol_ref[...] & (SEG_ALIGN - 1)), -1)

    def build(c, buf):
        want = lax.broadcasted_iota(I32, (chunk, tile), 0) + c * chunk
        onehot = jnp.concatenate(
            [jnp.where(pos[e:e + 1, :] == want, 1.0, 0.0).astype(BF16) for e in range(N_EXPERTS)], axis=0)
        stage[buf] = _dot(onehot, h_ref[...]).astype(BF16)

    nxt = jnp.minimum(i + 1, pl.num_programs(0) - 1)
    rel = [starts_ref[nxt * N_EXPERTS + e] - _seg_base(starts_ref, i, e) for e in range(N_EXPERTS)]

    def keep_head(e, buf):
        row = pl.multiple_of(e * chunk + ((rel[e] % chunk) // SEG_ALIGN) * SEG_ALIGN, SEG_ALIGN)
        head[e] = stage[buf, pl.ds(row, SEG_ALIGN), :]

    build(0, sl)
    for e in range(N_EXPERTS):
        stage[sl, e * chunk:e * chunk + SEG_ALIGN, :] = stage[sl, e * chunk:e * chunk + SEG_ALIGN, :] + head[e]
    for e in range(N_EXPERTS):
        keep_head(e, sl)

    @pl.when(i > 0)
    def _():
        for cp in copies(i - 1, 0, 1 - sl):
            cp.wait()

    for cp in copies(i, 0, sl):
        cp.start()

    def overflow(c, carry):
        for cp in copies(i, c - 1, sl):
            cp.wait()
        build(c, sl)
        for e in range(N_EXPERTS):
            @pl.when(rel[e] // chunk == c)
            def _():
                keep_head(e, sl)
        for cp in copies(i, c, sl):
            cp.start()
        return carry

    lax.fori_loop(1, nchunk_ref[i], overflow, 0)

    @pl.when(i == pl.num_programs(0) - 1)
    def _():
        for cp in copies(i, 0, sl):
            cp.wait()
        rows = xe_hbm.shape[1]
        zbuf = 1 - sl
        stage[zbuf] = jnp.zeros(stage.shape[1:], BF16)
        end = [_seg_base(starts_ref, i, e) + jnp.maximum(nchunk_ref[i], 1) * chunk for e in range(N_EXPERTS)]
        n_full = [(rows - end[e]) // chunk for e in range(N_EXPERTS)]

        def zero_copy(e, pos):
            return pltpu.make_async_copy(stage.at[zbuf, pl.ds(e * chunk, chunk), :],
                                         xe_hbm.at[e, pl.ds(pl.multiple_of(pos, SEG_ALIGN), chunk), :], sem.at[zbuf])

        def fill(wait):
            def body(k, carry):
                for e in range(N_EXPERTS):
                    @pl.when(k < n_full[e])
                    def _():
                        cp = zero_copy(e, end[e] + k * chunk)
                        cp.wait() if wait else cp.start()
                return carry
            return body

        most = functools.reduce(jnp.maximum, n_full)
        lax.fori_loop(0, most, fill(wait=False), 0)
        lax.fori_loop(0, most, fill(wait=True), 0)
        for e in range(N_EXPERTS):
            zero_copy(e, rows - chunk).start()
        for e in range(N_EXPERTS):
            zero_copy(e, rows - chunk).wait()


def _dispatch(slot, h2e, starts, nchunk, rows):
    n = h2e.shape[0]
    tile, chunk = ROUTE_TILE, ROUTE_CHUNK
    n_tiles = n // tile
    return pl.pallas_call(
        functools.partial(_dispatch_kernel, tile=tile, chunk=chunk),
        grid_spec=pltpu.PrefetchScalarGridSpec(
            num_scalar_prefetch=2, grid=(n_tiles,),
            in_specs=[pl.BlockSpec((N_EXPERTS, tile), lambda i, *_: (0, i)),
                      pl.BlockSpec((tile, D_MODEL + LANES), lambda i, *_: (i, 0)),
                      pl.BlockSpec((None, N_EXPERTS, 1), lambda i, *_: (i, 0, 0))],
            out_specs=pl.BlockSpec(memory_space=pl.ANY),
            scratch_shapes=[pltpu.VMEM((2, N_EXPERTS * chunk, XE_W), BF16),
                            pltpu.VMEM((N_EXPERTS, SEG_ALIGN, XE_W), BF16), pltpu.SemaphoreType.DMA((2,))]),
        out_shape=jax.ShapeDtypeStruct((N_EXPERTS, rows, XE_W), BF16),
        compiler_params=_params(("arbitrary",)),
        name="dispatch",
    )(starts.reshape(-1), nchunk, slot, h2e, starts.reshape(n_tiles, N_EXPERTS, 1))


def _ffn_kernel(tot_ref, *refs, tile, first_tile):
    n_grp = len(first_tile) - 1
    xe_refs, (wg_ref, wu_ref, wd_ref) = refs[:n_grp], refs[n_grp:n_grp + 3]
    ye_refs, (wg_s, wu_s, wd_s) = refs[n_grp + 3:2 * n_grp + 3], refs[2 * n_grp + 3:]
    e = pl.program_id(0)
    j = pl.program_id(1)

    def weights(fresh):
        def load(w_ref, w_s, idx):
            if not fresh:
                return w_s[idx]
            w = w_ref[idx].astype(BF16)
            w_s[idx] = w
            return w
        return load

    def run(xe_ref, ye_ref, live, fresh):
        load = weights(fresh)

        @pl.when(live > 0)
        def _():
            x = xe_ref[:, :D_MODEL]
            pieces = xe_ref[:, D_MODEL:].astype(F32)
            lane = lax.broadcasted_iota(I32, pieces.shape, 1)
            mine = ((lane & (N_EXPERTS - 1)) == e) & (lane < 3 * N_EXPERTS)
            gate = jnp.sum(jnp.where(mine, pieces, 0.0), axis=-1, keepdims=True)
            hid = []
            for fc in range(wg_s.shape[1] // FFN_COLS):
                cs = (slice(None), slice(fc * FFN_COLS, (fc + 1) * FFN_COLS))
                g = _dot(x, load(wg_ref, wg_s, cs))
                u = _dot(x, load(wu_ref, wu_s, cs))
                hid.append((g * jax.nn.sigmoid(g) * u).astype(BF16))
            y = _dot(jnp.concatenate(hid, axis=1), load(wd_ref, wd_s, (slice(None), slice(None)))) * gate
            rows = lax.broadcasted_iota(I32, y.shape, 0)
            ye_ref[...] = jnp.where(rows < live, y, 0.0).astype(BF16)

        @pl.when(live <= 0)
        def _():
            ye_ref[...] = jnp.zeros(ye_ref.shape, BF16)

    for g in range(n_grp):
        live = tot_ref[g * N_EXPERTS + e] - (j - first_tile[g]) * tile
        if g == 0:
            @pl.when(j == 0)
            def _():
                run(xe_refs[0], ye_refs[0], live, fresh=True)

        @pl.when((j >= max(first_tile[g], 1)) & (j < first_tile[g + 1]))
        def _():
            run(xe_refs[g], ye_refs[g], live, fresh=False)


def _ffn(xes, tots, w_gate, w_up, w_down):
    tile = FFN_TILE
    n_exp, d, f = w_gate.shape
    n_tiles = [xe.shape[1] // tile for xe in xes]
    first_tile = [sum(n_tiles[:g]) for g in range(len(xes) + 1)]

    def xe_spec(g):
        def index(e, j, tot):
            last = jnp.maximum(tot[g * N_EXPERTS + e] - 1, 0) // tile
            return (e, jnp.clip(j - first_tile[g], 0, last), 0)
        return pl.BlockSpec((None, tile, XE_W), index)

    def ye_spec(g):
        return pl.BlockSpec((None, tile, d), lambda e, j, tot: (e, jnp.clip(j - first_tile[g], 0, n_tiles[g] - 1), 0))

    wspec = lambda a, b: pl.BlockSpec((None, a, b), lambda e, j, tot: (e, 0, 0))
    return pl.pallas_call(
        functools.partial(_ffn_kernel, tile=tile, first_tile=tuple(first_tile)),
        grid_spec=pltpu.PrefetchScalarGridSpec(
            num_scalar_prefetch=1, grid=(n_exp, first_tile[-1]),
            in_specs=[xe_spec(g) for g in range(len(xes))] + [wspec(d, f), wspec(d, f), wspec(f, d)],
            out_specs=[ye_spec(g) for g in range(len(xes))],
            scratch_shapes=[pltpu.VMEM((d, f), BF16), pltpu.VMEM((d, f), BF16), pltpu.VMEM((f, d), BF16)]),
        out_shape=[jax.ShapeDtypeStruct((n_exp, xe.shape[1], d), BF16) for xe in xes],
        compiler_params=_params(("arbitrary", "arbitrary")),
        name="ffn",
    )(jnp.concatenate(tots), *xes, w_gate, w_up, w_down)


def _combine_kernel(starts_ref, nchunk_ref, x1_ref, slot_ref, srow_ref, ye_hbm, out_ref, stage, sem, *, tile, chunk):
    i = pl.program_id(0)
    sl = i % 2
    over = 2

    def copies(step, c, buf):
        return [pltpu.make_async_copy(
            ye_hbm.at[e, pl.ds(_seg_base(starts_ref, step, e) + c * chunk, chunk), :],
            stage.at[buf, pl.ds(e * chunk, chunk), :],
            sem.at[buf]) for e in range(N_EXPERTS)]

    @pl.when(i == 0)
    def _():
        for cp in copies(0, 0, 0):
            cp.start()

    @pl.when(i + 1 < pl.num_programs(0))
    def _():
        for cp in copies(i + 1, 0, 1 - sl):
            cp.start()

    col = lax.broadcasted_iota(I32, (N_EXPERTS, N_EXPERTS * chunk), 1)
    exp_row = lax.broadcasted_iota(I32, (N_EXPERTS, N_EXPERTS * chunk), 0)
    spread = jnp.where(col // chunk == exp_row, 1.0, 0.0).astype(BF16)
    slot = slot_ref[...]
    rank = jnp.where(slot < 0, -float(tile), slot.astype(F32)).astype(BF16)
    head = jnp.broadcast_to((srow_ref[...] & (SEG_ALIGN - 1)).astype(F32).astype(BF16), (8, N_EXPERTS))
    slots = _dot(rank, spread) + _dot(head, spread)[0:1]
    lane_slot = (lax.broadcasted_iota(I32, slots.shape, 1) & (chunk - 1)).astype(F32)

    def gathered(c, buf):
        onehot = jnp.where(slots == lane_slot + c * chunk, 1.0, 0.0).astype(BF16)
        return _dot(onehot, stage[buf])

    for cp in copies(i, 0, sl):
        cp.wait()
    out_ref[...] = x1_ref[...] + gathered(0, sl)

    def overflow(c, carry):
        for cp in copies(i, c, over):
            cp.start()
        for cp in copies(i, c, over):
            cp.wait()
        out_ref[...] += gathered(c, over)
        return carry

    lax.fori_loop(1, nchunk_ref[i], overflow, 0)


def _combine(x1, slot_t, ye, starts, nchunk):
    n, d = x1.shape
    tile, chunk = ROUTE_TILE, ROUTE_CHUNK
    n_tiles = n // tile
    return pl.pallas_call(
        functools.partial(_combine_kernel, tile=tile, chunk=chunk),
        grid_spec=pltpu.PrefetchScalarGridSpec(
            num_scalar_prefetch=2, grid=(n_tiles,),
            in_specs=[pl.BlockSpec((tile, d), lambda i, *_: (i, 0)),
                      pl.BlockSpec((tile, N_EXPERTS), lambda i, *_: (i, 0)),
                      pl.BlockSpec((None, 1, N_EXPERTS), lambda i, *_: (i, 0, 0)),
                      pl.BlockSpec(memory_space=pl.ANY)],
            out_specs=pl.BlockSpec((tile, d), lambda i, *_: (i, 0)),
            scratch_shapes=[pltpu.VMEM((3, N_EXPERTS * chunk, d), BF16), pltpu.SemaphoreType.DMA((3,))]),
        out_shape=jax.ShapeDtypeStruct((n, d), F32),
        compiler_params=_params(("arbitrary",)),
        name="combine",
    )(starts.reshape(-1), nchunk, x1, slot_t, starts.reshape(n_tiles, 1, N_EXPERTS), ye)


def _expert_rows(cap):
    need = cap + ROUTE_TILE + 2 * ROUTE_CHUNK
    return -(-need // FFN_TILE) * FFN_TILE


def _routed_ffn(groups, w):
    routes = []
    for x1, h2e, aff_t in groups:
        cap = max(1, EC_CAPACITY * x1.shape[0] // N_EXPERTS)
        slot, starts, nchunk, tot = _select(aff_t, cap)
        routes.append((slot, starts, nchunk, tot, _dispatch(slot, h2e, starts, nchunk, _expert_rows(cap))))
    yes = _ffn([r[4] for r in routes], [r[3] for r in routes], w["w_exp_gate"], w["w_exp_up"], w["w_exp_down"])
    return [_combine(x1, slot.T, ye, starts, nchunk)
            for (x1, _, _), (slot, starts, nchunk, _, _), ye in zip(groups, routes, yes)]


def _mixers(x, mem, w):
    b, s, d = x.shape
    x2 = x.reshape(b * s, d)
    kx, vx = _memkv(mem, w["mem_norm_g"], w["w_mem_kv"], w["xk_gain"])
    *qkv, p, g1 = _proj(x2, s, kx, vx, w)
    os_, ls_ = [], []
    for g in range(len(DIL_GROUPS)):
        o, lse = _attn(qkv[g], qkv[3 + g], qkv[6 + g], b, s, g)
        os_.append(o)
        ls_.append(lse)
    return _merge(x2, p, g1, os_, ls_, w)


def _layer(xs, mems, w):
    outs = _routed_ffn([_mixers(x, mem, w) for x, mem in zip(xs, mems)], w)
    return [o.reshape(x.shape) for o, x in zip(outs, xs)]


def _prepare(norm1_g, w_in, conv_w, q_norm_g, k_norm_g, mem_norm_g, w_mem_kv, xq_norm_g, xk_norm_g,
             w_br_conv, w_br_attn, w_br_xattn, w_o, norm2_g, w_router, w_exp_gate, w_exp_up, w_exp_down):
    row = lambda v: v.reshape(1, -1).astype(F32)
    head = jnp.arange(GROUP_WIDTH) // ATT_HEAD_DIM
    return {
        "norm1_g": row(norm1_g), "norm2_g": row(norm2_g), "mem_norm_g": row(mem_norm_g),
        "w_in": w_in.astype(BF16), "conv_w": conv_w.astype(F32),
        "q_gain": row(jnp.tile(q_norm_g, HEADS_PER_GROUP) * (ATT_HEAD_DIM ** -0.5)),
        "k_gain": row(jnp.tile(k_norm_g, HEADS_PER_GROUP)),
        "xq_gain": row(xq_norm_g), "xk_gain": row(xk_norm_g),
        "w_mem_kv": w_mem_kv.astype(BF16), "w_br_conv": w_br_conv.astype(BF16),
        "w_br_attn": w_br_attn.astype(BF16), "w_br_xattn": w_br_xattn.astype(BF16), "w_o": w_o.astype(BF16),
        "head_avg": jnp.where(head[:, None] == head[None, :], 1.0 / ATT_HEAD_DIM, 0.0).astype(BF16),
        "w_router3": jnp.concatenate([w_router] * 3 + [jnp.zeros((D_MODEL, LANES - 3 * N_EXPERTS), F32)],
                                     axis=1).astype(BF16),
        "w_exp_gate": w_exp_gate, "w_exp_up": w_exp_up, "w_exp_down": w_exp_down,
    }


def kernel(x_prompt, x_sample, mem_prompt, mem_sample, norm1_g, w_in, conv_w, q_norm_g, k_norm_g, mem_norm_g, w_mem_kv, xq_norm_g, xk_norm_g, w_br_conv, w_br_attn, w_br_xattn, w_o, norm2_g, w_router, w_exp_gate, w_exp_up, w_exp_down):
    per_layer = (norm1_g, w_in, conv_w, q_norm_g, k_norm_g, mem_norm_g, w_mem_kv, xq_norm_g, xk_norm_g,
                 w_br_conv, w_br_attn, w_br_xattn, w_o, norm2_g, w_router, w_exp_gate, w_exp_up, w_exp_down)
    ys = [x_prompt, x_sample]
    for layer in range(norm1_g.shape[0]):
        ys = _layer(ys, (mem_prompt, mem_sample), _prepare(*(t[layer] for t in per_layer)))
    return tuple(ys)
```

```python
import functools

import jax
import jax.numpy as jnp
from jax import lax
from jax.experimental import pallas as pl
from jax.experimental.pallas import tpu as pltpu

F32 = jnp.float32
BF16 = jnp.bfloat16
I32 = jnp.int32

D_MODEL = 1024
N_MEM = 256
CONV_WIDTH = 768
ATT_HEAD_DIM = 64
DIL_GROUPS = ((128, 1), (512, 4), (2048, 16))
HEADS_PER_GROUP = 4
N_ATT_HEADS = HEADS_PER_GROUP * len(DIL_GROUPS)
ATT_WIDTH = N_ATT_HEADS * ATT_HEAD_DIM
GROUP_WIDTH = HEADS_PER_GROUP * ATT_HEAD_DIM
ATT_RADIUS = 64
XATT_HEADS = 4
XATT_HEAD_DIM = 128
XATT_WIDTH = XATT_HEADS * XATT_HEAD_DIM
N_EXPERTS = 16
EC_CAPACITY = 2
ALIBI_MAX_EXP = 8.0
EPS = 1e-6
NEG_INF = -1e30

C_CB, C_CC, C_AQ, C_XQ, C_GATE, C_END = 0, 768, 2304, 4608, 5120, 8192

V7X_VMEM_LIMIT_BYTES = 56 * 1024 * 1024
LANES = 128
HALO = 16

PROJ_TILE = 512
MERGE_SPLIT = 2
ATT_QB = 128
ATT_UNITS = 16
ATT_BLOCK_BYTES = 28 * 1024 * 1024
ROUTE_TILE = 256
ROUTE_CHUNK = 64
SEG_ALIGN = 16
FFN_TILE = 512
FFN_COLS = 256
XE_W = D_MODEL + LANES


def _dot(a, b):
    return jnp.dot(a, b, preferred_element_type=F32)


def _dot_nt(a, b):
    return lax.dot_general(a, b, (((1,), (1,)), ((), ())), preferred_element_type=F32)


def _params(sem):
    return pltpu.CompilerParams(dimension_semantics=sem, vmem_limit_bytes=V7X_VMEM_LIMIT_BYTES)


def _full(shape):
    return pl.BlockSpec(shape, lambda *_: (0,) * len(shape))


def _resident(shape):
    return pl.BlockSpec(shape, lambda *_: (0,) * len(shape), pipeline_mode=pl.Buffered(1))


def _memkv_kernel(mem_ref, g_ref, w_ref, kg_ref, k_ref, v_ref):
    seqs, rows, d = mem_ref.shape
    m = mem_ref[...].reshape(seqs * rows, d)
    hn = (m * lax.rsqrt(jnp.mean(m * m, axis=-1, keepdims=True) + EPS) * g_ref[...]).astype(BF16)
    kv = _dot(hn, w_ref[...])
    ks = []
    for h in range(XATT_HEADS):
        kh = kv[:, h * XATT_HEAD_DIM:(h + 1) * XATT_HEAD_DIM]
        ks.append(kh * lax.rsqrt(jnp.mean(kh * kh, axis=-1, keepdims=True) + EPS) * kg_ref[...])
    k_ref[...] = jnp.concatenate(ks, axis=1).astype(BF16).reshape(seqs, rows, XATT_WIDTH)
    v_ref[...] = kv[:, XATT_WIDTH:].astype(BF16).reshape(seqs, rows, XATT_WIDTH)


def _memkv(mem, mem_g, w_mem_kv, xk_g):
    b, m, d = mem.shape
    seqs = max(s for s in (1, 2, 4) if b % s == 0)
    out = jax.ShapeDtypeStruct((b, m, XATT_WIDTH), BF16)
    return pl.pallas_call(
        _memkv_kernel,
        grid=(b // seqs,),
        in_specs=[pl.BlockSpec((seqs, m, d), lambda i: (i, 0, 0)), _full((1, d)),
                  _full((d, 2 * XATT_WIDTH)), _full((1, XATT_HEAD_DIM))],
        out_specs=[pl.BlockSpec((seqs, m, XATT_WIDTH), lambda i: (i, 0, 0))] * 2,
        out_shape=[out, out],
        compiler_params=_params(("arbitrary",)),
        name="memkv",
    )(mem, mem_g, w_mem_kv, xk_g)


def _proj_kernel(xp_ref, x_ref, xn_ref, n1g_ref, win_ref, cw_ref, qg_ref, kg_ref, xqg_ref, kx_ref, vx_ref,
                 wbc_ref, wbx_ref, bd_ref,
                 q0_ref, q1_ref, q2_ref, k0_ref, k1_ref, k2_ref, v0_ref, v1_ref, v2_ref, p_ref, g1_ref,
                 hb_s, u_s, il_s, *, tile, tiles_per_seq):
    tin = pl.program_id(0) % tiles_per_seq
    gain = n1g_ref[...]

    def nrm(x):
        return (x * lax.rsqrt(jnp.mean(x * x, axis=-1, keepdims=True) + EPS) * gain).astype(BF16)

    hb_s[0:HALO, :] = nrm(xp_ref[...])
    hb_s[HALO:HALO + tile, :] = nrm(x_ref[...])
    hb_s[HALO + tile:, :] = nrm(xn_ref[...])
    hc = hb_s[HALO:HALO + tile, :]

    ccx = _dot(hb_s[...], win_ref[:, C_CC:C_AQ])
    u = ccx[:, :CONV_WIDTH] * ccx[:, CONV_WIDTH:]
    u_s[0:HALO, :] = u[0:HALO] * jnp.where(tin == 0, 0.0, 1.0)
    u_s[HALO:HALO + tile, :] = u[HALO:HALO + tile]
    u_s[HALO + tile:, :] = u[HALO + tile:] * jnp.where(tin == tiles_per_seq - 1, 0.0, 1.0)
    cw = cw_ref[...]
    conv = (cw[0:1] * u_s[HALO - 1:HALO - 1 + tile, :] + cw[1:2] * u_s[HALO:HALO + tile, :]
            + cw[2:3] * u_s[HALO + 1:HALO + 1 + tile, :])
    cb = _dot(hc, win_ref[:, C_CB:C_CC])
    y_conv = _dot((cb * conv).astype(BF16), wbc_ref[...])

    qkv = _dot(hc, win_ref[:, C_AQ:C_XQ])
    bd = bd_ref[...]

    def head_norm(z, g_ref):
        outs = []
        for c in range(len(DIL_GROUPS)):
            zc = z[:, c * GROUP_WIDTH:(c + 1) * GROUP_WIDTH]
            ms = _dot((zc * zc).astype(BF16), bd)
            outs.append(zc * lax.rsqrt(ms + EPS) * g_ref[...])
        return outs

    def emit(ref, val, dilation):
        if dilation == 1:
            ref[...] = val.astype(BF16)
            return
        for half in range(GROUP_WIDTH // LANES):
            il_s[half] = val[:, half * LANES:(half + 1) * LANES]
        for r in range(dilation):
            for half in range(GROUP_WIDTH // LANES):
                col = r * GROUP_WIDTH + half * LANES
                ref[:, col:col + LANES] = il_s[half, pl.ds(r, tile // dilation, stride=dilation), :].astype(BF16)

    vals = (head_norm(qkv[:, :ATT_WIDTH], qg_ref) + head_norm(qkv[:, ATT_WIDTH:2 * ATT_WIDTH], kg_ref)
            + [qkv[:, 2 * ATT_WIDTH + c * GROUP_WIDTH:2 * ATT_WIDTH + (c + 1) * GROUP_WIDTH] for c in range(len(DIL_GROUPS))])
    refs = (q0_ref, q1_ref, q2_ref, k0_ref, k1_ref, k2_ref, v0_ref, v1_ref, v2_ref)
    for n, (ref, val) in enumerate(zip(refs, vals)):
        emit(ref, val, DIL_GROUPS[n % len(DIL_GROUPS)][1])

    xq = _dot(hc, win_ref[:, C_XQ:C_GATE])
    kx = kx_ref[...]
    vx = vx_ref[...]
    outs = []
    for h in range(XATT_HEADS):
        hs = slice(h * XATT_HEAD_DIM, (h + 1) * XATT_HEAD_DIM)
        qh = xq[:, hs]
        qh = qh * lax.rsqrt(jnp.mean(qh * qh, axis=-1, keepdims=True) + EPS) * xqg_ref[...]
        s = _dot_nt(qh.astype(BF16), kx[:, hs]) * (XATT_HEAD_DIM ** -0.5)
        p = jnp.exp(s - jnp.max(s, axis=-1, keepdims=True))
        den = jnp.sum(p, axis=-1, keepdims=True)
        outs.append(_dot(p.astype(BF16), vx[:, hs]) / den)
    y_x = _dot(jnp.concatenate(outs, axis=1).astype(BF16), wbx_ref[...])

    gs = jax.nn.sigmoid(_dot(hc, win_ref[:, C_GATE:C_END]))
    p_ref[...] = (gs[:, :D_MODEL] * y_conv + gs[:, 2 * D_MODEL:] * y_x).astype(BF16)
    g1_ref[...] = gs[:, D_MODEL:2 * D_MODEL].astype(BF16)


def _proj(x2, seq_len, kx, vx, w):
    n, d = x2.shape
    tile = PROJ_TILE
    tps = seq_len // tile
    hb = tile // HALO
    n_hblk = n // HALO
    wide = jax.ShapeDtypeStruct((n, d), BF16)
    row = lambda width: pl.BlockSpec((tile, width), lambda i: (i, 0))
    dils = [dil for _, dil in DIL_GROUPS] * 3
    grp_shapes = [jax.ShapeDtypeStruct((n // dil, dil * GROUP_WIDTH), BF16) for dil in dils]
    grp_specs = [pl.BlockSpec((tile // dil, dil * GROUP_WIDTH), lambda i: (i, 0)) for dil in dils]
    return pl.pallas_call(
        functools.partial(_proj_kernel, tile=tile, tiles_per_seq=tps),
        grid=(n // tile,),
        in_specs=[
            pl.BlockSpec((HALO, d), lambda i: (jnp.maximum(i * hb - 1, 0), 0)),
            row(d),
            pl.BlockSpec((HALO, d), lambda i: (jnp.minimum((i + 1) * hb, n_hblk - 1), 0)),
            _full((1, d)), _resident((d, C_END)), _full((3, CONV_WIDTH)),
            _full((1, GROUP_WIDTH)), _full((1, GROUP_WIDTH)), _full((1, XATT_HEAD_DIM)),
            pl.BlockSpec((None, N_MEM, XATT_WIDTH), lambda i: (i // tps, 0, 0)),
            pl.BlockSpec((None, N_MEM, XATT_WIDTH), lambda i: (i // tps, 0, 0)),
            _resident((CONV_WIDTH, d)), _resident((XATT_WIDTH, d)), _full((GROUP_WIDTH, GROUP_WIDTH)),
        ],
        out_specs=grp_specs + [row(d), row(d)],
        out_shape=grp_shapes + [wide, wide],
        scratch_shapes=[pltpu.VMEM((tile + 2 * HALO, d), BF16), pltpu.VMEM((tile + 2 * HALO, CONV_WIDTH), F32),
                        pltpu.VMEM((GROUP_WIDTH // LANES, tile, LANES), F32)],
        compiler_params=_params(("arbitrary",)),
        name="proj",
    )(x2, x2, x2, w["norm1_g"], w["w_in"], w["conv_w"], w["q_gain"], w["k_gain"], w["xq_gain"], kx, vx,
      w["w_br_conv"], w["w_br_xattn"], w["head_avg"])


def _attn_kernel(q_ref, k_ref, v_ref, bias_ref, o_ref, lse_ref, *, seq_sub, lq, qb, kw, n_res):
    i = pl.program_id(2)
    lane_head = lax.broadcasted_iota(I32, (qb, GROUP_WIDTH), 1) // ATT_HEAD_DIM
    head_mask = [lane_head == h for h in range(HEADS_PER_GROUP)]
    head_mask_bf = [jnp.where(m, 1.0, 0.0).astype(BF16) for m in head_mask]

    def block(sb, carry):
        row0 = pl.multiple_of(sb * qb, qb)
        qs = i * lq + row0
        ks = pl.multiple_of(jnp.clip(qs - ATT_RADIUS, 0, seq_sub - kw), ATT_RADIUS)
        case = jnp.where(qs == 0, 0, jnp.where(qs == seq_sub - qb, 2, 1))
        for r in range(n_res):
            cs = slice(r * GROUP_WIDTH, (r + 1) * GROUP_WIDTH)
            q = q_ref[pl.ds(row0, qb), cs]
            kk = k_ref[pl.ds(ks, kw), cs]
            vv = v_ref[pl.ds(ks, kw), cs]
            s = _dot_nt(jnp.concatenate([q * m for m in head_mask_bf], axis=0), kk)
            ps, inv_den, lse = [], [], []
            for h in range(HEADS_PER_GROUP):
                sh = s[h * qb:(h + 1) * qb] + bias_ref[case * HEADS_PER_GROUP + h]
                m = jnp.max(sh, axis=-1, keepdims=True)
                p = jnp.exp(sh - m)
                den = jnp.sum(p, axis=-1, keepdims=True)
                ps.append(p.astype(BF16))
                inv_den.append(1.0 / den)
                lse.append(m + jnp.log(den))
            of = _dot(jnp.concatenate(ps, axis=0), vv)
            o = jnp.zeros((qb, GROUP_WIDTH), F32)
            lb = jnp.zeros((qb, GROUP_WIDTH), F32)
            for h in range(HEADS_PER_GROUP):
                o = jnp.where(head_mask[h], of[h * qb:(h + 1) * qb] * inv_den[h], o)
                lb = jnp.where(head_mask[h], lse[h], lb)
            o_ref[pl.ds(row0, qb), cs] = o.astype(BF16)
            lse_ref[pl.ds(row0, qb), cs] = lb
        return carry

    lax.fori_loop(0, lq // qb, block, 0, unroll=max(1, min(lq // qb, ATT_UNITS // n_res)))


def _attn_bias(dilation, qb, kw, group):
    slopes = jnp.exp2(-ALIBI_MAX_EXP * jnp.arange(1, N_ATT_HEADS + 1, dtype=F32) / N_ATT_HEADS)
    slopes = slopes[group * HEADS_PER_GROUP:(group + 1) * HEADS_PER_GROUP]
    a = jnp.arange(qb)[:, None]
    c = jnp.arange(kw)[None, :]
    tabs = []
    for off in (0, -ATT_RADIUS, qb - kw):
        delta = off + c - a
        dist = (dilation * jnp.abs(delta)).astype(F32)
        bias = -slopes[:, None, None] * dist[None]
        tabs.append(jnp.where((jnp.abs(delta) <= ATT_RADIUS)[None], bias, NEG_INF))
    return jnp.concatenate(tabs, axis=0)


def _attn(q, k, v, batch, seq_len, group):
    _, dilation = DIL_GROUPS[group]
    seq_sub = seq_len // dilation
    qb = min(ATT_QB, seq_sub)
    kw = min(qb + 2 * ATT_RADIUS, seq_sub)
    lq = min(ATT_UNITS * qb, seq_sub)
    bytes_per_res = 2 * GROUP_WIDTH * (lq * (2 + 2 + 4) + seq_sub * (2 + 2))
    n_res = dilation
    while n_res > 1 and n_res * bytes_per_res > ATT_BLOCK_BYTES:
        n_res //= 2
    width = n_res * GROUP_WIDTH
    view = lambda t: t.reshape(batch, seq_sub, dilation * GROUP_WIDTH)
    qspec = pl.BlockSpec((None, lq, width), lambda b, r, i: (b, i, r))
    kspec = pl.BlockSpec((None, seq_sub, width), lambda b, r, i: (b, 0, r))
    bias = _attn_bias(dilation, qb, kw, group)
    o, lse = pl.pallas_call(
        functools.partial(_attn_kernel, seq_sub=seq_sub, lq=lq, qb=qb, kw=kw, n_res=n_res),
        grid=(batch, dilation // n_res, seq_sub // lq),
        in_specs=[qspec, kspec, kspec, _full(bias.shape)],
        out_specs=[qspec, qspec],
        out_shape=[jax.ShapeDtypeStruct((batch, seq_sub, dilation * GROUP_WIDTH), BF16),
                   jax.ShapeDtypeStruct((batch, seq_sub, dilation * GROUP_WIDTH), F32)],
        compiler_params=_params(("arbitrary", "arbitrary", "arbitrary")),
        name=f"attn_d{dilation}",
    )(view(q), view(k), view(v), bias)
    rows = batch * seq_sub
    return o.reshape(rows, dilation * GROUP_WIDTH), lse.reshape(rows, dilation * GROUP_WIDTH)


def _merge_kernel(x_ref, p_ref, g1_ref, o0_ref, o1_ref, o2_ref, l0_ref, l1_ref, l2_ref, wba_ref, wo_ref,
                  n2g_ref, wr_ref, x1_ref, h2e_ref, afft_ref, *il_s, tile):
    halves = GROUP_WIDTH // LANES
    dils = [dil for _, dil in DIL_GROUPS] * 2
    grp_refs = (o0_ref, o1_ref, o2_ref, l0_ref, l1_ref, l2_ref)

    for ref, dilation, scratch in zip(grp_refs, dils, il_s):
        for r in range(dilation if dilation > 1 else 0):
            for half in range(halves):
                col = r * GROUP_WIDTH + half * LANES
                scratch[half, pl.ds(r, tile // dilation, stride=dilation), :] = ref[:, col:col + LANES].astype(F32)

    def token_order(n, rs):
        if dils[n] == 1:
            return grp_refs[n][rs, :].astype(F32)
        return jnp.concatenate([il_s[n][half, rs, :] for half in range(halves)], axis=1)

    rows = tile // MERGE_SPLIT
    blocks = [slice(blk * rows, (blk + 1) * rows) for blk in range(MERGE_SPLIT)]

    def mixture(rs):
        o0, o1, o2, l0, l1, l2 = (token_order(n, rs) for n in range(6))
        m = jnp.maximum(jnp.maximum(l0, l1), l2)
        e0, e1, e2 = jnp.exp(l0 - m), jnp.exp(l1 - m), jnp.exp(l2 - m)
        return ((e0 * o0 + e1 * o1 + e2 * o2) / (e0 + e1 + e2)).astype(BF16)

    mixed = [mixture(rs) for rs in blocks]
    y_attn = [_dot(o, wba_ref[...]) for o in mixed]
    merged = [(p_ref[rs, :].astype(F32) + g1_ref[rs, :].astype(F32) * y).astype(BF16) for rs, y in zip(blocks, y_attn)]
    x1s = [x_ref[rs, :] + _dot(mg, wo_ref[...]) for rs, mg in zip(blocks, merged)]
    h2s = []
    for rs, x1 in zip(blocks, x1s):
        x1_ref[rs, :] = x1
        h2 = (x1 * lax.rsqrt(jnp.mean(x1 * x1, axis=-1, keepdims=True) + EPS) * n2g_ref[...]).astype(BF16)
        h2e_ref[rs, :D_MODEL] = h2
        h2s.append(h2)
    logits = [_dot(h2, wr_ref[...]) for h2 in h2s]
    for rs, lg in zip(blocks, logits):
        lane = lax.broadcasted_iota(I32, lg.shape, 1)
        first = lane < N_EXPERTS
        mx = jnp.max(jnp.where(first, lg, -jnp.inf), axis=-1, keepdims=True)
        ex = jnp.exp(lg - mx)
        a = ex / jnp.sum(jnp.where(first, ex, 0.0), axis=-1, keepdims=True)
        afft_ref[:, rs] = a.T[:N_EXPERTS, :]
        hi = a.astype(BF16).astype(F32)
        mid = (a - hi).astype(BF16).astype(F32)
        lo = (a - hi) - mid
        ext = jnp.where(first, hi, jnp.where(lane < 2 * N_EXPERTS, mid, jnp.where(lane < 3 * N_EXPERTS, lo, 0.0)))
        h2e_ref[rs, D_MODEL:] = ext.astype(BF16)


def _merge(x2, p, g1, os_, ls_, w):
    n, d = x2.shape
    tile = PROJ_TILE
    row = lambda width: pl.BlockSpec((tile, width), lambda i: (i, 0))
    grp = [pl.BlockSpec((tile // dil, dil * GROUP_WIDTH), lambda i: (i, 0)) for _, dil in DIL_GROUPS]
    return pl.pallas_call(
        functools.partial(_merge_kernel, tile=tile),
        grid=(n // tile,),
        in_specs=[row(d), row(d), row(d)] + grp + grp
                 + [_full((GROUP_WIDTH, d)), _full((d, d)), _full((1, d)), _full((d, LANES))],
        out_specs=[row(d), row(d + LANES), pl.BlockSpec((N_EXPERTS, tile), lambda i: (0, i))],
        out_shape=[jax.ShapeDtypeStruct((n, d), F32), jax.ShapeDtypeStruct((n, d + LANES), BF16),
                   jax.ShapeDtypeStruct((N_EXPERTS, n), F32)],
        scratch_shapes=[pltpu.VMEM((GROUP_WIDTH // LANES, tile, LANES), F32)] * 6,
        compiler_params=_params(("arbitrary",)),
        name="merge",
    )(x2, p, g1, *os_, *ls_, w["w_br_attn"], w["w_o"], w["norm2_g"], w["w_router3"])


def _select_kernel(aff_ref, slot_ref, starts_ref, nchunk_ref, tot_ref, *, n_tok, cap, tile):
    n_tiles = n_tok // tile

    def bisect(b, thr_bits):
        cand = thr_bits | jnp.left_shift(jnp.int32(1), 30 - b)
        cnt = jnp.sum(jnp.where(aff_ref[...] >= lax.bitcast_convert_type(cand, F32), 1.0, 0.0), axis=1, keepdims=True)
        return jnp.where(cnt >= cap, cand, thr_bits)

    thr = lax.bitcast_convert_type(lax.fori_loop(0, 31, bisect, jnp.zeros((N_EXPERTS, 1), I32)), F32)
    need = cap - jnp.sum(jnp.where(aff_ref[...] > thr, 1.0, 0.0), axis=1, keepdims=True)

    r = lax.broadcasted_iota(I32, (tile, tile), 0)
    c = lax.broadcasted_iota(I32, (tile, tile), 1)
    before = jnp.where(r < c, 1.0, 0.0).astype(BF16)

    def tile_body(i, carry):
        start, eq_seen = carry
        off = pl.multiple_of(i * tile, tile)
        a = aff_ref[:, pl.ds(off, tile)]
        eq = jnp.where(a == thr, 1.0, 0.0)
        eq_rank = eq_seen + _dot(eq.astype(BF16), before)
        sel = jnp.where(a > thr, 1.0, jnp.where(eq_rank < need, eq, 0.0))
        rank = _dot(sel.astype(BF16), before)
        slot_ref[:, pl.ds(off, tile)] = jnp.where(sel > 0.0, rank, -1.0).astype(I32)
        cnt = jnp.sum(sel, axis=1, keepdims=True)
        starts_ref[i] = jnp.broadcast_to(start, (N_EXPERTS, LANES)).astype(I32)
        head = start - jnp.floor(start * (1.0 / SEG_ALIGN)) * SEG_ALIGN
        nch = jnp.max(jnp.floor((head + cnt) * (1.0 / ROUTE_CHUNK)) + 1.0, axis=0, keepdims=True)
        nchunk_ref[i] = jnp.broadcast_to(nch, (8, LANES)).astype(I32)
        return start + cnt, eq_seen + jnp.sum(eq, axis=1, keepdims=True)

    zero = jnp.zeros((N_EXPERTS, 1), F32)
    total, _ = lax.fori_loop(0, n_tiles, tile_body, (zero, zero), unroll=8)
    tot_ref[...] = jnp.broadcast_to(total, (N_EXPERTS, LANES)).astype(I32)


def _select(aff_t, cap):
    n_exp, n = aff_t.shape
    tile = ROUTE_TILE
    n_tiles = n // tile
    slot, starts, nchunk, tot = pl.pallas_call(
        functools.partial(_select_kernel, n_tok=n, cap=cap, tile=tile),
        grid=(1,),
        in_specs=[_full((n_exp, n))],
        out_specs=[_full((n_exp, n)), _full((n_tiles, n_exp, LANES)), _full((n_tiles, 8, LANES)), _full((n_exp, LANES))],
        out_shape=[jax.ShapeDtypeStruct((n_exp, n), I32), jax.ShapeDtypeStruct((n_tiles, n_exp, LANES), I32),
                   jax.ShapeDtypeStruct((n_tiles, 8, LANES), I32), jax.ShapeDtypeStruct((n_exp, LANES), I32)],
        compiler_params=_params(("arbitrary",)),
        name="select",
    )(aff_t)
    return slot, starts[:, :, 0], nchunk[:, 0, 0], tot[:, 0]


def _seg_base(starts_ref, step, e):
    return pl.multiple_of((starts_ref[step * N_EXPERTS + e] // SEG_ALIGN) * SEG_ALIGN, SEG_ALIGN)


def _dispatch_kernel(starts_ref, nchunk_ref, slot_ref, h_ref, scol_ref, xe_hbm, stage, head, sem, *, tile, chunk):
    i = pl.program_id(0)
    sl = i % 2

    def copies(step, c, buf):
        return [pltpu.make_async_copy(
            stage.at[buf, pl.ds(e * chunk, chunk), :],
            xe_hbm.at[e, pl.ds(_seg_base(starts_ref, step, e) + c * chunk, chunk), :],
            sem.at[buf]) for e in range(N_EXPERTS)]

    @pl.when(i == 0)
    def _():
        head[...] = jnp.zeros(head.shape, BF16)

    slot = slot_ref[...]
    pos = jnp.where(slot >= 0, slot + (scol_ref[...] & (SEG_ALIGN - 1)), -1)

    def build(c, buf):
        want = lax.broadcasted_iota(I32, (chunk, tile), 0) + c * chunk
        onehot = jnp.concatenate(
            [jnp.where(pos[e:e + 1, :] == want, 1.0, 0.0).astype(BF16) for e in range(N_EXPERTS)], axis=0)
        stage[buf] = _dot(onehot, h_ref[...]).astype(BF16)

    nxt = jnp.minimum(i + 1, pl.num_programs(0) - 1)
    rel = [starts_ref[nxt * N_EXPERTS + e] - _seg_base(starts_ref, i, e) for e in range(N_EXPERTS)]

    def keep_head(e, buf):
        row = pl.multiple_of(e * chunk + ((rel[e] % chunk) // SEG_ALIGN) * SEG_ALIGN, SEG_ALIGN)
        head[e] = stage[buf, pl.ds(row, SEG_ALIGN), :]

    build(0, sl)
    for e in range(N_EXPERTS):
        stage[sl, e * chunk:e * chunk + SEG_ALIGN, :] = stage[sl, e * chunk:e * chunk + SEG_ALIGN, :] + head[e]
    for e in range(N_EXPERTS):
        keep_head(e, sl)

    @pl.when(i > 0)
    def _():
        for cp in copies(i - 1, 0, 1 - sl):
            cp.wait()

    for cp in copies(i, 0, sl):
        cp.start()

    def overflow(c, carry):
        for cp in copies(i, c - 1, sl):
            cp.wait()
        build(c, sl)
        for e in range(N_EXPERTS):
            @pl.when(rel[e] // chunk == c)
            def _():
                keep_head(e, sl)
        for cp in copies(i, c, sl):
            cp.start()
        return carry

    lax.fori_loop(1, nchunk_ref[i], overflow, 0)

    @pl.when(i == pl.num_programs(0) - 1)
    def _():
        for cp in copies(i, 0, sl):
            cp.wait()
        rows = xe_hbm.shape[1]
        zbuf = 1 - sl
        stage[zbuf] = jnp.zeros(stage.shape[1:], BF16)
        end = [_seg_base(starts_ref, i, e) + jnp.maximum(nchunk_ref[i], 1) * chunk for e in range(N_EXPERTS)]
        n_full = [(rows - end[e]) // chunk for e in range(N_EXPERTS)]

        def zero_copy(e, pos):
            return pltpu.make_async_copy(stage.at[zbuf, pl.ds(e * chunk, chunk), :],
                                         xe_hbm.at[e, pl.ds(pl.multiple_of(pos, SEG_ALIGN), chunk), :], sem.at[zbuf])

        def fill(wait):
            def body(k, carry):
                for e in range(N_EXPERTS):
                    @pl.when(k < n_full[e])
                    def _():
                        cp = zero_copy(e, end[e] + k * chunk)
                        cp.wait() if wait else cp.start()
                return carry
            return body

        most = functools.reduce(jnp.maximum, n_full)
        lax.fori_loop(0, most, fill(wait=False), 0)
        lax.fori_loop(0, most, fill(wait=True), 0)
        for e in range(N_EXPERTS):
            zero_copy(e, rows - chunk).start()
        for e in range(N_EXPERTS):
            zero_copy(e, rows - chunk).wait()


def _dispatch(slot, h2e, starts, nchunk, rows):
    n = h2e.shape[0]
    tile, chunk = ROUTE_TILE, ROUTE_CHUNK
    n_tiles = n // tile
    return pl.pallas_call(
        functools.partial(_dispatch_kernel, tile=tile, chunk=chunk),
        grid_spec=pltpu.PrefetchScalarGridSpec(
            num_scalar_prefetch=2, grid=(n_tiles,),
            in_specs=[pl.BlockSpec((N_EXPERTS, tile), lambda i, *_: (0, i)),
                      pl.BlockSpec((tile, D_MODEL + LANES), lambda i, *_: (i, 0)),
                      pl.BlockSpec((None, N_EXPERTS, 1), lambda i, *_: (i, 0, 0))],
            out_specs=pl.BlockSpec(memory_space=pl.ANY),
            scratch_shapes=[pltpu.VMEM((2, N_EXPERTS * chunk, XE_W), BF16),
                            pltpu.VMEM((N_EXPERTS, SEG_ALIGN, XE_W), BF16), pltpu.SemaphoreType.DMA((2,))]),
        out_shape=jax.ShapeDtypeStruct((N_EXPERTS, rows, XE_W), BF16),
        compiler_params=_params(("arbitrary",)),
        name="dispatch",
    )(starts.reshape(-1), nchunk, slot, h2e, starts.reshape(n_tiles, N_EXPERTS, 1))


def _ffn_kernel(tot_ref, *refs, tile, first_tile):
    n_grp = len(first_tile) - 1
    xe_refs, (wg_ref, wu_ref, wd_ref) = refs[:n_grp], refs[n_grp:n_grp + 3]
    ye_refs, (wg_s, wu_s, wd_s) = refs[n_grp + 3:2 * n_grp + 3], refs[2 * n_grp + 3:]
    e = pl.program_id(0)
    j = pl.program_id(1)

    def weights(fresh):
        def load(w_ref, w_s, idx):
            if not fresh:
                return w_s[idx]
            w = w_ref[idx].astype(BF16)
            w_s[idx] = w
            return w
        return load

    def run(xe_ref, ye_ref, live, fresh):
        load = weights(fresh)

        @pl.when(live > 0)
        def _():
            x = xe_ref[:, :D_MODEL]
            pieces = xe_ref[:, D_MODEL:].astype(F32)
            lane = lax.broadcasted_iota(I32, pieces.shape, 1)
            mine = ((lane & (N_EXPERTS - 1)) == e) & (lane < 3 * N_EXPERTS)
            gate = jnp.sum(jnp.where(mine, pieces, 0.0), axis=-1, keepdims=True)
            hid = []
            for fc in range(wg_s.shape[1] // FFN_COLS):
                cs = (slice(None), slice(fc * FFN_COLS, (fc + 1) * FFN_COLS))
                g = _dot(x, load(wg_ref, wg_s, cs))
                u = _dot(x, load(wu_ref, wu_s, cs))
                hid.append((g * jax.nn.sigmoid(g) * u).astype(BF16))
            y = _dot(jnp.concatenate(hid, axis=1), load(wd_ref, wd_s, (slice(None), slice(None)))) * gate
            rows = lax.broadcasted_iota(I32, y.shape, 0)
            ye_ref[...] = jnp.where(rows < live, y, 0.0).astype(BF16)

        @pl.when(live <= 0)
        def _():
            ye_ref[...] = jnp.zeros(ye_ref.shape, BF16)

    for g in range(n_grp):
        live = tot_ref[g * N_EXPERTS + e] - (j - first_tile[g]) * tile
        if g == 0:
            @pl.when(j == 0)
            def _():
                run(xe_refs[0], ye_refs[0], live, fresh=True)

        @pl.when((j >= max(first_tile[g], 1)) & (j < first_tile[g + 1]))
        def _():
            run(xe_refs[g], ye_refs[g], live, fresh=False)


def _ffn(xes, tots, w_gate, w_up, w_down):
    tile = FFN_TILE
    n_exp, d, f = w_gate.shape
    n_tiles = [xe.shape[1] // tile for xe in xes]
    first_tile = [sum(n_tiles[:g]) for g in range(len(xes) + 1)]

    def xe_spec(g):
        def index(e, j, tot):
            last = jnp.maximum(tot[g * N_EXPERTS + e] - 1, 0) // tile
            return (e, jnp.clip(j - first_tile[g], 0, last), 0)
        return pl.BlockSpec((None, tile, XE_W), index)

    def ye_spec(g):
        return pl.BlockSpec((None, tile, d), lambda e, j, tot: (e, jnp.clip(j - first_tile[g], 0, n_tiles[g] - 1), 0))

    wspec = lambda a, b: pl.BlockSpec((None, a, b), lambda e, j, tot: (e, 0, 0))
    return pl.pallas_call(
        functools.partial(_ffn_kernel, tile=tile, first_tile=tuple(first_tile)),
        grid_spec=pltpu.PrefetchScalarGridSpec(
            num_scalar_prefetch=1, grid=(n_exp, first_tile[-1]),
            in_specs=[xe_spec(g) for g in range(len(xes))] + [wspec(d, f), wspec(d, f), wspec(f, d)],
            out_specs=[ye_spec(g) for g in range(len(xes))],
            scratch_shapes=[pltpu.VMEM((d, f), BF16), pltpu.VMEM((d, f), BF16), pltpu.VMEM((f, d), BF16)]),
        out_shape=[jax.ShapeDtypeStruct((n_exp, xe.shape[1], d), BF16) for xe in xes],
        compiler_params=_params(("arbitrary", "arbitrary")),
        name="ffn",
    )(jnp.concatenate(tots), *xes, w_gate, w_up, w_down)


def _combine_kernel(starts_ref, nchunk_ref, x1_ref, slot_ref, srow_ref, ye_hbm, out_ref, stage, sem, *, tile, chunk):
    i = pl.program_id(0)
    sl = i % 2
    over = 2

    def copies(step, c, buf):
        return [pltpu.make_async_copy(
            ye_hbm.at[e, pl.ds(_seg_base(starts_ref, step, e) + c * chunk, chunk), :],
            stage.at[buf, pl.ds(e * chunk, chunk), :],
            sem.at[buf]) for e in range(N_EXPERTS)]

    @pl.when(i == 0)
    def _():
        for cp in copies(0, 0, 0):
            cp.start()

    @pl.when(i + 1 < pl.num_programs(0))
    def _():
        for cp in copies(i + 1, 0, 1 - sl):
            cp.start()

    col = lax.broadcasted_iota(I32, (N_EXPERTS, N_EXPERTS * chunk), 1)
    exp_row = lax.broadcasted_iota(I32, (N_EXPERTS, N_EXPERTS * chunk), 0)
    spread = jnp.where(col // chunk == exp_row, 1.0, 0.0).astype(BF16)
    slot = slot_ref[...]
    rank = jnp.where(slot < 0, -float(tile), slot.astype(F32)).astype(BF16)
    head = jnp.broadcast_to((srow_ref[...] & (SEG_ALIGN - 1)).astype(F32).astype(BF16), (8, N_EXPERTS))
    slots = _dot(rank, spread) + _dot(head, spread)[0:1]
    lane_slot = (lax.broadcasted_iota(I32, slots.shape, 1) & (chunk - 1)).astype(F32)

    def gathered(c, buf):
        onehot = jnp.where(slots == lane_slot + c * chunk, 1.0, 0.0).astype(BF16)
        return _dot(onehot, stage[buf])

    for cp in copies(i, 0, sl):
        cp.wait()
    out_ref[...] = x1_ref[...] + gathered(0, sl)

    def overflow(c, carry):
        for cp in copies(i, c, over):
            cp.start()
        for cp in copies(i, c, over):
            cp.wait()
        out_ref[...] += gathered(c, over)
        return carry

    lax.fori_loop(1, nchunk_ref[i], overflow, 0)


def _combine(x1, slot_t, ye, starts, nchunk):
    n, d = x1.shape
    tile, chunk = ROUTE_TILE, ROUTE_CHUNK
    n_tiles = n // tile
    return pl.pallas_call(
        functools.partial(_combine_kernel, tile=tile, chunk=chunk),
        grid_spec=pltpu.PrefetchScalarGridSpec(
            num_scalar_prefetch=2, grid=(n_tiles,),
            in_specs=[pl.BlockSpec((tile, d), lambda i, *_: (i, 0)),
                      pl.BlockSpec((tile, N_EXPERTS), lambda i, *_: (i, 0)),
                      pl.BlockSpec((None, 1, N_EXPERTS), lambda i, *_: (i, 0, 0)),
                      pl.BlockSpec(memory_space=pl.ANY)],
            out_specs=pl.BlockSpec((tile, d), lambda i, *_: (i, 0)),
            scratch_shapes=[pltpu.VMEM((3, N_EXPERTS * chunk, d), BF16), pltpu.SemaphoreType.DMA((3,))]),
        out_shape=jax.ShapeDtypeStruct((n, d), F32),
        compiler_params=_params(("arbitrary",)),
        name="combine",
    )(starts.reshape(-1), nchunk, x1, slot_t, starts.reshape(n_tiles, 1, N_EXPERTS), ye)


def _expert_rows(cap):
    need = cap + ROUTE_TILE + 2 * ROUTE_CHUNK
    return -(-need // FFN_TILE) * FFN_TILE


def _routed_ffn(groups, w):
    routes = []
    for x1, h2e, aff_t in groups:
        cap = max(1, EC_CAPACITY * x1.shape[0] // N_EXPERTS)
        slot, starts, nchunk, tot = _select(aff_t, cap)
        routes.append((slot, starts, nchunk, tot, _dispatch(slot, h2e, starts, nchunk, _expert_rows(cap))))
    yes = _ffn([r[4] for r in routes], [r[3] for r in routes], w["w_exp_gate"], w["w_exp_up"], w["w_exp_down"])
    return [_combine(x1, slot.T, ye, starts, nchunk)
            for (x1, _, _), (slot, starts, nchunk, _, _), ye in zip(groups, routes, yes)]


def _mixers(x, mem, w):
    b, s, d = x.shape
    x2 = x.reshape(b * s, d)
    kx, vx = _memkv(mem, w["mem_norm_g"], w["w_mem_kv"], w["xk_gain"])
    *qkv, p, g1 = _proj(x2, s, kx, vx, w)
    os_, ls_ = [], []
    for g in range(len(DIL_GROUPS)):
        o, lse = _attn(qkv[g], qkv[3 + g], qkv[6 + g], b, s, g)
        os_.append(o)
        ls_.append(lse)
    return _merge(x2, p, g1, os_, ls_, w)


def _layer(xs, mems, w):
    outs = _routed_ffn([_mixers(x, mem, w) for x, mem in zip(xs, mems)], w)
    return [o.reshape(x.shape) for o, x in zip(outs, xs)]


def _prepare(norm1_g, w_in, conv_w, q_norm_g, k_norm_g, mem_norm_g, w_mem_kv, xq_norm_g, xk_norm_g,
             w_br_conv, w_br_attn, w_br_xattn, w_o, norm2_g, w_router, w_exp_gate, w_exp_up, w_exp_down):
    row = lambda v: v.reshape(1, -1).astype(F32)
    head = jnp.arange(GROUP_WIDTH) // ATT_HEAD_DIM
    return {
        "norm1_g": row(norm1_g), "norm2_g": row(norm2_g), "mem_norm_g": row(mem_norm_g),
        "w_in": w_in.astype(BF16), "conv_w": conv_w.astype(F32),
        "q_gain": row(jnp.tile(q_norm_g, HEADS_PER_GROUP) * (ATT_HEAD_DIM ** -0.5)),
        "k_gain": row(jnp.tile(k_norm_g, HEADS_PER_GROUP)),
        "xq_gain": row(xq_norm_g), "xk_gain": row(xk_norm_g),
        "w_mem_kv": w_mem_kv.astype(BF16), "w_br_conv": w_br_conv.astype(BF16),
        "w_br_attn": w_br_attn.astype(BF16), "w_br_xattn": w_br_xattn.astype(BF16), "w_o": w_o.astype(BF16),
        "head_avg": jnp.where(head[:, None] == head[None, :], 1.0 / ATT_HEAD_DIM, 0.0).astype(BF16),
        "w_router3": jnp.concatenate([w_router] * 3 + [jnp.zeros((D_MODEL, LANES - 3 * N_EXPERTS), F32)],
                                     axis=1).astype(BF16),
        "w_exp_gate": w_exp_gate, "w_exp_up": w_exp_up, "w_exp_down": w_exp_down,
    }


def kernel(x_prompt, x_sample, mem_prompt, mem_sample, norm1_g, w_in, conv_w, q_norm_g, k_norm_g, mem_norm_g, w_mem_kv, xq_norm_g, xk_norm_g, w_br_conv, w_br_attn, w_br_xattn, w_o, norm2_g, w_router, w_exp_gate, w_exp_up, w_exp_down):
    per_layer = (norm1_g, w_in, conv_w, q_norm_g, k_norm_g, mem_norm_g, w_mem_kv, xq_norm_g, xk_norm_g,
                 w_br_conv, w_br_attn, w_br_xattn, w_o, norm2_g, w_router, w_exp_gate, w_exp_up, w_exp_down)
    ys = [x_prompt, x_sample]
    for layer in range(norm1_g.shape[0]):
        ys = _layer(ys, (mem_prompt, mem_sample), _prepare(*(t[layer] for t in per_layer)))
    return tuple(ys)
```

```python
import functools

import jax
import jax.numpy as jnp
from jax import lax
from jax.experimental import pallas as pl
from jax.experimental.pallas import tpu as pltpu

F32 = jnp.float32
BF16 = jnp.bfloat16
I32 = jnp.int32

D_MODEL = 1024
N_MEM = 256
CONV_WIDTH = 768
ATT_HEAD_DIM = 64
DIL_GROUPS = ((128, 1), (512, 4), (2048, 16))
HEADS_PER_GROUP = 4
N_ATT_HEADS = HEADS_PER_GROUP * len(DIL_GROUPS)
ATT_WIDTH = N_ATT_HEADS * ATT_HEAD_DIM
GROUP_WIDTH = HEADS_PER_GROUP * ATT_HEAD_DIM
ATT_RADIUS = 64
XATT_HEADS = 4
XATT_HEAD_DIM = 128
XATT_WIDTH = XATT_HEADS * XATT_HEAD_DIM
N_EXPERTS = 16
EC_CAPACITY = 2
ALIBI_MAX_EXP = 8.0
EPS = 1e-6
NEG_INF = -1e30

C_CB, C_CC, C_AQ, C_XQ, C_GATE, C_END = 0, 768, 2304, 4608, 5120, 8192

V7X_VMEM_LIMIT_BYTES = 56 * 1024 * 1024
LANES = 128
HALO = 16

PROJ_TILE = 512
MERGE_SPLIT = 2
ATT_QB = 128
ATT_UNITS = 16
ATT_BLOCK_BYTES = 28 * 1024 * 1024
ROUTE_TILE = 256
ROUTE_CHUNK = 64
SEG_ALIGN = 16
FFN_TILE = 512
FFN_COLS = 256
XE_W = D_MODEL + LANES


def _dot(a, b):
    return jnp.dot(a, b, preferred_element_type=F32)


def _dot_nt(a, b):
    return lax.dot_general(a, b, (((1,), (1,)), ((), ())), preferred_element_type=F32)


def _params(sem):
    return pltpu.CompilerParams(dimension_semantics=sem, vmem_limit_bytes=V7X_VMEM_LIMIT_BYTES)


def _full(shape):
    return pl.BlockSpec(shape, lambda *_: (0,) * len(shape))


def _resident(shape):
    return pl.BlockSpec(shape, lambda *_: (0,) * len(shape), pipeline_mode=pl.Buffered(1))


def _memkv_kernel(mem_ref, g_ref, w_ref, kg_ref, k_ref, v_ref):
    seqs, rows, d = mem_ref.shape
    m = mem_ref[...].reshape(seqs * rows, d)
    hn = (m * lax.rsqrt(jnp.mean(m * m, axis=-1, keepdims=True) + EPS) * g_ref[...]).astype(BF16)
    kv = _dot(hn, w_ref[...])
    ks = []
    for h in range(XATT_HEADS):
        kh = kv[:, h * XATT_HEAD_DIM:(h + 1) * XATT_HEAD_DIM]
        ks.append(kh * lax.rsqrt(jnp.mean(kh * kh, axis=-1, keepdims=True) + EPS) * kg_ref[...])
    k_ref[...] = jnp.concatenate(ks, axis=1).astype(BF16).reshape(seqs, rows, XATT_WIDTH)
    v_ref[...] = kv[:, XATT_WIDTH:].astype(BF16).reshape(seqs, rows, XATT_WIDTH)


def _memkv(mem, mem_g, w_mem_kv, xk_g):
    b, m, d = mem.shape
    seqs = max(s for s in (1, 2, 4) if b % s == 0)
    out = jax.ShapeDtypeStruct((b, m, XATT_WIDTH), BF16)
    return pl.pallas_call(
        _memkv_kernel,
        grid=(b // seqs,),
        in_specs=[pl.BlockSpec((seqs, m, d), lambda i: (i, 0, 0)), _full((1, d)),
                  _full((d, 2 * XATT_WIDTH)), _full((1, XATT_HEAD_DIM))],
        out_specs=[pl.BlockSpec((seqs, m, XATT_WIDTH), lambda i: (i, 0, 0))] * 2,
        out_shape=[out, out],
        compiler_params=_params(("arbitrary",)),
        name="memkv",
    )(mem, mem_g, w_mem_kv, xk_g)


def _proj_kernel(xp_ref, x_ref, xn_ref, n1g_ref, win_ref, cw_ref, qg_ref, kg_ref, xqg_ref, kx_ref, vx_ref,
                 wbc_ref, wbx_ref, bd_ref,
                 q0_ref, q1_ref, q2_ref, k0_ref, k1_ref, k2_ref, v0_ref, v1_ref, v2_ref, p_ref, g1_ref,
                 hb_s, u_s, il_s, *, tile, tiles_per_seq):
    tin = pl.program_id(0) % tiles_per_seq
    gain = n1g_ref[...]

    def nrm(x):
        return (x * lax.rsqrt(jnp.mean(x * x, axis=-1, keepdims=True) + EPS) * gain).astype(BF16)

    hb_s[0:HALO, :] = nrm(xp_ref[...])
    hb_s[HALO:HALO + tile, :] = nrm(x_ref[...])
    hb_s[HALO + tile:, :] = nrm(xn_ref[...])
    hc = hb_s[HALO:HALO + tile, :]

    ccx = _dot(hb_s[...], win_ref[:, C_CC:C_AQ])
    u = ccx[:, :CONV_WIDTH] * ccx[:, CONV_WIDTH:]
    u_s[0:HALO, :] = u[0:HALO] * jnp.where(tin == 0, 0.0, 1.0)
    u_s[HALO:HALO + tile, :] = u[HALO:HALO + tile]
    u_s[HALO + tile:, :] = u[HALO + tile:] * jnp.where(tin == tiles_per_seq - 1, 0.0, 1.0)
    cw = cw_ref[...]
    conv = (cw[0:1] * u_s[HALO - 1:HALO - 1 + tile, :] + cw[1:2] * u_s[HALO:HALO + tile, :]
            + cw[2:3] * u_s[HALO + 1:HALO + 1 + tile, :])
    cb = _dot(hc, win_ref[:, C_CB:C_CC])
    y_conv = _dot((cb * conv).astype(BF16), wbc_ref[...])

    qkv = _dot(hc, win_ref[:, C_AQ:C_XQ])
    bd = bd_ref[...]

    def head_norm(z, g_ref):
        outs = []
        for c in range(len(DIL_GROUPS)):
            zc = z[:, c * GROUP_WIDTH:(c + 1) * GROUP_WIDTH]
            ms = _dot((zc * zc).astype(BF16), bd)
            outs.append(zc * lax.rsqrt(ms + EPS) * g_ref[...])
        return outs

    def emit(ref, val, dilation):
        if dilation == 1:
            ref[...] = val.astype(BF16)
            return
        for half in range(GROUP_WIDTH // LANES):
            il_s[half] = val[:, half * LANES:(half + 1) * LANES]
        for r in range(dilation):
            for half in range(GROUP_WIDTH // LANES):
                col = r * GROUP_WIDTH + half * LANES
                ref[:, col:col + LANES] = il_s[half, pl.ds(r, tile // dilation, stride=dilation), :].astype(BF16)

    vals = (head_norm(qkv[:, :ATT_WIDTH], qg_ref) + head_norm(qkv[:, ATT_WIDTH:2 * ATT_WIDTH], kg_ref)
            + [qkv[:, 2 * ATT_WIDTH + c * GROUP_WIDTH:2 * ATT_WIDTH + (c + 1) * GROUP_WIDTH] for c in range(len(DIL_GROUPS))])
    refs = (q0_ref, q1_ref, q2_ref, k0_ref, k1_ref, k2_ref, v0_ref, v1_ref, v2_ref)
    for n, (ref, val) in enumerate(zip(refs, vals)):
        emit(ref, val, DIL_GROUPS[n % len(DIL_GROUPS)][1])

    xq = _dot(hc, win_ref[:, C_XQ:C_GATE])
    kx = kx_ref[...]
    vx = vx_ref[...]
    outs = []
    for h in range(XATT_HEADS):
        hs = slice(h * XATT_HEAD_DIM, (h + 1) * XATT_HEAD_DIM)
        qh = xq[:, hs]
        qh = qh * lax.rsqrt(jnp.mean(qh * qh, axis=-1, keepdims=True) + EPS) * xqg_ref[...]
        s = _dot_nt(qh.astype(BF16), kx[:, hs]) * (XATT_HEAD_DIM ** -0.5)
        p = jnp.exp(s - jnp.max(s, axis=-1, keepdims=True))
        den = jnp.sum(p, axis=-1, keepdims=True)
        outs.append(_dot(p.astype(BF16), vx[:, hs]) / den)
    y_x = _dot(jnp.concatenate(outs, axis=1).astype(BF16), wbx_ref[...])

    gs = jax.nn.sigmoid(_dot(hc, win_ref[:, C_GATE:C_END]))
    p_ref[...] = (gs[:, :D_MODEL] * y_conv + gs[:, 2 * D_MODEL:] * y_x).astype(BF16)
    g1_ref[...] = gs[:, D_MODEL:2 * D_MODEL].astype(BF16)


def _proj(x2, seq_len, kx, vx, w):
    n, d = x2.shape
    tile = PROJ_TILE
    tps = seq_len // tile
    hb = tile // HALO
    n_hblk = n // HALO
    wide = jax.ShapeDtypeStruct((n, d), BF16)
    row = lambda width: pl.BlockSpec((tile, width), lambda i: (i, 0))
    dils = [dil for _, dil in DIL_GROUPS] * 3
    grp_shapes = [jax.ShapeDtypeStruct((n // dil, dil * GROUP_WIDTH), BF16) for dil in dils]
    grp_specs = [pl.BlockSpec((tile // dil, dil * GROUP_WIDTH), lambda i: (i, 0)) for dil in dils]
    return pl.pallas_call(
        functools.partial(_proj_kernel, tile=tile, tiles_per_seq=tps),
        grid=(n // tile,),
        in_specs=[
            pl.BlockSpec((HALO, d), lambda i: (jnp.maximum(i * hb - 1, 0), 0)),
            row(d),
            pl.BlockSpec((HALO, d), lambda i: (jnp.minimum((i + 1) * hb, n_hblk - 1), 0)),
            _full((1, d)), _resident((d, C_END)), _full((3, CONV_WIDTH)),
            _full((1, GROUP_WIDTH)), _full((1, GROUP_WIDTH)), _full((1, XATT_HEAD_DIM)),
            pl.BlockSpec((None, N_MEM, XATT_WIDTH), lambda i: (i // tps, 0, 0)),
            pl.BlockSpec((None, N_MEM, XATT_WIDTH), lambda i: (i // tps, 0, 0)),
            _resident((CONV_WIDTH, d)), _resident((XATT_WIDTH, d)), _full((GROUP_WIDTH, GROUP_WIDTH)),
        ],
        out_specs=grp_specs + [row(d), row(d)],
        out_shape=grp_shapes + [wide, wide],
        scratch_shapes=[pltpu.VMEM((tile + 2 * HALO, d), BF16), pltpu.VMEM((tile + 2 * HALO, CONV_WIDTH), F32),
                        pltpu.VMEM((GROUP_WIDTH // LANES, tile, LANES), F32)],
        compiler_params=_params(("arbitrary",)),
        name="proj",
    )(x2, x2, x2, w["norm1_g"], w["w_in"], w["conv_w"], w["q_gain"], w["k_gain"], w["xq_gain"], kx, vx,
      w["w_br_conv"], w["w_br_xattn"], w["head_avg"])


def _attn_kernel(q_ref, k_ref, v_ref, bias_ref, o_ref, lse_ref, *, seq_sub, lq, qb, kw, n_res):
    i = pl.program_id(2)
    lane_head = lax.broadcasted_iota(I32, (qb, GROUP_WIDTH), 1) // ATT_HEAD_DIM
    head_mask = [lane_head == h for h in range(HEADS_PER_GROUP)]
    head_mask_bf = [jnp.where(m, 1.0, 0.0).astype(BF16) for m in head_mask]

    def block(sb, carry):
        row0 = pl.multiple_of(sb * qb, qb)
        qs = i * lq + row0
        ks = pl.multiple_of(jnp.clip(qs - ATT_RADIUS, 0, seq_sub - kw), ATT_RADIUS)
        case = jnp.where(qs == 0, 0, jnp.where(qs == seq_sub - qb, 2, 1))
        for r in range(n_res):
            cs = slice(r * GROUP_WIDTH, (r + 1) * GROUP_WIDTH)
            q = q_ref[pl.ds(row0, qb), cs]
            kk = k_ref[pl.ds(ks, kw), cs]
            vv = v_ref[pl.ds(ks, kw), cs]
            s = _dot_nt(jnp.concatenate([q * m for m in head_mask_bf], axis=0), kk)
            ps, inv_den, lse = [], [], []
            for h in range(HEADS_PER_GROUP):
                sh = s[h * qb:(h + 1) * qb] + bias_ref[case * HEADS_PER_GROUP + h]
                m = jnp.max(sh, axis=-1, keepdims=True)
                p = jnp.exp(sh - m)
                den = jnp.sum(p, axis=-1, keepdims=True)
                ps.append(p.astype(BF16))
                inv_den.append(1.0 / den)
                lse.append(m + jnp.log(den))
            of = _dot(jnp.concatenate(ps, axis=0), vv)
            o = jnp.zeros((qb, GROUP_WIDTH), F32)
            lb = jnp.zeros((qb, GROUP_WIDTH), F32)
            for h in range(HEADS_PER_GROUP):
                o = jnp.where(head_mask[h], of[h * qb:(h + 1) * qb] * inv_den[h], o)
                lb = jnp.where(head_mask[h], lse[h], lb)
            o_ref[pl.ds(row0, qb), cs] = o.astype(BF16)
            lse_ref[pl.ds(row0, qb), cs] = lb
        return carry

    lax.fori_loop(0, lq // qb, block, 0, unroll=max(1, min(lq // qb, ATT_UNITS // n_res)))


def _attn_bias(dilation, qb, kw, group):
    slopes = jnp.exp2(-ALIBI_MAX_EXP * jnp.arange(1, N_ATT_HEADS + 1, dtype=F32) / N_ATT_HEADS)
    slopes = slopes[group * HEADS_PER_GROUP:(group + 1) * HEADS_PER_GROUP]
    a = jnp.arange(qb)[:, None]
    c = jnp.arange(kw)[None, :]
    tabs = []
    for off in (0, -ATT_RADIUS, qb - kw):
        delta = off + c - a
        dist = (dilation * jnp.abs(delta)).astype(F32)
        bias = -slopes[:, None, None] * dist[None]
        tabs.append(jnp.where((jnp.abs(delta) <= ATT_RADIUS)[None], bias, NEG_INF))
    return jnp.concatenate(tabs, axis=0)


def _attn(q, k, v, batch, seq_len, group):
    _, dilation = DIL_GROUPS[group]
    seq_sub = seq_len // dilation
    qb = min(ATT_QB, seq_sub)
    kw = min(qb + 2 * ATT_RADIUS, seq_sub)
    lq = min(ATT_UNITS * qb, seq_sub)
    bytes_per_res = 2 * GROUP_WIDTH * (lq * (2 + 2 + 4) + seq_sub * (2 + 2))
    n_res = dilation
    while n_res > 1 and n_res * bytes_per_res > ATT_BLOCK_BYTES:
        n_res //= 2
    width = n_res * GROUP_WIDTH
    view = lambda t: t.reshape(batch, seq_sub, dilation * GROUP_WIDTH)
    qspec = pl.BlockSpec((None, lq, width), lambda b, r, i: (b, i, r))
    kspec = pl.BlockSpec((None, seq_sub, width), lambda b, r, i: (b, 0, r))
    bias = _attn_bias(dilation, qb, kw, group)
    o, lse = pl.pallas_call(
        functools.partial(_attn_kernel, seq_sub=seq_sub, lq=lq, qb=qb, kw=kw, n_res=n_res),
        grid=(batch, dilation // n_res, seq_sub // lq),
        in_specs=[qspec, kspec, kspec, _full(bias.shape)],
        out_specs=[qspec, qspec],
        out_shape=[jax.ShapeDtypeStruct((batch, seq_sub, dilation * GROUP_WIDTH), BF16),
                   jax.ShapeDtypeStruct((batch, seq_sub, dilation * GROUP_WIDTH), F32)],
        compiler_params=_params(("arbitrary", "arbitrary", "arbitrary")),
        name=f"attn_d{dilation}",
    )(view(q), view(k), view(v), bias)
    rows = batch * seq_sub
    return o.reshape(rows, dilation * GROUP_WIDTH), lse.reshape(rows, dilation * GROUP_WIDTH)


def _merge_kernel(x_ref, p_ref, g1_ref, o0_ref, o1_ref, o2_ref, l0_ref, l1_ref, l2_ref, wba_ref, wo_ref,
                  n2g_ref, wr_ref, x1_ref, h2e_ref, afft_ref, *il_s, tile):
    halves = GROUP_WIDTH // LANES
    dils = [dil for _, dil in DIL_GROUPS] * 2
    grp_refs = (o0_ref, o1_ref, o2_ref, l0_ref, l1_ref, l2_ref)

    for ref, dilation, scratch in zip(grp_refs, dils, il_s):
        for r in range(dilation if dilation > 1 else 0):
            for half in range(halves):
                col = r * GROUP_WIDTH + half * LANES
                scratch[half, pl.ds(r, tile // dilation, stride=dilation), :] = ref[:, col:col + LANES].astype(F32)

    def token_order(n, rs):
        if dils[n] == 1:
            return grp_refs[n][rs, :].astype(F32)
        return jnp.concatenate([il_s[n][half, rs, :] for half in range(halves)], axis=1)

    rows = tile // MERGE_SPLIT
    blocks = [slice(blk * rows, (blk + 1) * rows) for blk in range(MERGE_SPLIT)]

    def mixture(rs):
        o0, o1, o2, l0, l1, l2 = (token_order(n, rs) for n in range(6))
        m = jnp.maximum(jnp.maximum(l0, l1), l2)
        e0, e1, e2 = jnp.exp(l0 - m), jnp.exp(l1 - m), jnp.exp(l2 - m)
        return ((e0 * o0 + e1 * o1 + e2 * o2) / (e0 + e1 + e2)).astype(BF16)

    mixed = [mixture(rs) for rs in blocks]
    y_attn = [_dot(o, wba_ref[...]) for o in mixed]
    merged = [(p_ref[rs, :].astype(F32) + g1_ref[rs, :].astype(F32) * y).astype(BF16) for rs, y in zip(blocks, y_attn)]
    x1s = [x_ref[rs, :] + _dot(mg, wo_ref[...]) for rs, mg in zip(blocks, merged)]
    h2s = []
    for rs, x1 in zip(blocks, x1s):
        x1_ref[rs, :] = x1
        h2 = (x1 * lax.rsqrt(jnp.mean(x1 * x1, axis=-1, keepdims=True) + EPS) * n2g_ref[...]).astype(BF16)
        h2e_ref[rs, :D_MODEL] = h2
        h2s.append(h2)
    logits = [_dot(h2, wr_ref[...]) for h2 in h2s]
    for rs, lg in zip(blocks, logits):
        lane = lax.broadcasted_iota(I32, lg.shape, 1)
        first = lane < N_EXPERTS
        mx = jnp.max(jnp.where(first, lg, -jnp.inf), axis=-1, keepdims=True)
        ex = jnp.exp(lg - mx)
        a = ex / jnp.sum(jnp.where(first, ex, 0.0), axis=-1, keepdims=True)
        afft_ref[:, rs] = a.T[:N_EXPERTS, :]
        hi = a.astype(BF16).astype(F32)
        mid = (a - hi).astype(BF16).astype(F32)
        lo = (a - hi) - mid
        ext = jnp.where(first, hi, jnp.where(lane < 2 * N_EXPERTS, mid, jnp.where(lane < 3 * N_EXPERTS, lo, 0.0)))
        h2e_ref[rs, D_MODEL:] = ext.astype(BF16)


def _merge(x2, p, g1, os_, ls_, w):
    n, d = x2.shape
    tile = PROJ_TILE
    row = lambda width: pl.BlockSpec((tile, width), lambda i: (i, 0))
    grp = [pl.BlockSpec((tile // dil, dil * GROUP_WIDTH), lambda i: (i, 0)) for _, dil in DIL_GROUPS]
    return pl.pallas_call(
        functools.partial(_merge_kernel, tile=tile),
        grid=(n // tile,),
        in_specs=[row(d), row(d), row(d)] + grp + grp
                 + [_full((GROUP_WIDTH, d)), _full((d, d)), _full((1, d)), _full((d, LANES))],
        out_specs=[row(d), row(d + LANES), pl.BlockSpec((N_EXPERTS, tile), lambda i: (0, i))],
        out_shape=[jax.ShapeDtypeStruct((n, d), F32), jax.ShapeDtypeStruct((n, d + LANES), BF16),
                   jax.ShapeDtypeStruct((N_EXPERTS, n), F32)],
        scratch_shapes=[pltpu.VMEM((GROUP_WIDTH // LANES, tile, LANES), F32)] * 6,
        compiler_params=_params(("arbitrary",)),
        name="merge",
    )(x2, p, g1, *os_, *ls_, w["w_br_attn"], w["w_o"], w["norm2_g"], w["w_router3"])


def _select_kernel(aff_ref, slot_ref, starts_ref, nchunk_ref, tot_ref, *, n_tok, cap, tile):
    n_tiles = n_tok // tile

    def bisect(b, thr_bits):
        cand = thr_bits | jnp.left_shift(jnp.int32(1), 30 - b)
        cnt = jnp.sum(jnp.where(aff_ref[...] >= lax.bitcast_convert_type(cand, F32), 1.0, 0.0), axis=1, keepdims=True)
        return jnp.where(cnt >= cap, cand, thr_bits)

    thr = lax.bitcast_convert_type(lax.fori_loop(0, 31, bisect, jnp.zeros((N_EXPERTS, 1), I32)), F32)
    need = cap - jnp.sum(jnp.where(aff_ref[...] > thr, 1.0, 0.0), axis=1, keepdims=True)

    r = lax.broadcasted_iota(I32, (tile, tile), 0)
    c = lax.broadcasted_iota(I32, (tile, tile), 1)
    before = jnp.where(r < c, 1.0, 0.0).astype(BF16)

    def tile_body(i, carry):
        start, eq_seen = carry
        off = pl.multiple_of(i * tile, tile)
        a = aff_ref[:, pl.ds(off, tile)]
        eq = jnp.where(a == thr, 1.0, 0.0)
        eq_rank = eq_seen + _dot(eq.astype(BF16), before)
        sel = jnp.where(a > thr, 1.0, jnp.where(eq_rank < need, eq, 0.0))
        rank = _dot(sel.astype(BF16), before)
        slot_ref[:, pl.ds(off, tile)] = jnp.where(sel > 0.0, rank, -1.0).astype(I32)
        cnt = jnp.sum(sel, axis=1, keepdims=True)
        starts_ref[i] = jnp.broadcast_to(start, (N_EXPERTS, LANES)).astype(I32)
        head = start - jnp.floor(start * (1.0 / SEG_ALIGN)) * SEG_ALIGN
        nch = jnp.max(jnp.floor((head + cnt) * (1.0 / ROUTE_CHUNK)) + 1.0, axis=0, keepdims=True)
        nchunk_ref[i] = jnp.broadcast_to(nch, (8, LANES)).astype(I32)
        return start + cnt, eq_seen + jnp.sum(eq, axis=1, keepdims=True)

    zero = jnp.zeros((N_EXPERTS, 1), F32)
    total, _ = lax.fori_loop(0, n_tiles, tile_body, (zero, zero), unroll=8)
    tot_ref[...] = jnp.broadcast_to(total, (N_EXPERTS, LANES)).astype(I32)


def _select(aff_t, cap):
    n_exp, n = aff_t.shape
    tile = ROUTE_TILE
    n_tiles = n // tile
    slot, starts, nchunk, tot = pl.pallas_call(
        functools.partial(_select_kernel, n_tok=n, cap=cap, tile=tile),
        grid=(1,),
        in_specs=[_full((n_exp, n))],
        out_specs=[_full((n_exp, n)), _full((n_tiles, n_exp, LANES)), _full((n_tiles, 8, LANES)), _full((n_exp, LANES))],
        out_shape=[jax.ShapeDtypeStruct((n_exp, n), I32), jax.ShapeDtypeStruct((n_tiles, n_exp, LANES), I32),
                   jax.ShapeDtypeStruct((n_tiles, 8, LANES), I32), jax.ShapeDtypeStruct((n_exp, LANES), I32)],
        compiler_params=_params(("arbitrary",)),
        name="select",
    )(aff_t)
    return slot, starts[:, :, 0], nchunk[:, 0, 0], tot[:, 0]


def _seg_base(starts_ref, step, e):
    return pl.multiple_of((starts_ref[step * N_EXPERTS + e] // SEG_ALIGN) * SEG_ALIGN, SEG_ALIGN)


def _dispatch_kernel(starts_ref, nchunk_ref, slot_ref, h_ref, scol_ref, xe_hbm, stage, head, sem, *, tile, chunk):
    i = pl.program_id(0)
    sl = i % 2

    def copies(step, c, buf):
        return [pltpu.make_async_copy(
            stage.at[buf, pl.ds(e * chunk, chunk), :],
            xe_hbm.at[e, pl.ds(_seg_base(starts_ref, step, e) + c * chunk, chunk), :],
            sem.at[buf]) for e in range(N_EXPERTS)]

    @pl.when(i == 0)
    def _():
        head[...] = jnp.zeros(head.shape, BF16)

    slot = slot_ref[...]
    pos = jnp.where(slot >= 0, slot + (scol_ref[...] & (SEG_ALIGN - 1)), -1)

    def build(c, buf):
        want = lax.broadcasted_iota(I32, (chunk, tile), 0) + c * chunk
        onehot = jnp.concatenate(
            [jnp.where(pos[e:e + 1, :] == want, 1.0, 0.0).astype(BF16) for e in range(N_EXPERTS)], axis=0)
        stage[buf] = _dot(onehot, h_ref[...]).astype(BF16)

    nxt = jnp.minimum(i + 1, pl.num_programs(0) - 1)
    rel = [starts_ref[nxt * N_EXPERTS + e] - _seg_base(starts_ref, i, e) for e in range(N_EXPERTS)]

    def keep_head(e, buf):
        row = pl.multiple_of(e * chunk + ((rel[e] % chunk) // SEG_ALIGN) * SEG_ALIGN, SEG_ALIGN)
        head[e] = stage[buf, pl.ds(row, SEG_ALIGN), :]

    build(0, sl)
    for e in range(N_EXPERTS):
        stage[sl, e * chunk:e * chunk + SEG_ALIGN, :] = stage[sl, e * chunk:e * chunk + SEG_ALIGN, :] + head[e]
    for e in range(N_EXPERTS):
        keep_head(e, sl)

    @pl.when(i > 0)
    def _():
        for cp in copies(i - 1, 0, 1 - sl):
            cp.wait()

    for cp in copies(i, 0, sl):
        cp.start()

    def overflow(c, carry):
        for cp in copies(i, c - 1, sl):
            cp.wait()
        build(c, sl)
        for e in range(N_EXPERTS):
            @pl.when(rel[e] // chunk == c)
            def _():
                keep_head(e, sl)
        for cp in copies(i, c, sl):
            cp.start()
        return carry

    lax.fori_loop(1, nchunk_ref[i], overflow, 0)

    @pl.when(i == pl.num_programs(0) - 1)
    def _():
        for cp in copies(i, 0, sl):
            cp.wait()
        rows = xe_hbm.shape[1]
        zbuf = 1 - sl
        stage[zbuf] = jnp.zeros(stage.shape[1:], BF16)
        end = [_seg_base(starts_ref, i, e) + jnp.maximum(nchunk_ref[i], 1) * chunk for e in range(N_EXPERTS)]
        n_full = [(rows - end[e]) // chunk for e in range(N_EXPERTS)]

        def zero_copy(e, pos):
            return pltpu.make_async_copy(stage.at[zbuf, pl.ds(e * chunk, chunk), :],
                                         xe_hbm.at[e, pl.ds(pl.multiple_of(pos, SEG_ALIGN), chunk), :], sem.at[zbuf])

        def fill(wait):
            def body(k, carry):
                for e in range(N_EXPERTS):
                    @pl.when(k < n_full[e])
                    def _():
                        cp = zero_copy(e, end[e] + k * chunk)
                        cp.wait() if wait else cp.start()
                return carry
            return body

        most = functools.reduce(jnp.maximum, n_full)
        lax.fori_loop(0, most, fill(wait=False), 0)
        lax.fori_loop(0, most, fill(wait=True), 0)
        for e in range(N_EXPERTS):
            zero_copy(e, rows - chunk).start()
        for e in range(N_EXPERTS):
            zero_copy(e, rows - chunk).wait()


def _dispatch(slot, h2e, starts, nchunk, rows):
    n = h2e.shape[0]
    tile, chunk = ROUTE_TILE, ROUTE_CHUNK
    n_tiles = n // tile
    return pl.pallas_call(
        functools.partial(_dispatch_kernel, tile=tile, chunk=chunk),
        grid_spec=pltpu.PrefetchScalarGridSpec(
            num_scalar_prefetch=2, grid=(n_tiles,),
            in_specs=[pl.BlockSpec((N_EXPERTS, tile), lambda i, *_: (0, i)),
                      pl.BlockSpec((tile, D_MODEL + LANES), lambda i, *_: (i, 0)),
                      pl.BlockSpec((None, N_EXPERTS, 1), lambda i, *_: (i, 0, 0))],
            out_specs=pl.BlockSpec(memory_space=pl.ANY),
            scratch_shapes=[pltpu.VMEM((2, N_EXPERTS * chunk, XE_W), BF16),
                            pltpu.VMEM((N_EXPERTS, SEG_ALIGN, XE_W), BF16), pltpu.SemaphoreType.DMA((2,))]),
        out_shape=jax.ShapeDtypeStruct((N_EXPERTS, rows, XE_W), BF16),
        compiler_params=_params(("arbitrary",)),
        name="dispatch",
    )(starts.reshape(-1), nchunk, slot, h2e, starts.reshape(n_tiles, N_EXPERTS, 1))


def _ffn_kernel(tot_ref, *refs, tile, first_tile):
    n_grp = len(first_tile) - 1
    xe_refs, (wg_ref, wu_ref, wd_ref) = refs[:n_grp], refs[n_grp:n_grp + 3]
    ye_refs, (wg_s, wu_s, wd_s) = refs[n_grp + 3:2 * n_grp + 3], refs[2 * n_grp + 3:]
    e = pl.program_id(0)
    j = pl.program_id(1)

    def weights(fresh):
        def load(w_ref, w_s, idx):
            if not fresh:
                return w_s[idx]
            w = w_ref[idx].astype(BF16)
            w_s[idx] = w
            return w
        return load

    def run(xe_ref, ye_ref, live, fresh):
        load = weights(fresh)

        @pl.when(live > 0)
        def _():
            x = xe_ref[:, :D_MODEL]
            pieces = xe_ref[:, D_MODEL:].astype(F32)
            lane = lax.broadcasted_iota(I32, pieces.shape, 1)
            mine = ((lane & (N_EXPERTS - 1)) == e) & (lane < 3 * N_EXPERTS)
            gate = jnp.sum(jnp.where(mine, pieces, 0.0), axis=-1, keepdims=True)
            hid = []
            for fc in range(wg_s.shape[1] // FFN_COLS):
                cs = (slice(None), slice(fc * FFN_COLS, (fc + 1) * FFN_COLS))
                g = _dot(x, load(wg_ref, wg_s, cs))
                u = _dot(x, load(wu_ref, wu_s, cs))
                hid.append((g * jax.nn.sigmoid(g) * u).astype(BF16))
            y = _dot(jnp.concatenate(hid, axis=1), load(wd_ref, wd_s, (slice(None), slice(None)))) * gate
            rows = lax.broadcasted_iota(I32, y.shape, 0)
            ye_ref[...] = jnp.where(rows < live, y, 0.0).astype(BF16)

        @pl.when(live <= 0)
        def _():
            ye_ref[...] = jnp.zeros(ye_ref.shape, BF16)

    for g in range(n_grp):
        live = tot_ref[g * N_EXPERTS + e] - (j - first_tile[g]) * tile
        if g == 0:
            @pl.when(j == 0)
            def _():
                run(xe_refs[0], ye_refs[0], live, fresh=True)

        @pl.when((j >= max(first_tile[g], 1)) & (j < first_tile[g + 1]))
        def _():
            run(xe_refs[g], ye_refs[g], live, fresh=False)


def _ffn(xes, tots, w_gate, w_up, w_down):
    tile = FFN_TILE
    n_exp, d, f = w_gate.shape
    n_tiles = [xe.shape[1] // tile for xe in xes]
    first_tile = [sum(n_tiles[:g]) for g in range(len(xes) + 1)]

    def xe_spec(g):
        def index(e, j, tot):
            last = jnp.maximum(tot[g * N_EXPERTS + e] - 1, 0) // tile
            return (e, jnp.clip(j - first_tile[g], 0, last), 0)
        return pl.BlockSpec((None, tile, XE_W), index)

    def ye_spec(g):
        return pl.BlockSpec((None, tile, d), lambda e, j, tot: (e, jnp.clip(j - first_tile[g], 0, n_tiles[g] - 1), 0))

    wspec = lambda a, b: pl.BlockSpec((None, a, b), lambda e, j, tot: (e, 0, 0))
    return pl.pallas_call(
        functools.partial(_ffn_kernel, tile=tile, first_tile=tuple(first_tile)),
        grid_spec=pltpu.PrefetchScalarGridSpec(
            num_scalar_prefetch=1, grid=(n_exp, first_tile[-1]),
            in_specs=[xe_spec(g) for g in range(len(xes))] + [wspec(d, f), wspec(d, f), wspec(f, d)],
            out_specs=[ye_spec(g) for g in range(len(xes))],
            scratch_shapes=[pltpu.VMEM((d, f), BF16), pltpu.VMEM((d, f), BF16), pltpu.VMEM((f, d), BF16)]),
        out_shape=[jax.ShapeDtypeStruct((n_exp, xe.shape[1], d), BF16) for xe in xes],
        compiler_params=_params(("arbitrary", "arbitrary")),
        name="ffn",
    )(jnp.concatenate(tots), *xes, w_gate, w_up, w_down)


def _combine_kernel(starts_ref, nchunk_ref, x1_hbm, slot_ref, srow_ref, ye_hbm, out_ref, stage, sem, xbuf, xsem, *,
                    tile, chunk):
    i = pl.program_id(0)
    sl = i % 2
    over = 2
    ring = xbuf.shape[0]

    def x_copy(step):
        return pltpu.make_async_copy(x1_hbm.at[pl.ds(pl.multiple_of(step * tile, tile), tile), :],
                                     xbuf.at[step % ring], xsem.at[step % ring])

    @pl.when(i == 0)
    def _():
        x_copy(0).start()

        @pl.when(pl.num_programs(0) > 1)
        def _():
            x_copy(1).start()

    @pl.when(i + ring - 1 < pl.num_programs(0))
    def _():
        x_copy(i + ring - 1).start()

    def copies(step, c, buf):
        return [pltpu.make_async_copy(
            ye_hbm.at[e, pl.ds(_seg_base(starts_ref, step, e) + c * chunk, chunk), :],
            stage.at[buf, pl.ds(e * chunk, chunk), :],
            sem.at[buf]) for e in range(N_EXPERTS)]

    @pl.when(i == 0)
    def _():
        for cp in copies(0, 0, 0):
            cp.start()

    @pl.when(i + 1 < pl.num_programs(0))
    def _():
        for cp in copies(i + 1, 0, 1 - sl):
            cp.start()

    col = lax.broadcasted_iota(I32, (N_EXPERTS, N_EXPERTS * chunk), 1)
    exp_row = lax.broadcasted_iota(I32, (N_EXPERTS, N_EXPERTS * chunk), 0)
    spread = jnp.where(col // chunk == exp_row, 1.0, 0.0).astype(BF16)
    slot = slot_ref[...]
    rank = jnp.where(slot < 0, -float(tile), slot.astype(F32)).astype(BF16)
    head = jnp.broadcast_to((srow_ref[...] & (SEG_ALIGN - 1)).astype(F32).astype(BF16), (8, N_EXPERTS))
    slots = _dot(rank, spread) + _dot(head, spread)[0:1]
    lane_slot = (lax.broadcasted_iota(I32, slots.shape, 1) & (chunk - 1)).astype(F32)

    def gathered(c, buf):
        onehot = jnp.where(slots == lane_slot + c * chunk, 1.0, 0.0).astype(BF16)
        return _dot(onehot, stage[buf])

    for cp in copies(i, 0, sl):
        cp.wait()
    x_copy(i).wait()
    out_ref[...] = xbuf[i % ring] + gathered(0, sl)

    def overflow(c, carry):
        for cp in copies(i, c, over):
            cp.start()
        for cp in copies(i, c, over):
            cp.wait()
        out_ref[...] += gathered(c, over)
        return carry

    lax.fori_loop(1, nchunk_ref[i], overflow, 0)


def _combine(x1, slot_t, ye, starts, nchunk):
    n, d = x1.shape
    tile, chunk = ROUTE_TILE, ROUTE_CHUNK
    n_tiles = n // tile
    return pl.pallas_call(
        functools.partial(_combine_kernel, tile=tile, chunk=chunk),
        grid_spec=pltpu.PrefetchScalarGridSpec(
            num_scalar_prefetch=2, grid=(n_tiles,),
            in_specs=[pl.BlockSpec(memory_space=pl.ANY),
                      pl.BlockSpec((tile, N_EXPERTS), lambda i, *_: (i, 0)),
                      pl.BlockSpec((None, 1, N_EXPERTS), lambda i, *_: (i, 0, 0)),
                      pl.BlockSpec(memory_space=pl.ANY)],
            out_specs=pl.BlockSpec((tile, d), lambda i, *_: (i, 0)),
            scratch_shapes=[pltpu.VMEM((3, N_EXPERTS * chunk, d), BF16), pltpu.SemaphoreType.DMA((3,)),
                            pltpu.VMEM((3, tile, d), F32), pltpu.SemaphoreType.DMA((3,))]),
        out_shape=jax.ShapeDtypeStruct((n, d), F32),
        compiler_params=_params(("arbitrary",)),
        name="combine",
    )(starts.reshape(-1), nchunk, x1, slot_t, starts.reshape(n_tiles, 1, N_EXPERTS), ye)


def _expert_rows(cap):
    need = cap + ROUTE_TILE + 2 * ROUTE_CHUNK
    return -(-need // FFN_TILE) * FFN_TILE


def _routed_ffn(groups, w):
    routes = []
    for x1, h2e, aff_t in groups:
        cap = max(1, EC_CAPACITY * x1.shape[0] // N_EXPERTS)
        slot, starts, nchunk, tot = _select(aff_t, cap)
        routes.append((slot, starts, nchunk, tot, _dispatch(slot, h2e, starts, nchunk, _expert_rows(cap))))
    yes = _ffn([r[4] for r in routes], [r[3] for r in routes], w["w_exp_gate"], w["w_exp_up"], w["w_exp_down"])
    return [_combine(x1, slot.T, ye, starts, nchunk)
            for (x1, _, _), (slot, starts, nchunk, _, _), ye in zip(groups, routes, yes)]


def _mixers(x, mem, w):
    b, s, d = x.shape
    x2 = x.reshape(b * s, d)
    kx, vx = _memkv(mem, w["mem_norm_g"], w["w_mem_kv"], w["xk_gain"])
    *qkv, p, g1 = _proj(x2, s, kx, vx, w)
    os_, ls_ = [], []
    for g in range(len(DIL_GROUPS)):
        o, lse = _attn(qkv[g], qkv[3 + g], qkv[6 + g], b, s, g)
        os_.append(o)
        ls_.append(lse)
    return _merge(x2, p, g1, os_, ls_, w)


def _layer(xs, mems, w):
    outs = _routed_ffn([_mixers(x, mem, w) for x, mem in zip(xs, mems)], w)
    return [o.reshape(x.shape) for o, x in zip(outs, xs)]


def _prepare(norm1_g, w_in, conv_w, q_norm_g, k_norm_g, mem_norm_g, w_mem_kv, xq_norm_g, xk_norm_g,
             w_br_conv, w_br_attn, w_br_xattn, w_o, norm2_g, w_router, w_exp_gate, w_exp_up, w_exp_down):
    row = lambda v: v.reshape(1, -1).astype(F32)
    head = jnp.arange(GROUP_WIDTH) // ATT_HEAD_DIM
    return {
        "norm1_g": row(norm1_g), "norm2_g": row(norm2_g), "mem_norm_g": row(mem_norm_g),
        "w_in": w_in.astype(BF16), "conv_w": conv_w.astype(F32),
        "q_gain": row(jnp.tile(q_norm_g, HEADS_PER_GROUP) * (ATT_HEAD_DIM ** -0.5)),
        "k_gain": row(jnp.tile(k_norm_g, HEADS_PER_GROUP)),
        "xq_gain": row(xq_norm_g), "xk_gain": row(xk_norm_g),
        "w_mem_kv": w_mem_kv.astype(BF16), "w_br_conv": w_br_conv.astype(BF16),
        "w_br_attn": w_br_attn.astype(BF16), "w_br_xattn": w_br_xattn.astype(BF16), "w_o": w_o.astype(BF16),
        "head_avg": jnp.where(head[:, None] == head[None, :], 1.0 / ATT_HEAD_DIM, 0.0).astype(BF16),
        "w_router3": jnp.concatenate([w_router] * 3 + [jnp.zeros((D_MODEL, LANES - 3 * N_EXPERTS), F32)],
                                     axis=1).astype(BF16),
        "w_exp_gate": w_exp_gate, "w_exp_up": w_exp_up, "w_exp_down": w_exp_down,
    }


def kernel(x_prompt, x_sample, mem_prompt, mem_sample, norm1_g, w_in, conv_w, q_norm_g, k_norm_g, mem_norm_g, w_mem_kv, xq_norm_g, xk_norm_g, w_br_conv, w_br_attn, w_br_xattn, w_o, norm2_g, w_router, w_exp_gate, w_exp_up, w_exp_down):
    per_layer = (norm1_g, w_in, conv_w, q_norm_g, k_norm_g, mem_norm_g, w_mem_kv, xq_norm_g, xk_norm_g,
                 w_br_conv, w_br_attn, w_br_xattn, w_o, norm2_g, w_router, w_exp_gate, w_exp_up, w_exp_down)
    ys = [x_prompt, x_sample]
    for layer in range(norm1_g.shape[0]):
        ys = _layer(ys, (mem_prompt, mem_sample), _prepare(*(t[layer] for t in per_layer)))
    return tuple(ys)
```
